```python
import jax, jax.numpy as jnp
from jax import lax
import numpy as np

D_MODEL = 2048
BATCH = 4
SEQ = 4096
DEPTH = 2

CONV_GROUPS = 8
CONV_WIDTH = D_MODEL // 2
CONV_K = 3
SGU_GROUPS = 8
SGU_WIDTH = D_MODEL // 2
SGU_GROUP_DIM = SGU_WIDTH // SGU_GROUPS
CHUNK = 128
N_BRANCH = 2
BRANCH_WIDTH = CONV_WIDTH
D_IN = 3 * CONV_WIDTH + 2 * SGU_WIDTH + N_BRANCH * D_MODEL
N_EXPERT_GROUPS = 4
EXPERTS_PER_GROUP = 8
N_EXPERTS = N_EXPERT_GROUPS * EXPERTS_PER_GROUP
TOP_K = 2
D_EXPERT = D_MODEL // 4
MOE_BLOCK = 128
EPS = 1e-6

kernel_name = "hybrid_conv_sgu_hmoe_block"


def rmsnorm(x, g):
    xf = x.astype(jnp.float32)
    y = xf * lax.rsqrt(jnp.mean(xf * xf, axis=-1, keepdims=True) + EPS)
    return (y * g.astype(jnp.float32)).astype(x.dtype)


def layernorm(x, g, b):
    xf = x.astype(jnp.float32)
    mu = jnp.mean(xf, axis=-1, keepdims=True)
    xc = xf - mu
    var = jnp.mean(xc * xc, axis=-1, keepdims=True)
    y = xc * lax.rsqrt(var + EPS)
    return (y * g.astype(jnp.float32) + b.astype(jnp.float32)).astype(x.dtype)


def hybrid_mixer(xn, w_in, b_gate, conv_w, ln_v_g, ln_v_b, sgu_w, sgu_b, w_branch, w_out):
    bsz, s, _ = xn.shape
    proj = xn @ w_in
    c_pre, b_post, xa, z, gates = jnp.split(
        proj, [CONV_WIDTH, 2 * CONV_WIDTH, 3 * CONV_WIDTH, 3 * CONV_WIDTH + 2 * SGU_WIDTH], axis=-1)

    xc = c_pre * xa
    xp = jnp.pad(xc, ((0, 0), (CONV_K - 1, 0), (0, 0)))
    conv = conv_w[0] * xp[:, 0:s]
    for k in range(1, CONV_K):
        conv = conv + conv_w[k] * xp[:, k:k + s]
    y_a = b_post * conv

    z = jax.nn.gelu(z)
    u, v = jnp.split(z, 2, axis=-1)
    v = layernorm(v, ln_v_g, ln_v_b)
    v = v.reshape(bsz, s // CHUNK, CHUNK, SGU_GROUPS, SGU_GROUP_DIM)
    causal = jnp.tril(jnp.ones((CHUNK, CHUNK), dtype=bool))
    ws = jnp.where(causal[None], sgu_w, jnp.zeros_like(sgu_w))
    mixed = jnp.einsum('gts,bnsgc->bntgc', ws, v) + sgu_b.T[:, :, None]
    y_b = u * mixed.reshape(bsz, s, SGU_WIDTH)

    branches = jnp.stack([y_a, y_b], axis=2)
    br = jnp.einsum('bsrw,rwd->bsrd', branches, w_branch)
    g = jax.nn.sigmoid(gates + b_gate).reshape(bsz, s, N_BRANCH, D_MODEL)
    merged = jnp.sum(g * br, axis=2)
    return merged @ w_out


def hier_moe(xn, router_g, router_g_b, router_e, router_e_b, w_gate, w_up, w_down):
    bsz, s, d = xn.shape
    n = bsz * s
    xt = xn.reshape(n, d)
    g_logits = (xt @ router_g + router_g_b).astype(jnp.float32)
    g_prob = jax.nn.softmax(g_logits, axis=-1)
    g_idx = jnp.argmax(g_logits, axis=-1)
    g_w = jnp.take_along_axis(g_prob, g_idx[:, None], axis=1)[:, 0]
    e_logits = (xt @ router_e + router_e_b).astype(jnp.float32).reshape(n, N_EXPERT_GROUPS, EXPERTS_PER_GROUP)
    e_sel = jnp.take_along_axis(e_logits, g_idx[:, None, None], axis=1)[:, 0]
    e_prob = jax.nn.softmax(e_sel, axis=-1)
    top_w, top_i = lax.top_k(e_prob, TOP_K)
    top_w = top_w / jnp.sum(top_w, axis=-1, keepdims=True)
    expert_id = (g_idx[:, None] * EXPERTS_PER_GROUP + top_i).astype(jnp.int32)
    weight = (g_w[:, None] * top_w).astype(xt.dtype)

    a = n * TOP_K
    flat_e = expert_id.reshape(a)
    flat_tok = jnp.repeat(jnp.arange(n, dtype=jnp.int32), TOP_K)
    flat_w = weight.reshape(a)
    order = jnp.argsort(flat_e)
    sorted_e = flat_e[order]
    counts = jnp.bincount(flat_e, length=N_EXPERTS).astype(jnp.int32)
    start = jnp.cumsum(counts) - counts
    padded = (counts + MOE_BLOCK - 1) // MOE_BLOCK * MOE_BLOCK
    pad_end = jnp.cumsum(padded)
    pad_start = pad_end - padded
    rank = jnp.arange(a, dtype=jnp.int32) - start[sorted_e]
    dest = pad_start[sorted_e] + rank
    p = -(-a // MOE_BLOCK) * MOE_BLOCK + N_EXPERTS * MOE_BLOCK
    buf_tok = jnp.zeros((p,), jnp.int32).at[dest].set(flat_tok[order])
    buf_w = jnp.zeros((p,), xt.dtype).at[dest].set(flat_w[order])
    n_blocks = p // MOE_BLOCK
    block_e = jnp.minimum(
        jnp.searchsorted(pad_end, jnp.arange(n_blocks, dtype=jnp.int32) * MOE_BLOCK, side='right'),
        N_EXPERTS - 1)
    xb = xt[buf_tok].reshape(n_blocks, MOE_BLOCK, d)

    def expert_block(args):
        xblk, e = args
        h = jax.nn.silu(xblk @ w_gate[e]) * (xblk @ w_up[e])
        return h @ w_down[e]

    yb = lax.map(expert_block, (xb, block_e))
    y = jax.ops.segment_sum(yb.reshape(p, d) * buf_w[:, None], buf_tok, num_segments=n)
    return y.reshape(bsz, s, d)


def setup_inputs(seed: int = 0) -> dict:
    key = jax.random.key(seed)
    ks = jax.random.split(key, 24)

    def nrm(k, shape, scale):
        return jax.random.normal(k, shape, jnp.float32) * scale

    d = D_MODEL
    return {
        "x": nrm(ks[0], (BATCH, SEQ, d), 1.0),
        "norm1_g": 1.0 + nrm(ks[1], (DEPTH, d), 0.02),
        "w_in": nrm(ks[2], (DEPTH, d, D_IN), d ** -0.5),
        "b_gate": nrm(ks[3], (DEPTH, N_BRANCH * d), 0.02),
        "conv_w": nrm(ks[4], (DEPTH, CONV_K, CONV_WIDTH), CONV_K ** -0.5),
        "ln_v_g": 1.0 + nrm(ks[5], (DEPTH, SGU_WIDTH), 0.02),
        "ln_v_b": nrm(ks[6], (DEPTH, SGU_WIDTH), 0.02),
        "sgu_w": nrm(ks[7], (DEPTH, SGU_GROUPS, CHUNK, CHUNK), CHUNK ** -0.5),
        "sgu_b": 1.0 + nrm(ks[8], (DEPTH, SGU_GROUPS, CHUNK), 0.02),
        "w_branch": nrm(ks[9], (DEPTH, N_BRANCH, BRANCH_WIDTH, d), BRANCH_WIDTH ** -0.5),
        "w_out": nrm(ks[10], (DEPTH, d, d), d ** -0.5),
        "norm2_g": 1.0 + nrm(ks[11], (DEPTH, d), 0.02),
        "router_g": nrm(ks[12], (DEPTH, d, N_EXPERT_GROUPS), d ** -0.5),
        "router_g_b": nrm(ks[13], (DEPTH, N_EXPERT_GROUPS), 0.01),
        "router_e": nrm(ks[14], (DEPTH, d, N_EXPERTS), d ** -0.5),
        "router_e_b": nrm(ks[15], (DEPTH, N_EXPERTS), 0.01),
        "w_gate": nrm(ks[16], (DEPTH, N_EXPERTS, d, D_EXPERT), d ** -0.5),
        "w_up": nrm(ks[17], (DEPTH, N_EXPERTS, d, D_EXPERT), d ** -0.5),
        "w_down": nrm(ks[18], (DEPTH, N_EXPERTS, D_EXPERT, d), D_EXPERT ** -0.5),
        "final_g": 1.0 + nrm(ks[19], (d,), 0.02),
    }


def reference(x, norm1_g, w_in, b_gate, conv_w, ln_v_g, ln_v_b, sgu_w, sgu_b, w_branch, w_out,
              norm2_g, router_g, router_g_b, router_e, router_e_b, w_gate, w_up, w_down, final_g):
    h = x
    for l in range(DEPTH):
        h = h + hybrid_mixer(rmsnorm(h, norm1_g[l]), w_in[l], b_gate[l], conv_w[l], ln_v_g[l], ln_v_b[l],
                             sgu_w[l], sgu_b[l], w_branch[l], w_out[l])
        h = h + hier_moe(rmsnorm(h, norm2_g[l]), router_g[l], router_g_b[l], router_e[l], router_e_b[l],
                         w_gate[l], w_up[l], w_down[l])
    return rmsnorm(h, final_g)
```

```python
import functools

import jax
import jax.numpy as jnp
from jax import lax
from jax.experimental import pallas as pl
from jax.experimental.pallas import tpu as pltpu

F32 = jnp.float32
BF16 = jnp.bfloat16

D_MODEL = 2048
BATCH = 4
SEQ = 4096
DEPTH = 2
N_TOK = BATCH * SEQ
WIDTH = D_MODEL // 2
CONV_K = 3
SGU_GROUPS = 8
CHUNK = 128
GROUP_DIM = WIDTH // SGU_GROUPS
D_IN = 3 * WIDTH + 2 * WIDTH + 2 * D_MODEL
N_GROUPS = 4
PER_GROUP = 8
N_EXPERTS = N_GROUPS * PER_GROUP
TOP_K = 2
D_EXPERT = D_MODEL // 4
EPS = 1e-6

LANES = 128
SUBLANES = 8
VMEM_LIMIT = 56 * 1024 * 1024

COL_C, COL_B, COL_XA = 0, WIDTH, 2 * WIDTH
COL_U, COL_V = 3 * WIDTH, 4 * WIDTH
COL_GA, COL_GB = 5 * WIDTH, 5 * WIDTH + D_MODEL

TM = 512
TN = 512
TILES_PER_SEQ = SEQ // TM
JC = WIDTH // TN
JZ = WIDTH // TN
J_IN = JC + JZ
J1 = D_MODEL // TN
J2 = D_MODEL // TN
J_OUT = J1 + J2

TB = 256
P_ROWS = N_TOK * TOP_K + N_EXPERTS * TB
N_BLOCKS = P_ROWS // TB
TMD = 256


def _dot(a, b):
    return jnp.dot(a, b, preferred_element_type=F32)


def _mixer_in_body(h_ref, g1_ref, wc_ref, wb_ref, wa_ref, wu_ref, wv_ref, cw_ref, lng_ref, lnb_ref,
                   sw_ref, sb_ref, ya_ref, yb_ref, xn_scr, xc_scr, carry_scr, u_scr, v_scr):
    i = pl.program_id(0)
    j = pl.program_id(1)

    @pl.when(j == 0)
    def _():
        x = h_ref[...]
        ms = jnp.mean(x * x, axis=-1, keepdims=True)
        xn_scr[...] = ((x * lax.rsqrt(ms + EPS)) * g1_ref[...]).astype(BF16)

    @pl.when(j < JC)
    def _():
        xn = xn_scr[...]
        xc = _dot(xn, wc_ref[...]) * _dot(xn, wa_ref[...])

        @pl.when(i % TILES_PER_SEQ == 0)
        def _():
            xc_scr[0:SUBLANES, :] = jnp.zeros((SUBLANES, TN), F32)

        @pl.when(i % TILES_PER_SEQ != 0)
        def _():
            xc_scr[0:SUBLANES, :] = carry_scr[j]

        xc_scr[SUBLANES:SUBLANES + TM, :] = xc
        x1 = xc_scr[SUBLANES - 1:SUBLANES - 1 + TM, :]
        x2 = xc_scr[SUBLANES - 2:SUBLANES - 2 + TM, :]
        cw = cw_ref[...]
        conv = cw[0:1, :] * x2 + cw[1:2, :] * x1 + cw[2:3, :] * xc
        carry_scr[j] = xc_scr[TM:TM + SUBLANES, :]
        ya_ref[...] = (_dot(xn, wb_ref[...]) * conv).astype(BF16)

    @pl.when(j >= JC)
    def _():
        jj = j - JC
        xn = xn_scr[...]
        u_scr[jj] = jax.nn.gelu(_dot(xn, wu_ref[...]))
        v_scr[jj] = jax.nn.gelu(_dot(xn, wv_ref[...]))

    @pl.when(j == J_IN - 1)
    def _():
        s1 = jnp.zeros((TM, 1), F32)
        for k in range(JZ):
            s1 = s1 + jnp.sum(v_scr[k], axis=-1, keepdims=True)
        mu = s1 / WIDTH
        s2 = jnp.zeros((TM, 1), F32)
        for k in range(JZ):
            d = v_scr[k] - mu
            s2 = s2 + jnp.sum(d * d, axis=-1, keepdims=True)
        rstd = lax.rsqrt(s2 / WIDTH + EPS)
        row = lax.broadcasted_iota(jnp.int32, (CHUNK, CHUNK), 0)
        col = lax.broadcasted_iota(jnp.int32, (CHUNK, CHUNK), 1)
        causal = col <= row
        gpt = TN // GROUP_DIM
        for k in range(JZ):
            vn = (((v_scr[k] - mu) * rstd) * lng_ref[:, k * TN:(k + 1) * TN]
                  + lnb_ref[:, k * TN:(k + 1) * TN]).astype(BF16)
            for gl in range(gpt):
                g = k * gpt + gl
                w = jnp.where(causal, sw_ref[g], 0.0).astype(BF16)
                bias = sb_ref[:, g * GROUP_DIM:(g + 1) * GROUP_DIM]
                for n in range(TM // CHUNK):
                    rows = slice(n * CHUNK, (n + 1) * CHUNK)
                    mixed = _dot(w, vn[rows, gl * GROUP_DIM:(gl + 1) * GROUP_DIM])
                    u = u_scr[k, rows, gl * GROUP_DIM:(gl + 1) * GROUP_DIM]
                    yb_ref[rows, g * GROUP_DIM:(g + 1) * GROUP_DIM] = (u * (mixed + bias)).astype(BF16)


def _mixer_in(l, h, g1, w_in, conv_w, ln_g, ln_b, sgu_w, sgu_bias):
    cj = lambda j: jnp.minimum(j, JC - 1)
    zj = lambda j: jnp.clip(j - JC, 0, JZ - 1)
    wspec = lambda off, f: pl.BlockSpec((None, D_MODEL, TN), lambda i, j: (l, 0, off // TN + f(j)))
    return pl.pallas_call(
        _mixer_in_body,
        grid=(N_TOK // TM, J_IN),
        in_specs=[
            pl.BlockSpec((TM, D_MODEL), lambda i, j: (i, 0)),
            pl.BlockSpec((None, 1, D_MODEL), lambda i, j: (l, 0, 0)),
            wspec(COL_C, cj), wspec(COL_B, cj), wspec(COL_XA, cj), wspec(COL_U, zj), wspec(COL_V, zj),
            pl.BlockSpec((None, CONV_K, TN), lambda i, j: (l, 0, cj(j))),
            pl.BlockSpec((None, 1, WIDTH), lambda i, j: (l, 0, 0)),
            pl.BlockSpec((None, 1, WIDTH), lambda i, j: (l, 0, 0)),
            pl.BlockSpec((None, SGU_GROUPS, CHUNK, CHUNK), lambda i, j: (l, 0, 0, 0)),
            pl.BlockSpec((None, CHUNK, WIDTH), lambda i, j: (l, 0, 0)),
        ],
        out_specs=[
            pl.BlockSpec((TM, TN), lambda i, j: (i, cj(j))),
            pl.BlockSpec((TM, WIDTH), lambda i, j: (i, 0)),
        ],
        out_shape=[jax.ShapeDtypeStruct((N_TOK, WIDTH), BF16),
                   jax.ShapeDtypeStruct((N_TOK, WIDTH), BF16)],
        scratch_shapes=[
            pltpu.VMEM((TM, D_MODEL), BF16),
            pltpu.VMEM((TM + SUBLANES, TN), F32),
            pltpu.VMEM((JC, SUBLANES, TN), F32),
            pltpu.VMEM((JZ, TM, TN), F32),
            pltpu.VMEM((JZ, TM, TN), F32),
        ],
        compiler_params=pltpu.CompilerParams(
            dimension_semantics=("arbitrary", "arbitrary"), vmem_limit_bytes=VMEM_LIMIT),
        name="mixer_in",
    )(h, g1, w_in, w_in, w_in, w_in, w_in, conv_w, ln_g, ln_b, sgu_w, sgu_bias)


def _mixer_out_body(h_ref, g1_ref, ya_ref, yb_ref, wga_ref, wgb_ref, bga_ref, bgb_ref, wba_ref, wbb_ref,
                    wo_ref, g2_ref, wrh_ref, wrl_ref, rb_ref,
                    h1_ref, ri_ref, rw_ref, cnt_ref,
                    xn_scr, mg_scr, out_scr, cnt_scr):
    i = pl.program_id(0)
    j = pl.program_id(1)

    @pl.when(j == 0)
    def _():
        x = h_ref[...]
        ms = jnp.mean(x * x, axis=-1, keepdims=True)
        xn_scr[...] = ((x * lax.rsqrt(ms + EPS)) * g1_ref[...]).astype(BF16)

    @pl.when((i == 0) & (j == 0))
    def _():
        cnt_scr[...] = jnp.zeros((1, LANES), F32)

    @pl.when(j < J1)
    def _():
        xn = xn_scr[...]
        ga = jax.nn.sigmoid(_dot(xn, wga_ref[...]) + bga_ref[...])
        gb = jax.nn.sigmoid(_dot(xn, wgb_ref[...]) + bgb_ref[...])
        mg_scr[j] = (ga * _dot(ya_ref[...], wba_ref[...]) + gb * _dot(yb_ref[...], wbb_ref[...])).astype(BF16)

    @pl.when(j >= J1)
    def _():
        acc = _dot(mg_scr[0], wo_ref[0:TN, :])
        for k in range(1, J1):
            acc = acc + _dot(mg_scr[k], wo_ref[k * TN:(k + 1) * TN, :])
        out_scr[j - J1] = acc

    @pl.when(j == J_OUT - 1)
    def _():
        ss = jnp.zeros((TM, 1), F32)
        for k in range(J2):
            cols = slice(k * TN, (k + 1) * TN)
            hk = h_ref[:, cols] + out_scr[k]
            h1_ref[:, cols] = hk
            ss = ss + jnp.sum(hk * hk, axis=-1, keepdims=True)
        rstd = lax.rsqrt(ss / D_MODEL + EPS)

        logits = jnp.zeros((TM, LANES), F32)
        for k in range(J2):
            cols = slice(k * TN, (k + 1) * TN)
            xk = (h1_ref[:, cols] * rstd) * g2_ref[:, cols]
            hi = xk.astype(BF16)
            lo = (xk - hi.astype(F32)).astype(BF16)
            logits = logits + (_dot(hi, wrh_ref[cols, :]) + _dot(lo, wrh_ref[cols, :])
                               + _dot(hi, wrl_ref[cols, :]))
        lg = logits + rb_ref[...]

        lane_i = lax.broadcasted_iota(jnp.int32, (TM, LANES), 1)
        lane = lane_i.astype(F32)
        neg = -jnp.inf
        big = float(LANES)
        is_g = lane_i < N_GROUPS
        gl = jnp.where(is_g, lg, neg)
        gmax = jnp.max(gl, axis=-1, keepdims=True)
        g_idx = jnp.min(jnp.where(gl == gmax, lane, big), axis=-1, keepdims=True)
        g_w = 1.0 / jnp.sum(jnp.exp(gl - gmax), axis=-1, keepdims=True)

        first = N_GROUPS + g_idx * PER_GROUP
        in_grp = (lane >= first) & (lane < first + PER_GROUP)
        el = jnp.where(in_grp, lg, neg)
        m1 = jnp.max(el, axis=-1, keepdims=True)
        i1 = jnp.min(jnp.where(in_grp & (el == m1), lane, big), axis=-1, keepdims=True)
        el2 = jnp.where(lane == i1, neg, el)
        m2 = jnp.max(el2, axis=-1, keepdims=True)
        i2 = jnp.min(jnp.where(in_grp & (lane != i1) & (el2 == m2), lane, big), axis=-1, keepdims=True)
        t = jnp.exp(m2 - m1)
        w1 = g_w * (1.0 / (1.0 + t))
        w2 = g_w * (t / (1.0 + t))

        onehot = jnp.where((lane == i1) | (lane == i2), 1.0, 0.0)
        r_i = lax.broadcasted_iota(jnp.int32, (TM, TM), 0)
        c_i = lax.broadcasted_iota(jnp.int32, (TM, TM), 1)
        tri = jnp.where(c_i < r_i, 1.0, 0.0).astype(BF16)
        before = cnt_scr[...] + _dot(tri, onehot.astype(BF16))
        rank1 = jnp.sum(jnp.where(lane == i1, before, 0.0), axis=-1, keepdims=True)
        rank2 = jnp.sum(jnp.where(lane == i2, before, 0.0), axis=-1, keepdims=True)
        cnt_scr[...] = cnt_scr[...] + jnp.sum(onehot, axis=0, keepdims=True)

        ri = jnp.where(lane_i == 0, i1 - N_GROUPS,
                       jnp.where(lane_i == 1, i2 - N_GROUPS,
                                 jnp.where(lane_i == 2, rank1, jnp.where(lane_i == 3, rank2, 0.0))))
        ri_ref[...] = ri.astype(jnp.int32)
        rw_ref[...] = jnp.where(lane_i == 0, w1, jnp.where(lane_i == 1, w2, 0.0))
        cnt_ref[...] = jnp.broadcast_to(cnt_scr[...], (SUBLANES, LANES))


def _mixer_out(l, h, g1, ya, yb, w_in, b_gate, w_branch, w_out, g2, wr_hi, wr_lo, r_bias):
    mj = lambda j: jnp.minimum(j, J1 - 1)
    oj = lambda j: jnp.clip(j - J1, 0, J2 - 1)
    const = lambda *blk: pl.BlockSpec((None,) + blk, lambda i, j: (l,) + (0,) * len(blk))
    return pl.pallas_call(
        _mixer_out_body,
        grid=(N_TOK // TM, J_OUT),
        in_specs=[
            pl.BlockSpec((TM, D_MODEL), lambda i, j: (i, 0)),
            const(1, D_MODEL),
            pl.BlockSpec((TM, WIDTH), lambda i, j: (i, 0)),
            pl.BlockSpec((TM, WIDTH), lambda i, j: (i, 0)),
            pl.BlockSpec((None, D_MODEL, TN), lambda i, j: (l, 0, COL_GA // TN + mj(j))),
            pl.BlockSpec((None, D_MODEL, TN), lambda i, j: (l, 0, COL_GB // TN + mj(j))),
            pl.BlockSpec((None, 1, TN), lambda i, j: (l, 0, mj(j))),
            pl.BlockSpec((None, 1, TN), lambda i, j: (l, 0, D_MODEL // TN + mj(j))),
            pl.BlockSpec((None, None, WIDTH, TN), lambda i, j: (l, 0, 0, mj(j))),
            pl.BlockSpec((None, None, WIDTH, TN), lambda i, j: (l, 1, 0, mj(j))),
            pl.BlockSpec((None, D_MODEL, TN), lambda i, j: (l, 0, oj(j))),
            const(1, D_MODEL),
            const(D_MODEL, LANES), const(D_MODEL, LANES), const(1, LANES),
        ],
        out_specs=[
            pl.BlockSpec((TM, D_MODEL), lambda i, j: (i, 0)),
            pl.BlockSpec((TM, LANES), lambda i, j: (i, 0)),
            pl.BlockSpec((TM, LANES), lambda i, j: (i, 0)),
            pl.BlockSpec((SUBLANES, LANES), lambda i, j: (0, 0)),
        ],
        out_shape=[jax.ShapeDtypeStruct((N_TOK, D_MODEL), F32),
                   jax.ShapeDtypeStruct((N_TOK, LANES), jnp.int32),
                   jax.ShapeDtypeStruct((N_TOK, LANES), F32),
                   jax.ShapeDtypeStruct((SUBLANES, LANES), F32)],
        scratch_shapes=[
            pltpu.VMEM((TM, D_MODEL), BF16),
            pltpu.VMEM((J1, TM, TN), BF16),
            pltpu.VMEM((J2, TM, TN), F32),
            pltpu.VMEM((1, LANES), F32),
        ],
        compiler_params=pltpu.CompilerParams(
            dimension_semantics=("arbitrary", "arbitrary"), vmem_limit_bytes=VMEM_LIMIT),
        name="mixer_out",
    )(h, g1, ya, yb, w_in, w_in, b_gate, b_gate, w_branch, w_branch, w_out, g2, wr_hi, wr_lo, r_bias)


def _row_copy(src_ref, src_row, dst_ref, dst_row, sem):
    return pltpu.make_async_copy(src_ref.at[pl.ds(src_row, 1)], dst_ref.at[pl.ds(dst_row, 1)], sem)


def _dispatch_body(zs_ref, pe_ref, na_ref, dest_ref, h1_ref, g2_ref, xb_ref, xn_buf, zero_buf, sem_z, sem_s):
    i = pl.program_id(0)

    @pl.when(i == 0)
    def _():
        zero_buf[...] = jnp.zeros((TB, D_MODEL), F32)

        def zero_copy(start, rows):
            return pltpu.make_async_copy(
                zero_buf.at[pl.ds(0, rows)],
                xb_ref.at[pl.ds(pl.multiple_of(start, SUBLANES), rows)], sem_z)

        def pad_windows(e, act):
            start = zs_ref[e]
            tiles = (pe_ref[e] - start) // SUBLANES
            rows = TB
            while rows >= SUBLANES:
                hit = (tiles & (rows // SUBLANES)) != 0
                pl.when(hit)(functools.partial(act, start, rows))
                start = start + jnp.where(hit, rows, 0)
                rows //= 2

        def pad_start(e, c):
            pad_windows(e, lambda s, n: zero_copy(s, n).start())
            return c

        def pad_wait(e, c):
            pad_windows(e, lambda s, n: zero_copy(s, n).wait())
            return c

        def tail_start(b, c):
            zero_copy(b * TB, TB).start()
            return c

        def tail_wait(b, c):
            zero_copy(b * TB, TB).wait()
            return c

        lax.fori_loop(0, N_EXPERTS, pad_start, 0)
        lax.fori_loop(na_ref[0], N_BLOCKS, tail_start, 0)
        lax.fori_loop(0, N_EXPERTS, pad_wait, 0)
        lax.fori_loop(na_ref[0], N_BLOCKS, tail_wait, 0)

    x = h1_ref[...]
    ms = jnp.mean(x * x, axis=-1, keepdims=True)
    xn_buf[...] = (x * lax.rsqrt(ms + EPS)) * g2_ref[...]

    def issue(r, c):
        _row_copy(xn_buf, r, xb_ref, dest_ref[0, 2 * r], sem_s).start()
        _row_copy(xn_buf, r, xb_ref, dest_ref[0, 2 * r + 1], sem_s).start()
        return c

    lax.fori_loop(0, TMD, issue, 0)

    def drain(r, c):
        _row_copy(xn_buf, r, xb_ref, dest_ref[0, 2 * r], sem_s).wait()
        _row_copy(xn_buf, r, xb_ref, dest_ref[0, 2 * r + 1], sem_s).wait()
        return c

    lax.fori_loop(0, TMD, drain, 0)


def _dispatch(l, zstart, pad_end, n_active, dest, h1, g2):
    return pl.pallas_call(
        _dispatch_body,
        grid_spec=pltpu.PrefetchScalarGridSpec(
            num_scalar_prefetch=3,
            grid=(N_TOK // TMD,),
            in_specs=[
                pl.BlockSpec((None, 1, TOP_K * TMD), lambda i, zs, pe, na: (i, 0, 0), memory_space=pltpu.SMEM),
                pl.BlockSpec((TMD, D_MODEL), lambda i, zs, pe, na: (i, 0)),
                pl.BlockSpec((None, 1, D_MODEL), lambda i, zs, pe, na: (l, 0, 0)),
            ],
            out_specs=pl.BlockSpec(memory_space=pl.ANY),
            scratch_shapes=[
                pltpu.VMEM((TMD, D_MODEL), F32),
                pltpu.VMEM((TB, D_MODEL), F32),
                pltpu.SemaphoreType.DMA,
                pltpu.SemaphoreType.DMA,
            ],
        ),
        out_shape=jax.ShapeDtypeStruct((P_ROWS, D_MODEL), F32),
        compiler_params=pltpu.CompilerParams(
            dimension_semantics=("arbitrary",), vmem_limit_bytes=VMEM_LIMIT),
        name="dispatch",
    )(zstart, pad_end, n_active, dest, h1, g2)


def _experts_body(be_ref, na_ref, x_ref, wg_ref, wu_ref, wd_ref, y_ref, wg_s, wu_s, wd_s):
    b = pl.program_id(0)

    @pl.when(b < na_ref[0])
    def _():
        prev = be_ref[jnp.maximum(b - 1, 0)]

        @pl.when((b == 0) | (be_ref[b] != prev))
        def _():
            wg_s[...] = wg_ref[...].astype(BF16)
            wu_s[...] = wu_ref[...].astype(BF16)
            wd_s[...] = wd_ref[...].astype(BF16)

        x = x_ref[...].astype(BF16)
        hm = (jax.nn.silu(_dot(x, wg_s[...])) * _dot(x, wu_s[...])).astype(BF16)
        y_ref[...] = _dot(hm, wd_s[...])

    @pl.when(b >= na_ref[0])
    def _():
        y_ref[...] = jnp.zeros((TB, D_MODEL), F32)


def _experts(l, block_e, n_active, xb, w_gate, w_up, w_down):
    blk = lambda b, na: jnp.minimum(b, na[0] - 1)
    return pl.pallas_call(
        _experts_body,
        grid_spec=pltpu.PrefetchScalarGridSpec(
            num_scalar_prefetch=2,
            grid=(N_BLOCKS,),
            in_specs=[
                pl.BlockSpec((TB, D_MODEL), lambda b, be, na: (blk(b, na), 0)),
                pl.BlockSpec((None, None, D_MODEL, D_EXPERT), lambda b, be, na: (l, be[blk(b, na)], 0, 0)),
                pl.BlockSpec((None, None, D_MODEL, D_EXPERT), lambda b, be, na: (l, be[blk(b, na)], 0, 0)),
                pl.BlockSpec((None, None, D_EXPERT, D_MODEL), lambda b, be, na: (l, be[blk(b, na)], 0, 0)),
            ],
            out_specs=pl.BlockSpec((TB, D_MODEL), lambda b, be, na: (b, 0)),
            scratch_shapes=[
                pltpu.VMEM((D_MODEL, D_EXPERT), BF16),
                pltpu.VMEM((D_MODEL, D_EXPERT), BF16),
                pltpu.VMEM((D_EXPERT, D_MODEL), BF16),
            ],
        ),
        out_shape=jax.ShapeDtypeStruct((P_ROWS, D_MODEL), F32),
        compiler_params=pltpu.CompilerParams(
            dimension_semantics=("arbitrary",), vmem_limit_bytes=VMEM_LIMIT),
        name="experts",
    )(block_e, n_active, xb, w_gate, w_up, w_down)


def _combine_body(final, dest_ref, h1_ref, rw_ref, gf_ref, yb_ref, out_ref, ybuf, sem):
    def issue(r, c):
        _row_copy(yb_ref, dest_ref[0, 2 * r], ybuf.at[0], r, sem).start()
        _row_copy(yb_ref, dest_ref[0, 2 * r + 1], ybuf.at[1], r, sem).start()
        return c

    lax.fori_loop(0, TMD, issue, 0)

    def drain(r, c):
        _row_copy(yb_ref, dest_ref[0, 2 * r], ybuf.at[0], r, sem).wait()
        _row_copy(yb_ref, dest_ref[0, 2 * r + 1], ybuf.at[1], r, sem).wait()
        return c

    lax.fori_loop(0, TMD, drain, 0)

    w = rw_ref[...]
    h2 = h1_ref[...] + (w[:, 0:1] * ybuf[0] + w[:, 1:2] * ybuf[1])
    if final:
        ms = jnp.mean(h2 * h2, axis=-1, keepdims=True)
        h2 = (h2 * lax.rsqrt(ms + EPS)) * gf_ref[...]
    out_ref[...] = h2


def _combine(final, dest, h1, rw, gf, yb):
    return pl.pallas_call(
        functools.partial(_combine_body, final),
        grid=(N_TOK // TMD,),
        in_specs=[
            pl.BlockSpec((None, 1, TOP_K * TMD), lambda i: (i, 0, 0), memory_space=pltpu.SMEM),
            pl.BlockSpec((TMD, D_MODEL), lambda i: (i, 0)),
            pl.BlockSpec((TMD, LANES), lambda i: (i, 0)),
            pl.BlockSpec((1, D_MODEL), lambda i: (0, 0)),
            pl.BlockSpec(memory_space=pl.ANY),
        ],
        out_specs=pl.BlockSpec((TMD, D_MODEL), lambda i: (i, 0)),
        out_shape=jax.ShapeDtypeStruct((N_TOK, D_MODEL), F32),
        scratch_shapes=[
            pltpu.VMEM((TOP_K, TMD, D_MODEL), F32),
            pltpu.SemaphoreType.DMA,
        ],
        compiler_params=pltpu.CompilerParams(
            dimension_semantics=("arbitrary",), vmem_limit_bytes=VMEM_LIMIT),
        name="combine",
    )(dest, h1, rw, gf, yb)


def _route_tables(cnt, ri):
    counts = cnt[0, N_GROUPS:N_GROUPS + N_EXPERTS].astype(jnp.int32)
    padded = (counts + TB - 1) // TB * TB
    pad_end = jnp.cumsum(padded)
    pad_start = pad_end - padded
    dest = pad_start[ri[:, 0:TOP_K]] + ri[:, TOP_K:2 * TOP_K]
    dest = dest.reshape(N_TOK // TMD, 1, TOP_K * TMD)
    n_active = (pad_end[-1:] // TB).astype(jnp.int32)
    block_e = jnp.minimum(
        jnp.searchsorted(pad_end, jnp.arange(N_BLOCKS, dtype=jnp.int32) * TB, side="right"),
        N_EXPERTS - 1).astype(jnp.int32)
    zero_start = (pad_start + counts) // SUBLANES * SUBLANES
    return dest, block_e, n_active, zero_start.astype(jnp.int32), pad_end.astype(jnp.int32)


def kernel(x, norm1_g, w_in, b_gate, conv_w, ln_v_g, ln_v_b, sgu_w, sgu_b, w_branch, w_out, norm2_g,
           router_g, router_g_b, router_e, router_e_b, w_gate, w_up, w_down, final_g):
    h = x.reshape(N_TOK, D_MODEL)

    w_in_b = w_in.astype(BF16)
    w_branch_b = w_branch.astype(BF16)
    w_out_b = w_out.astype(BF16)
    g1 = norm1_g.reshape(DEPTH, 1, D_MODEL)
    g2 = norm2_g.reshape(DEPTH, 1, D_MODEL)
    gf = final_g.reshape(1, D_MODEL)
    ln_g = ln_v_g.reshape(DEPTH, 1, WIDTH)
    ln_b = ln_v_b.reshape(DEPTH, 1, WIDTH)
    bg = b_gate.reshape(DEPTH, 1, 2 * D_MODEL)
    sgu_bias = jnp.repeat(jnp.swapaxes(sgu_b, 1, 2), GROUP_DIM, axis=2)
    pad = LANES - N_GROUPS - N_EXPERTS
    w_r = jnp.concatenate([router_g, router_e, jnp.zeros((DEPTH, D_MODEL, pad), F32)], axis=2)
    wr_hi = w_r.astype(BF16)
    wr_lo = (w_r - wr_hi.astype(F32)).astype(BF16)
    r_bias = jnp.concatenate([router_g_b, router_e_b, jnp.zeros((DEPTH, pad), F32)], axis=1)
    r_bias = r_bias.reshape(DEPTH, 1, LANES)

    for l in range(DEPTH):
        ya, yb = _mixer_in(l, h, g1, w_in_b, conv_w, ln_g, ln_b, sgu_w, sgu_bias)
        h1, ri, rw, cnt = _mixer_out(l, h, g1, ya, yb, w_in_b, bg, w_branch_b, w_out_b, g2,
                                     wr_hi, wr_lo, r_bias)
        dest, block_e, n_active, zstart, pad_end = _route_tables(cnt, ri)
        xb = _dispatch(l, zstart, pad_end, n_active, dest, h1, g2)
        yexp = _experts(l, block_e, n_active, xb, w_gate, w_up, w_down)
        h = _combine(l == DEPTH - 1, dest, h1, rw, gf, yexp)
    return h.reshape(BATCH, SEQ, D_MODEL)
```

```python
import functools

import jax
import jax.numpy as jnp
from jax import lax
from jax.experimental import pallas as pl
from jax.experimental.pallas import tpu as pltpu

F32 = jnp.float32
BF16 = jnp.bfloat16

D_MODEL = 2048
BATCH = 4
SEQ = 4096
DEPTH = 2
N_TOK = BATCH * SEQ
WIDTH = D_MODEL // 2
CONV_K = 3
SGU_GROUPS = 8
CHUNK = 128
GROUP_DIM = WIDTH // SGU_GROUPS
D_IN = 3 * WIDTH + 2 * WIDTH + 2 * D_MODEL
N_GROUPS = 4
PER_GROUP = 8
N_EXPERTS = N_GROUPS * PER_GROUP
TOP_K = 2
D_EXPERT = D_MODEL // 4
EPS = 1e-6

LANES = 128
SUBLANES = 8
VMEM_LIMIT = 56 * 1024 * 1024

COL_C, COL_B, COL_XA = 0, WIDTH, 2 * WIDTH
COL_U, COL_V = 3 * WIDTH, 4 * WIDTH
COL_GA, COL_GB = 5 * WIDTH, 5 * WIDTH + D_MODEL

TM = 512
TN = 512
TILES_PER_SEQ = SEQ // TM
JC = WIDTH // TN
JZ = WIDTH // TN
J_IN = JC + JZ
J1 = D_MODEL // TN
J2 = D_MODEL // TN
J_OUT = J1 + J2

TB = 256
P_ROWS = N_TOK * TOP_K + N_EXPERTS * TB
N_BLOCKS = P_ROWS // TB
TMD = 256


def _dot(a, b):
    return jnp.dot(a, b, preferred_element_type=F32)


def _mixer_in_body(h_ref, g1_ref, wc_ref, wb_ref, wa_ref, wu_ref, wv_ref, cw_ref, lng_ref, lnb_ref,
                   sw_ref, sb_ref, ya_ref, yb_ref, xn_scr, xc_scr, carry_scr, u_scr, v_scr):
    i = pl.program_id(0)
    j = pl.program_id(1)

    @pl.when(j == 0)
    def _():
        x = h_ref[...]
        ms = jnp.mean(x * x, axis=-1, keepdims=True)
        xn_scr[...] = ((x * lax.rsqrt(ms + EPS)) * g1_ref[...]).astype(BF16)

    @pl.when(j < JC)
    def _():
        xn = xn_scr[...]
        xc = _dot(xn, wc_ref[...]) * _dot(xn, wa_ref[...])

        @pl.when(i % TILES_PER_SEQ == 0)
        def _():
            xc_scr[0:SUBLANES, :] = jnp.zeros((SUBLANES, TN), F32)

        @pl.when(i % TILES_PER_SEQ != 0)
        def _():
            xc_scr[0:SUBLANES, :] = carry_scr[j]

        xc_scr[SUBLANES:SUBLANES + TM, :] = xc
        x1 = xc_scr[SUBLANES - 1:SUBLANES - 1 + TM, :]
        x2 = xc_scr[SUBLANES - 2:SUBLANES - 2 + TM, :]
        cw = cw_ref[...]
        conv = cw[0:1, :] * x2 + cw[1:2, :] * x1 + cw[2:3, :] * xc
        carry_scr[j] = xc_scr[TM:TM + SUBLANES, :]
        ya_ref[...] = (_dot(xn, wb_ref[...]) * conv).astype(BF16)

    @pl.when(j >= JC)
    def _():
        jj = j - JC
        xn = xn_scr[...]
        u_scr[jj] = jax.nn.gelu(_dot(xn, wu_ref[...]))
        v_scr[jj] = jax.nn.gelu(_dot(xn, wv_ref[...]))

    @pl.when(j == J_IN - 1)
    def _():
        s1 = jnp.zeros((TM, 1), F32)
        for k in range(JZ):
            s1 = s1 + jnp.sum(v_scr[k], axis=-1, keepdims=True)
        mu = s1 / WIDTH
        s2 = jnp.zeros((TM, 1), F32)
        for k in range(JZ):
            d = v_scr[k] - mu
            s2 = s2 + jnp.sum(d * d, axis=-1, keepdims=True)
        rstd = lax.rsqrt(s2 / WIDTH + EPS)
        row = lax.broadcasted_iota(jnp.int32, (CHUNK, CHUNK), 0)
        col = lax.broadcasted_iota(jnp.int32, (CHUNK, CHUNK), 1)
        causal = col <= row
        gpt = TN // GROUP_DIM
        for k in range(JZ):
            vn = (((v_scr[k] - mu) * rstd) * lng_ref[:, k * TN:(k + 1) * TN]
                  + lnb_ref[:, k * TN:(k + 1) * TN]).astype(BF16)
            for gl in range(gpt):
                g = k * gpt + gl
                w = jnp.where(causal, sw_ref[g], 0.0).astype(BF16)
                bias = sb_ref[:, g * GROUP_DIM:(g + 1) * GROUP_DIM]
                for n in range(TM // CHUNK):
                    rows = slice(n * CHUNK, (n + 1) * CHUNK)
                    mixed = _dot(w, vn[rows, gl * GROUP_DIM:(gl + 1) * GROUP_DIM])
                    u = u_scr[k, rows, gl * GROUP_DIM:(gl + 1) * GROUP_DIM]
                    yb_ref[rows, g * GROUP_DIM:(g + 1) * GROUP_DIM] = (u * (mixed + bias)).astype(BF16)


def _mixer_in(l, h, g1, w_in, conv_w, ln_g, ln_b, sgu_w, sgu_bias):
    cj = lambda j: jnp.minimum(j, JC - 1)
    zj = lambda j: jnp.clip(j - JC, 0, JZ - 1)
    wspec = lambda off, f: pl.BlockSpec((None, D_MODEL, TN), lambda i, j: (l, 0, off // TN + f(j)))
    return pl.pallas_call(
        _mixer_in_body,
        grid=(N_TOK // TM, J_IN),
        in_specs=[
            pl.BlockSpec((TM, D_MODEL), lambda i, j: (i, 0)),
            pl.BlockSpec((None, 1, D_MODEL), lambda i, j: (l, 0, 0)),
            wspec(COL_C, cj), wspec(COL_B, cj), wspec(COL_XA, cj), wspec(COL_U, zj), wspec(COL_V, zj),
            pl.BlockSpec((None, CONV_K, TN), lambda i, j: (l, 0, cj(j))),
            pl.BlockSpec((None, 1, WIDTH), lambda i, j: (l, 0, 0)),
            pl.BlockSpec((None, 1, WIDTH), lambda i, j: (l, 0, 0)),
            pl.BlockSpec((None, SGU_GROUPS, CHUNK, CHUNK), lambda i, j: (l, 0, 0, 0)),
            pl.BlockSpec((None, CHUNK, WIDTH), lambda i, j: (l, 0, 0)),
        ],
        out_specs=[
            pl.BlockSpec((TM, TN), lambda i, j: (i, cj(j))),
            pl.BlockSpec((TM, WIDTH), lambda i, j: (i, 0)),
        ],
        out_shape=[jax.ShapeDtypeStruct((N_TOK, WIDTH), BF16),
                   jax.ShapeDtypeStruct((N_TOK, WIDTH), BF16)],
        scratch_shapes=[
            pltpu.VMEM((TM, D_MODEL), BF16),
            pltpu.VMEM((TM + SUBLANES, TN), F32),
            pltpu.VMEM((JC, SUBLANES, TN), F32),
            pltpu.VMEM((JZ, TM, TN), F32),
            pltpu.VMEM((JZ, TM, TN), F32),
        ],
        compiler_params=pltpu.CompilerParams(
            dimension_semantics=("arbitrary", "arbitrary"), vmem_limit_bytes=VMEM_LIMIT),
        name="mixer_in",
    )(h, g1, w_in, w_in, w_in, w_in, w_in, conv_w, ln_g, ln_b, sgu_w, sgu_bias)


def _mixer_out_body(h_ref, g1_ref, ya_ref, yb_ref, wga_ref, wgb_ref, bga_ref, bgb_ref, wba_ref, wbb_ref,
                    wo_ref, g2_ref, wrh_ref, wrl_ref, rb_ref,
                    h1_ref, ri_ref, rw_ref, cnt_ref,
                    xn_scr, mg_scr, out_scr, cnt_scr):
    i = pl.program_id(0)
    j = pl.program_id(1)

    @pl.when(j == 0)
    def _():
        x = h_ref[...]
        ms = jnp.mean(x * x, axis=-1, keepdims=True)
        xn_scr[...] = ((x * lax.rsqrt(ms + EPS)) * g1_ref[...]).astype(BF16)

    @pl.when((i == 0) & (j == 0))
    def _():
        cnt_scr[...] = jnp.zeros((1, LANES), F32)

    @pl.when(j < J1)
    def _():
        xn = xn_scr[...]
        ga = jax.nn.sigmoid(_dot(xn, wga_ref[...]) + bga_ref[...])
        gb = jax.nn.sigmoid(_dot(xn, wgb_ref[...]) + bgb_ref[...])
        mg_scr[j] = (ga * _dot(ya_ref[...], wba_ref[...]) + gb * _dot(yb_ref[...], wbb_ref[...])).astype(BF16)

    @pl.when(j >= J1)
    def _():
        acc = _dot(mg_scr[0], wo_ref[0:TN, :])
        for k in range(1, J1):
            acc = acc + _dot(mg_scr[k], wo_ref[k * TN:(k + 1) * TN, :])
        out_scr[j - J1] = acc

    @pl.when(j == J_OUT - 1)
    def _():
        ss = jnp.zeros((TM, 1), F32)
        for k in range(J2):
            cols = slice(k * TN, (k + 1) * TN)
            hk = h_ref[:, cols] + out_scr[k]
            h1_ref[:, cols] = hk
            ss = ss + jnp.sum(hk * hk, axis=-1, keepdims=True)
        rstd = lax.rsqrt(ss / D_MODEL + EPS)

        logits = jnp.zeros((TM, LANES), F32)
        for k in range(J2):
            cols = slice(k * TN, (k + 1) * TN)
            xk = (h1_ref[:, cols] * rstd) * g2_ref[:, cols]
            hi = xk.astype(BF16)
            lo = (xk - hi.astype(F32)).astype(BF16)
            logits = logits + (_dot(hi, wrh_ref[cols, :]) + _dot(lo, wrh_ref[cols, :])
                               + _dot(hi, wrl_ref[cols, :]))
        lg = logits + rb_ref[...]

        lane_i = lax.broadcasted_iota(jnp.int32, (TM, LANES), 1)
        lane = lane_i.astype(F32)
        neg = -jnp.inf
        big = float(LANES)
        is_g = lane_i < N_GROUPS
        gl = jnp.where(is_g, lg, neg)
        gmax = jnp.max(gl, axis=-1, keepdims=True)
        g_idx = jnp.min(jnp.where(gl == gmax, lane, big), axis=-1, keepdims=True)
        g_w = 1.0 / jnp.sum(jnp.exp(gl - gmax), axis=-1, keepdims=True)

        first = N_GROUPS + g_idx * PER_GROUP
        in_grp = (lane >= first) & (lane < first + PER_GROUP)
        el = jnp.where(in_grp, lg, neg)
        m1 = jnp.max(el, axis=-1, keepdims=True)
        i1 = jnp.min(jnp.where(in_grp & (el == m1), lane, big), axis=-1, keepdims=True)
        el2 = jnp.where(lane == i1, neg, el)
        m2 = jnp.max(el2, axis=-1, keepdims=True)
        i2 = jnp.min(jnp.where(in_grp & (lane != i1) & (el2 == m2), lane, big), axis=-1, keepdims=True)
        t = jnp.exp(m2 - m1)
        w1 = g_w * (1.0 / (1.0 + t))
        w2 = g_w * (t / (1.0 + t))

        onehot = jnp.where((lane == i1) | (lane == i2), 1.0, 0.0)
        r_i = lax.broadcasted_iota(jnp.int32, (TM, TM), 0)
        c_i = lax.broadcasted_iota(jnp.int32, (TM, TM), 1)
        tri = jnp.where(c_i < r_i, 1.0, 0.0).astype(BF16)
        before = cnt_scr[...] + _dot(tri, onehot.astype(BF16))
        rank1 = jnp.sum(jnp.where(lane == i1, before, 0.0), axis=-1, keepdims=True)
        rank2 = jnp.sum(jnp.where(lane == i2, before, 0.0), axis=-1, keepdims=True)
        cnt_scr[...] = cnt_scr[...] + jnp.sum(onehot, axis=0, keepdims=True)

        ri = jnp.where(lane_i == 0, i1 - N_GROUPS,
                       jnp.where(lane_i == 1, i2 - N_GROUPS,
                                 jnp.where(lane_i == 2, rank1, jnp.where(lane_i == 3, rank2, 0.0))))
        ri_ref[...] = ri.astype(jnp.int32)
        rw_ref[...] = jnp.where(lane_i == 0, w1, jnp.where(lane_i == 1, w2, 0.0))
        cnt_ref[...] = jnp.broadcast_to(cnt_scr[...], (SUBLANES, LANES))


def _mixer_out(l, h, g1, ya, yb, w_in, b_gate, w_branch, w_out, g2, wr_hi, wr_lo, r_bias):
    mj = lambda j: jnp.minimum(j, J1 - 1)
    oj = lambda j: jnp.clip(j - J1, 0, J2 - 1)
    const = lambda *blk: pl.BlockSpec((None,) + blk, lambda i, j: (l,) + (0,) * len(blk))
    return pl.pallas_call(
        _mixer_out_body,
        grid=(N_TOK // TM, J_OUT),
        in_specs=[
            pl.BlockSpec((TM, D_MODEL), lambda i, j: (i, 0)),
            const(1, D_MODEL),
            pl.BlockSpec((TM, WIDTH), lambda i, j: (i, 0)),
            pl.BlockSpec((TM, WIDTH), lambda i, j: (i, 0)),
            pl.BlockSpec((None, D_MODEL, TN), lambda i, j: (l, 0, COL_GA // TN + mj(j))),
            pl.BlockSpec((None, D_MODEL, TN), lambda i, j: (l, 0, COL_GB // TN + mj(j))),
            pl.BlockSpec((None, 1, TN), lambda i, j: (l, 0, mj(j))),
            pl.BlockSpec((None, 1, TN), lambda i, j: (l, 0, D_MODEL // TN + mj(j))),
            pl.BlockSpec((None, None, WIDTH, TN), lambda i, j: (l, 0, 0, mj(j))),
            pl.BlockSpec((None, None, WIDTH, TN), lambda i, j: (l, 1, 0, mj(j))),
            pl.BlockSpec((None, D_MODEL, TN), lambda i, j: (l, 0, oj(j))),
            const(1, D_MODEL),
            const(D_MODEL, LANES), const(D_MODEL, LANES), const(1, LANES),
        ],
        out_specs=[
            pl.BlockSpec((TM, D_MODEL), lambda i, j: (i, 0)),
            pl.BlockSpec((TM, LANES), lambda i, j: (i, 0)),
            pl.BlockSpec((TM, LANES), lambda i, j: (i, 0)),
            pl.BlockSpec((SUBLANES, LANES), lambda i, j: (0, 0)),
        ],
        out_shape=[jax.ShapeDtypeStruct((N_TOK, D_MODEL), F32),
                   jax.ShapeDtypeStruct((N_TOK, LANES), jnp.int32),
                   jax.ShapeDtypeStruct((N_TOK, LANES), F32),
                   jax.ShapeDtypeStruct((SUBLANES, LANES), F32)],
        scratch_shapes=[
            pltpu.VMEM((TM, D_MODEL), BF16),
            pltpu.VMEM((J1, TM, TN), BF16),
            pltpu.VMEM((J2, TM, TN), F32),
            pltpu.VMEM((1, LANES), F32),
        ],
        compiler_params=pltpu.CompilerParams(
            dimension_semantics=("arbitrary", "arbitrary"), vmem_limit_bytes=VMEM_LIMIT),
        name="mixer_out",
    )(h, g1, ya, yb, w_in, w_in, b_gate, b_gate, w_branch, w_branch, w_out, g2, wr_hi, wr_lo, r_bias)


def _row_copy(src_ref, src_row, dst_ref, dst_row, sem):
    return pltpu.make_async_copy(src_ref.at[pl.ds(src_row, 1)], dst_ref.at[pl.ds(dst_row, 1)], sem)


def _dispatch_body(zs_ref, pe_ref, na_ref, dest_ref, h1_ref, g2_ref, xb_ref, xn_buf, zero_buf, sem_z, sem_s):
    i = pl.program_id(0)

    @pl.when(i == 0)
    def _():
        zero_buf[...] = jnp.zeros((TB, D_MODEL), F32)

        def zero_copy(start, rows):
            return pltpu.make_async_copy(
                zero_buf.at[pl.ds(0, rows)],
                xb_ref.at[pl.ds(pl.multiple_of(start, SUBLANES), rows)], sem_z)

        def pad_windows(e, act):
            start = zs_ref[e]
            tiles = (pe_ref[e] - start) // SUBLANES
            rows = TB
            while rows >= SUBLANES:
                hit = (tiles & (rows // SUBLANES)) != 0
                pl.when(hit)(functools.partial(act, start, rows))
                start = start + jnp.where(hit, rows, 0)
                rows //= 2

        def pad_start(e, c):
            pad_windows(e, lambda s, n: zero_copy(s, n).start())
            return c

        def pad_wait(e, c):
            pad_windows(e, lambda s, n: zero_copy(s, n).wait())
            return c

        def tail_start(b, c):
            zero_copy(b * TB, TB).start()
            return c

        def tail_wait(b, c):
            zero_copy(b * TB, TB).wait()
            return c

        lax.fori_loop(0, N_EXPERTS, pad_start, 0)
        lax.fori_loop(na_ref[0], N_BLOCKS, tail_start, 0)
        lax.fori_loop(0, N_EXPERTS, pad_wait, 0)
        lax.fori_loop(na_ref[0], N_BLOCKS, tail_wait, 0)

    x = h1_ref[...]
    ms = jnp.mean(x * x, axis=-1, keepdims=True)
    xn_buf[...] = (x * lax.rsqrt(ms + EPS)) * g2_ref[...]

    def issue(r, c):
        _row_copy(xn_buf, r, xb_ref, dest_ref[0, 2 * r], sem_s).start()
        _row_copy(xn_buf, r, xb_ref, dest_ref[0, 2 * r + 1], sem_s).start()
        return c

    lax.fori_loop(0, TMD, issue, 0)

    def drain(r, c):
        _row_copy(xn_buf, r, xb_ref, dest_ref[0, 2 * r], sem_s).wait()
        _row_copy(xn_buf, r, xb_ref, dest_ref[0, 2 * r + 1], sem_s).wait()
        return c

    lax.fori_loop(0, TMD, drain, 0)


def _dispatch(l, zstart, pad_end, n_active, dest, h1, g2):
    return pl.pallas_call(
        _dispatch_body,
        grid_spec=pltpu.PrefetchScalarGridSpec(
            num_scalar_prefetch=3,
            grid=(N_TOK // TMD,),
            in_specs=[
                pl.BlockSpec((None, 1, TOP_K * TMD), lambda i, zs, pe, na: (i, 0, 0), memory_space=pltpu.SMEM),
                pl.BlockSpec((TMD, D_MODEL), lambda i, zs, pe, na: (i, 0)),
                pl.BlockSpec((None, 1, D_MODEL), lambda i, zs, pe, na: (l, 0, 0)),
            ],
            out_specs=pl.BlockSpec(memory_space=pl.ANY),
            scratch_shapes=[
                pltpu.VMEM((TMD, D_MODEL), F32),
                pltpu.VMEM((TB, D_MODEL), F32),
                pltpu.SemaphoreType.DMA,
                pltpu.SemaphoreType.DMA,
            ],
        ),
        out_shape=jax.ShapeDtypeStruct((P_ROWS, D_MODEL), F32),
        compiler_params=pltpu.CompilerParams(
            dimension_semantics=("arbitrary",), vmem_limit_bytes=VMEM_LIMIT),
        name="dispatch",
    )(zstart, pad_end, n_active, dest, h1, g2)


def _experts_body(be_ref, na_ref, x_ref, wg_ref, wu_ref, wd_ref, y_ref, wg_s, wu_s, wd_s):
    b = pl.program_id(0)

    @pl.when(b < na_ref[0])
    def _():
        prev = be_ref[jnp.maximum(b - 1, 0)]

        @pl.when((b == 0) | (be_ref[b] != prev))
        def _():
            wg_s[...] = wg_ref[...].astype(BF16)
            wu_s[...] = wu_ref[...].astype(BF16)
            wd_s[...] = wd_ref[...].astype(BF16)

        x = x_ref[...].astype(BF16)
        hm = (jax.nn.silu(_dot(x, wg_s[...])) * _dot(x, wu_s[...])).astype(BF16)
        y_ref[...] = _dot(hm, wd_s[...])

    @pl.when(b >= na_ref[0])
    def _():
        y_ref[...] = jnp.zeros((TB, D_MODEL), F32)


def _experts(l, block_e, n_active, xb, w_gate, w_up, w_down):
    blk = lambda b, na: jnp.minimum(b, na[0] - 1)
    return pl.pallas_call(
        _experts_body,
        grid_spec=pltpu.PrefetchScalarGridSpec(
            num_scalar_prefetch=2,
            grid=(N_BLOCKS,),
            in_specs=[
                pl.BlockSpec((TB, D_MODEL), lambda b, be, na: (blk(b, na), 0)),
                pl.BlockSpec((None, None, D_MODEL, D_EXPERT), lambda b, be, na: (l, be[blk(b, na)], 0, 0)),
                pl.BlockSpec((None, None, D_MODEL, D_EXPERT), lambda b, be, na: (l, be[blk(b, na)], 0, 0)),
                pl.BlockSpec((None, None, D_EXPERT, D_MODEL), lambda b, be, na: (l, be[blk(b, na)], 0, 0)),
            ],
            out_specs=pl.BlockSpec((TB, D_MODEL), lambda b, be, na: (b, 0)),
            scratch_shapes=[
                pltpu.VMEM((D_MODEL, D_EXPERT), BF16),
                pltpu.VMEM((D_MODEL, D_EXPERT), BF16),
                pltpu.VMEM((D_EXPERT, D_MODEL), BF16),
            ],
        ),
        out_shape=jax.ShapeDtypeStruct((P_ROWS, D_MODEL), F32),
        compiler_params=pltpu.CompilerParams(
            dimension_semantics=("arbitrary",), vmem_limit_bytes=VMEM_LIMIT),
        name="experts",
    )(block_e, n_active, xb, w_gate, w_up, w_down)


def _combine_body(final, dest_ref, h1_ref, rw_ref, gf_ref, yb_ref, out_ref, ybuf, sem):
    def issue(r, c):
        _row_copy(yb_ref, dest_ref[0, 2 * r], ybuf.at[0], r, sem).start()
        _row_copy(yb_ref, dest_ref[0, 2 * r + 1], ybuf.at[1], r, sem).start()
        return c

    lax.fori_loop(0, TMD, issue, 0)

    def drain(r, c):
        _row_copy(yb_ref, dest_ref[0, 2 * r], ybuf.at[0], r, sem).wait()
        _row_copy(yb_ref, dest_ref[0, 2 * r + 1], ybuf.at[1], r, sem).wait()
        return c

    lax.fori_loop(0, TMD, drain, 0)

    w = rw_ref[...]
    h2 = h1_ref[...] + (w[:, 0:1] * ybuf[0] + w[:, 1:2] * ybuf[1])
    if final:
        ms = jnp.mean(h2 * h2, axis=-1, keepdims=True)
        h2 = (h2 * lax.rsqrt(ms + EPS)) * gf_ref[...]
    out_ref[...] = h2


def _combine(final, dest, h1, rw, gf, yb):
    return pl.pallas_call(
        functools.partial(_combine_body, final),
        grid=(N_TOK // TMD,),
        in_specs=[
            pl.BlockSpec((None, 1, TOP_K * TMD), lambda i: (i, 0, 0), memory_space=pltpu.SMEM),
            pl.BlockSpec((TMD, D_MODEL), lambda i: (i, 0)),
            pl.BlockSpec((TMD, LANES), lambda i: (i, 0)),
            pl.BlockSpec((1, D_MODEL), lambda i: (0, 0)),
            pl.BlockSpec(memory_space=pl.ANY),
        ],
        out_specs=pl.BlockSpec((TMD, D_MODEL), lambda i: (i, 0)),
        out_shape=jax.ShapeDtypeStruct((N_TOK, D_MODEL), F32),
        scratch_shapes=[
            pltpu.VMEM((TOP_K, TMD, D_MODEL), F32),
            pltpu.SemaphoreType.DMA,
        ],
        compiler_params=pltpu.CompilerParams(
            dimension_semantics=("arbitrary",), vmem_limit_bytes=VMEM_LIMIT),
        name="combine",
    )(dest, h1, rw, gf, yb)


def _route_tables(cnt, ri):
    counts = cnt[0, N_GROUPS:N_GROUPS + N_EXPERTS].astype(jnp.int32)
    padded = (counts + TB - 1) // TB * TB
    pad_end = jnp.cumsum(padded)
    pad_start = pad_end - padded
    experts = jnp.arange(N_EXPERTS, dtype=jnp.int32)
    start_of = jnp.sum(jnp.where(ri[:, 0:TOP_K, None] == experts, pad_start, 0), axis=-1)
    dest = start_of + ri[:, TOP_K:2 * TOP_K]
    dest = dest.reshape(N_TOK // TMD, 1, TOP_K * TMD)
    n_active = (pad_end[-1:] // TB).astype(jnp.int32)
    block_row = jnp.arange(N_BLOCKS, dtype=jnp.int32)[:, None] * TB
    block_e = jnp.minimum(jnp.sum((pad_end[None, :] <= block_row).astype(jnp.int32), axis=1), N_EXPERTS - 1)
    zero_start = (pad_start + counts) // SUBLANES * SUBLANES
    return dest, block_e, n_active, zero_start.astype(jnp.int32), pad_end.astype(jnp.int32)


def kernel(x, norm1_g, w_in, b_gate, conv_w, ln_v_g, ln_v_b, sgu_w, sgu_b, w_branch, w_out, norm2_g,
           router_g, router_g_b, router_e, router_e_b, w_gate, w_up, w_down, final_g):
    h = x.reshape(N_TOK, D_MODEL)

    w_in_b = w_in.astype(BF16)
    w_branch_b = w_branch.astype(BF16)
    w_out_b = w_out.astype(BF16)
    g1 = norm1_g.reshape(DEPTH, 1, D_MODEL)
    g2 = norm2_g.reshape(DEPTH, 1, D_MODEL)
    gf = final_g.reshape(1, D_MODEL)
    ln_g = ln_v_g.reshape(DEPTH, 1, WIDTH)
    ln_b = ln_v_b.reshape(DEPTH, 1, WIDTH)
    bg = b_gate.reshape(DEPTH, 1, 2 * D_MODEL)
    sgu_bias = jnp.repeat(jnp.swapaxes(sgu_b, 1, 2), GROUP_DIM, axis=2)
    pad = LANES - N_GROUPS - N_EXPERTS
    w_r = jnp.concatenate([router_g, router_e, jnp.zeros((DEPTH, D_MODEL, pad), F32)], axis=2)
    wr_hi = w_r.astype(BF16)
    wr_lo = (w_r - wr_hi.astype(F32)).astype(BF16)
    r_bias = jnp.concatenate([router_g_b, router_e_b, jnp.zeros((DEPTH, pad), F32)], axis=1)
    r_bias = r_bias.reshape(DEPTH, 1, LANES)

    for l in range(DEPTH):
        ya, yb = _mixer_in(l, h, g1, w_in_b, conv_w, ln_g, ln_b, sgu_w, sgu_bias)
        h1, ri, rw, cnt = _mixer_out(l, h, g1, ya, yb, w_in_b, bg, w_branch_b, w_out_b, g2,
                                     wr_hi, wr_lo, r_bias)
        dest, block_e, n_active, zstart, pad_end = _route_tables(cnt, ri)
        xb = _dispatch(l, zstart, pad_end, n_active, dest, h1, g2)
        yexp = _experts(l, block_e, n_active, xb, w_gate, w_up, w_down)
        h = _combine(l == DEPTH - 1, dest, h1, rw, gf, yexp)
    return h.reshape(BATCH, SEQ, D_MODEL)
```

```python
import functools

import jax
import jax.numpy as jnp
from jax import lax
from jax.experimental import pallas as pl
from jax.experimental.pallas import tpu as pltpu

F32 = jnp.float32
BF16 = jnp.bfloat16

D_MODEL = 2048
BATCH = 4
SEQ = 4096
DEPTH = 2
N_TOK = BATCH * SEQ
WIDTH = D_MODEL // 2
CONV_K = 3
SGU_GROUPS = 8
CHUNK = 128
GROUP_DIM = WIDTH // SGU_GROUPS
D_IN = 3 * WIDTH + 2 * WIDTH + 2 * D_MODEL
N_GROUPS = 4
PER_GROUP = 8
N_EXPERTS = N_GROUPS * PER_GROUP
TOP_K = 2
TOP_K_SHIFT = TOP_K.bit_length() - 1
assert TOP_K == 1 << TOP_K_SHIFT and N_TOK & (N_TOK - 1) == 0
N_ASSIGN = N_TOK * TOP_K
D_EXPERT = D_MODEL // 4
EPS = 1e-6

LANES = 128
SUBLANES = 8
VMEM_LIMIT = 56 * 1024 * 1024

COL_C, COL_B, COL_XA = 0, WIDTH, 2 * WIDTH
COL_U, COL_V = 3 * WIDTH, 4 * WIDTH
COL_GA, COL_GB = 5 * WIDTH, 5 * WIDTH + D_MODEL

TM = 512
TN = 512
TILES_PER_SEQ = SEQ // TM
JC = WIDTH // TN
JZ = WIDTH // TN
J_IN = JC + JZ
J1 = D_MODEL // TN
J2 = D_MODEL // TN
J_OUT = J1 + J2

TB = 256
P_SLOTS = N_ASSIGN + N_EXPERTS * TB
N_BLOCKS = P_SLOTS // TB
INVERT_UNROLL = 8


def _dot(a, b):
    return jnp.dot(a, b, preferred_element_type=F32)


def _mixer_in_body(h_ref, g1_ref, wc_ref, wb_ref, wa_ref, wu_ref, wv_ref, cw_ref, lng_ref, lnb_ref,
                   sw_ref, sb_ref, ya_ref, yb_ref, xn_scr, xc_scr, carry_scr, u_scr, v_scr):
    i = pl.program_id(0)
    j = pl.program_id(1)

    @pl.when(j == 0)
    def _():
        x = h_ref[...]
        ms = jnp.mean(x * x, axis=-1, keepdims=True)
        xn_scr[...] = ((x * lax.rsqrt(ms + EPS)) * g1_ref[...]).astype(BF16)

    @pl.when(j < JC)
    def _():
        xn = xn_scr[...]
        xc = _dot(xn, wc_ref[...]) * _dot(xn, wa_ref[...])

        @pl.when(i % TILES_PER_SEQ == 0)
        def _():
            xc_scr[0:SUBLANES, :] = jnp.zeros((SUBLANES, TN), F32)

        @pl.when(i % TILES_PER_SEQ != 0)
        def _():
            xc_scr[0:SUBLANES, :] = carry_scr[j]

        xc_scr[SUBLANES:SUBLANES + TM, :] = xc
        x1 = xc_scr[SUBLANES - 1:SUBLANES - 1 + TM, :]
        x2 = xc_scr[SUBLANES - 2:SUBLANES - 2 + TM, :]
        cw = cw_ref[...]
        conv = cw[0:1, :] * x2 + cw[1:2, :] * x1 + cw[2:3, :] * xc
        carry_scr[j] = xc_scr[TM:TM + SUBLANES, :]
        ya_ref[...] = (_dot(xn, wb_ref[...]) * conv).astype(BF16)

    @pl.when(j >= JC)
    def _():
        jj = j - JC
        xn = xn_scr[...]
        u_scr[jj] = jax.nn.gelu(_dot(xn, wu_ref[...]))
        v_scr[jj] = jax.nn.gelu(_dot(xn, wv_ref[...]))

    @pl.when(j == J_IN - 1)
    def _():
        s1 = jnp.zeros((TM, 1), F32)
        for k in range(JZ):
            s1 = s1 + jnp.sum(v_scr[k], axis=-1, keepdims=True)
        mu = s1 / WIDTH
        s2 = jnp.zeros((TM, 1), F32)
        for k in range(JZ):
            d = v_scr[k] - mu
            s2 = s2 + jnp.sum(d * d, axis=-1, keepdims=True)
        rstd = lax.rsqrt(s2 / WIDTH + EPS)
        row = lax.broadcasted_iota(jnp.int32, (CHUNK, CHUNK), 0)
        col = lax.broadcasted_iota(jnp.int32, (CHUNK, CHUNK), 1)
        causal = col <= row
        gpt = TN // GROUP_DIM
        for k in range(JZ):
            vn = (((v_scr[k] - mu) * rstd) * lng_ref[:, k * TN:(k + 1) * TN]
                  + lnb_ref[:, k * TN:(k + 1) * TN]).astype(BF16)
            for gl in range(gpt):
                g = k * gpt + gl
                w = jnp.where(causal, sw_ref[g], 0.0).astype(BF16)
                bias = sb_ref[:, g * GROUP_DIM:(g + 1) * GROUP_DIM]
                for n in range(TM // CHUNK):
                    rows = slice(n * CHUNK, (n + 1) * CHUNK)
                    mixed = _dot(w, vn[rows, gl * GROUP_DIM:(gl + 1) * GROUP_DIM])
                    u = u_scr[k, rows, gl * GROUP_DIM:(gl + 1) * GROUP_DIM]
                    yb_ref[rows, g * GROUP_DIM:(g + 1) * GROUP_DIM] = (u * (mixed + bias)).astype(BF16)


def _mixer_in(l, h, g1, w_in, conv_w, ln_g, ln_b, sgu_w, sgu_bias):
    cj = lambda j: jnp.minimum(j, JC - 1)
    zj = lambda j: jnp.clip(j - JC, 0, JZ - 1)
    wspec = lambda off, f: pl.BlockSpec((None, D_MODEL, TN), lambda i, j: (l, 0, off // TN + f(j)))
    return pl.pallas_call(
        _mixer_in_body,
        grid=(N_TOK // TM, J_IN),
        in_specs=[
            pl.BlockSpec((TM, D_MODEL), lambda i, j: (i, 0)),
            pl.BlockSpec((None, 1, D_MODEL), lambda i, j: (l, 0, 0)),
            wspec(COL_C, cj), wspec(COL_B, cj), wspec(COL_XA, cj), wspec(COL_U, zj), wspec(COL_V, zj),
            pl.BlockSpec((None, CONV_K, TN), lambda i, j: (l, 0, cj(j))),
            pl.BlockSpec((None, 1, WIDTH), lambda i, j: (l, 0, 0)),
            pl.BlockSpec((None, 1, WIDTH), lambda i, j: (l, 0, 0)),
            pl.BlockSpec((None, SGU_GROUPS, CHUNK, CHUNK), lambda i, j: (l, 0, 0, 0)),
            pl.BlockSpec((None, CHUNK, WIDTH), lambda i, j: (l, 0, 0)),
        ],
        out_specs=[
            pl.BlockSpec((TM, TN), lambda i, j: (i, cj(j))),
            pl.BlockSpec((TM, WIDTH), lambda i, j: (i, 0)),
        ],
        out_shape=[jax.ShapeDtypeStruct((N_TOK, WIDTH), BF16),
                   jax.ShapeDtypeStruct((N_TOK, WIDTH), BF16)],
        scratch_shapes=[
            pltpu.VMEM((TM, D_MODEL), BF16),
            pltpu.VMEM((TM + SUBLANES, TN), F32),
            pltpu.VMEM((JC, SUBLANES, TN), F32),
            pltpu.VMEM((JZ, TM, TN), F32),
            pltpu.VMEM((JZ, TM, TN), F32),
        ],
        compiler_params=pltpu.CompilerParams(
            dimension_semantics=("arbitrary", "arbitrary"), vmem_limit_bytes=VMEM_LIMIT),
        name="mixer_in",
    )(h, g1, w_in, w_in, w_in, w_in, w_in, conv_w, ln_g, ln_b, sgu_w, sgu_bias)


def _mixer_out_body(h_ref, g1_ref, ya_ref, yb_ref, wga_ref, wgb_ref, bga_ref, bgb_ref, wba_ref, wbb_ref,
                    wo_ref, g2_ref, wrh_ref, wrl_ref, rb_ref,
                    h1_ref, ri_ref, rw_ref, cnt_ref,
                    xn_scr, mg_scr, out_scr, cnt_scr):
    i = pl.program_id(0)
    j = pl.program_id(1)

    @pl.when(j == 0)
    def _():
        x = h_ref[...]
        ms = jnp.mean(x * x, axis=-1, keepdims=True)
        xn_scr[...] = ((x * lax.rsqrt(ms + EPS)) * g1_ref[...]).astype(BF16)

    @pl.when((i == 0) & (j == 0))
    def _():
        cnt_scr[...] = jnp.zeros((1, LANES), F32)

    @pl.when(j < J1)
    def _():
        xn = xn_scr[...]
        ga = jax.nn.sigmoid(_dot(xn, wga_ref[...]) + bga_ref[...])
        gb = jax.nn.sigmoid(_dot(xn, wgb_ref[...]) + bgb_ref[...])
        mg_scr[j] = (ga * _dot(ya_ref[...], wba_ref[...]) + gb * _dot(yb_ref[...], wbb_ref[...])).astype(BF16)

    @pl.when(j >= J1)
    def _():
        acc = _dot(mg_scr[0], wo_ref[0:TN, :])
        for k in range(1, J1):
            acc = acc + _dot(mg_scr[k], wo_ref[k * TN:(k + 1) * TN, :])
        out_scr[j - J1] = acc

    @pl.when(j == J_OUT - 1)
    def _():
        ss = jnp.zeros((TM, 1), F32)
        for k in range(J2):
            cols = slice(k * TN, (k + 1) * TN)
            hk = h_ref[:, cols] + out_scr[k]
            h1_ref[:, cols] = hk
            ss = ss + jnp.sum(hk * hk, axis=-1, keepdims=True)
        rstd = lax.rsqrt(ss / D_MODEL + EPS)

        logits = jnp.zeros((TM, LANES), F32)
        for k in range(J2):
            cols = slice(k * TN, (k + 1) * TN)
            xk = (h1_ref[:, cols] * rstd) * g2_ref[:, cols]
            hi = xk.astype(BF16)
            lo = (xk - hi.astype(F32)).astype(BF16)
            logits = logits + (_dot(hi, wrh_ref[cols, :]) + _dot(lo, wrh_ref[cols, :])
                               + _dot(hi, wrl_ref[cols, :]))
        lg = logits + rb_ref[...]

        lane_i = lax.broadcasted_iota(jnp.int32, (TM, LANES), 1)
        lane = lane_i.astype(F32)
        neg = -jnp.inf
        big = float(LANES)
        is_g = lane_i < N_GROUPS
        gl = jnp.where(is_g, lg, neg)
        gmax = jnp.max(gl, axis=-1, keepdims=True)
        g_idx = jnp.min(jnp.where(gl == gmax, lane, big), axis=-1, keepdims=True)
        g_w = 1.0 / jnp.sum(jnp.exp(gl - gmax), axis=-1, keepdims=True)

        first = N_GROUPS + g_idx * PER_GROUP
        in_grp = (lane >= first) & (lane < first + PER_GROUP)
        el = jnp.where(in_grp, lg, neg)
        m1 = jnp.max(el, axis=-1, keepdims=True)
        i1 = jnp.min(jnp.where(in_grp & (el == m1), lane, big), axis=-1, keepdims=True)
        el2 = jnp.where(lane == i1, neg, el)
        m2 = jnp.max(el2, axis=-1, keepdims=True)
        i2 = jnp.min(jnp.where(in_grp & (lane != i1) & (el2 == m2), lane, big), axis=-1, keepdims=True)
        t = jnp.exp(m2 - m1)
        w1 = g_w * (1.0 / (1.0 + t))
        w2 = g_w * (t / (1.0 + t))

        onehot = jnp.where((lane == i1) | (lane == i2), 1.0, 0.0)
        r_i = lax.broadcasted_iota(jnp.int32, (TM, TM), 0)
        c_i = lax.broadcasted_iota(jnp.int32, (TM, TM), 1)
        tri = jnp.where(c_i < r_i, 1.0, 0.0).astype(BF16)
        before = cnt_scr[...] + _dot(tri, onehot.astype(BF16))
        rank1 = jnp.sum(jnp.where(lane == i1, before, 0.0), axis=-1, keepdims=True)
        rank2 = jnp.sum(jnp.where(lane == i2, before, 0.0), axis=-1, keepdims=True)
        cnt_scr[...] = cnt_scr[...] + jnp.sum(onehot, axis=0, keepdims=True)

        ri = jnp.where(lane_i == 0, i1 - N_GROUPS,
                       jnp.where(lane_i == 1, i2 - N_GROUPS,
                                 jnp.where(lane_i == 2, rank1, jnp.where(lane_i == 3, rank2, 0.0))))
        ri_ref[...] = ri.astype(jnp.int32)
        rw_ref[...] = jnp.where(lane_i == 0, w1, jnp.where(lane_i == 1, w2, 0.0))
        cnt_ref[...] = jnp.broadcast_to(cnt_scr[...], (SUBLANES, LANES))


def _mixer_out(l, h, g1, ya, yb, w_in, b_gate, w_branch, w_out, g2, wr_hi, wr_lo, r_bias):
    mj = lambda j: jnp.minimum(j, J1 - 1)
    oj = lambda j: jnp.clip(j - J1, 0, J2 - 1)
    const = lambda *blk: pl.BlockSpec((None,) + blk, lambda i, j: (l,) + (0,) * len(blk))
    return pl.pallas_call(
        _mixer_out_body,
        grid=(N_TOK // TM, J_OUT),
        in_specs=[
            pl.BlockSpec((TM, D_MODEL), lambda i, j: (i, 0)),
            const(1, D_MODEL),
            pl.BlockSpec((TM, WIDTH), lambda i, j: (i, 0)),
            pl.BlockSpec((TM, WIDTH), lambda i, j: (i, 0)),
            pl.BlockSpec((None, D_MODEL, TN), lambda i, j: (l, 0, COL_GA // TN + mj(j))),
            pl.BlockSpec((None, D_MODEL, TN), lambda i, j: (l, 0, COL_GB // TN + mj(j))),
            pl.BlockSpec((None, 1, TN), lambda i, j: (l, 0, mj(j))),
            pl.BlockSpec((None, 1, TN), lambda i, j: (l, 0, D_MODEL // TN + mj(j))),
            pl.BlockSpec((None, None, WIDTH, TN), lambda i, j: (l, 0, 0, mj(j))),
            pl.BlockSpec((None, None, WIDTH, TN), lambda i, j: (l, 1, 0, mj(j))),
            pl.BlockSpec((None, D_MODEL, TN), lambda i, j: (l, 0, oj(j))),
            const(1, D_MODEL),
            const(D_MODEL, LANES), const(D_MODEL, LANES), const(1, LANES),
        ],
        out_specs=[
            pl.BlockSpec((TM, D_MODEL), lambda i, j: (i, 0)),
            pl.BlockSpec((TM, LANES), lambda i, j: (i, 0)),
            pl.BlockSpec((TM, LANES), lambda i, j: (i, 0)),
            pl.BlockSpec((SUBLANES, LANES), lambda i, j: (0, 0)),
        ],
        out_shape=[jax.ShapeDtypeStruct((N_TOK, D_MODEL), F32),
                   jax.ShapeDtypeStruct((N_TOK, LANES), jnp.int32),
                   jax.ShapeDtypeStruct((N_TOK, LANES), F32),
                   jax.ShapeDtypeStruct((SUBLANES, LANES), F32)],
        scratch_shapes=[
            pltpu.VMEM((TM, D_MODEL), BF16),
            pltpu.VMEM((J1, TM, TN), BF16),
            pltpu.VMEM((J2, TM, TN), F32),
            pltpu.VMEM((1, LANES), F32),
        ],
        compiler_params=pltpu.CompilerParams(
            dimension_semantics=("arbitrary", "arbitrary"), vmem_limit_bytes=VMEM_LIMIT),
        name="mixer_out",
    )(h, g1, ya, yb, w_in, w_in, b_gate, b_gate, w_branch, w_branch, w_out, g2, wr_hi, wr_lo, r_bias)


def _invert_body(dest_ref, inv_ref):

    def clear(c, carry):
        for u in range(INVERT_UNROLL):
            p = c * INVERT_UNROLL + u
            inv_ref[p] = N_ASSIGN + (p & (2 * TB - 1))
        return carry

    lax.fori_loop(0, P_SLOTS // INVERT_UNROLL, clear, 0)

    def place(c, carry):
        for u in range(INVERT_UNROLL):
            a = c * INVERT_UNROLL + u
            inv_ref[dest_ref[a]] = (a & (TOP_K - 1)) * N_TOK + lax.shift_right_logical(a, TOP_K_SHIFT)
        return carry

    lax.fori_loop(0, N_ASSIGN // INVERT_UNROLL, place, 0)


def _invert(dest):
    return pl.pallas_call(
        _invert_body,
        in_specs=[pl.BlockSpec(memory_space=pltpu.SMEM)],
        out_specs=pl.BlockSpec(memory_space=pltpu.SMEM),
        out_shape=jax.ShapeDtypeStruct((P_SLOTS,), jnp.int32),
        name="invert",
    )(dest)


def _experts_body(be_ref, na_ref, inv_ref, h1_ref, g2_ref, wg_ref, wu_ref, wd_ref, y2_ref,
                  xbuf0, xbuf1, ybuf0, ybuf1, gsem0, gsem1, ssem0, ssem1, wg_s, wu_s, wd_s):
    b = pl.program_id(0)
    n_act = na_ref[0]
    bufs = ((xbuf0, ybuf0, gsem0, ssem0), (xbuf1, ybuf1, gsem1, ssem1))

    def gather(blk, r, xdst, sem):
        tok = inv_ref[blk * TB + r] & (N_TOK - 1)
        return pltpu.make_async_copy(h1_ref.at[pl.ds(tok, 1)], xdst.at[pl.ds(r, 1)], sem)

    def scatter(blk, r, ysrc, sem):
        return pltpu.make_async_copy(
            ysrc.at[pl.ds(r, 1)], y2_ref.at[pl.ds(inv_ref[blk * TB + r], 1)], sem)

    def wait_gather(xdst, sem):
        pltpu.make_async_copy(h1_ref.at[pl.ds(0, TB)], xdst, sem).wait()

    def wait_scatter(ysrc, sem):
        pltpu.make_async_copy(ysrc, y2_ref.at[pl.ds(0, TB)], sem).wait()

    def start_rows(copy):
        def body(r, carry):
            copy(r).start()
            return carry
        lax.fori_loop(0, TB, body, 0)

    def step(par, first):
        xcur, ycur, gcur, scur = bufs[par]
        xnxt, yprv, gnxt, sprv = bufs[1 - par]
        wait_gather(xcur, gcur)
        if not first:
            @pl.when(b >= 2)
            def _():
                wait_scatter(ycur, scur)

        @pl.when((b == 0) | (be_ref[b] != be_ref[jnp.maximum(b - 1, 0)]))
        def _():
            wg_s[...] = wg_ref[...].astype(BF16)
            wu_s[...] = wu_ref[...].astype(BF16)
            wd_s[...] = wd_ref[...].astype(BF16)

        for r in range(TB):
            gather(b + 1, r, xnxt, gnxt).start()
        if not first:
            for r in range(TB):
                scatter(b - 1, r, yprv, sprv).start()
        x = xcur[...]
        ms = jnp.mean(x * x, axis=-1, keepdims=True)
        xn = ((x * lax.rsqrt(ms + EPS)) * g2_ref[...]).astype(BF16)
        hm = (jax.nn.silu(_dot(xn, wg_s[...])) * _dot(xn, wu_s[...])).astype(BF16)
        ycur[...] = _dot(hm, wd_s[...])

    @pl.when(b == 0)
    def _():
        ybuf0[...] = jnp.zeros((TB, D_MODEL), F32)
        spare = [pltpu.make_async_copy(ybuf0, y2_ref.at[pl.ds(N_ASSIGN + s * TB, TB)], sem)
                 for s, sem in enumerate((ssem0, ssem1))]
        for c in spare:
            c.start()
        for c in spare:
            c.wait()
        start_rows(lambda r: gather(0, r, xbuf0, gsem0))
        step(0, first=True)

    for par in (0, 1):
        xcur, ycur, gcur, scur = bufs[par]
        _, yprv, _, sprv = bufs[1 - par]
        mine = b % 2 == par

        @pl.when(mine & (b > 0) & (b < n_act))
        def _():
            step(par, first=False)

        @pl.when(mine & (b == n_act))
        def _():
            wait_gather(xcur, gcur)

            @pl.when(b >= 2)
            def _():
                wait_scatter(ycur, scur)

            start_rows(lambda r: scatter(b - 1, r, yprv, sprv))

        @pl.when(mine & (b == n_act + 1))
        def _():
            wait_scatter(ycur, scur)


def _experts(l, block_e, n_active, inv, h1, g2, w_gate, w_up, w_down):
    eidx = lambda b, be, na, iv: (l, be[jnp.minimum(b, na[0] - 1)], 0, 0)
    row_buf = pltpu.VMEM((TB, D_MODEL), F32)
    return pl.pallas_call(
        _experts_body,
        grid_spec=pltpu.PrefetchScalarGridSpec(
            num_scalar_prefetch=3,
            grid=(N_BLOCKS + 2,),
            in_specs=[
                pl.BlockSpec(memory_space=pl.ANY),
                pl.BlockSpec((None, 1, D_MODEL), lambda b, be, na, iv: (l, 0, 0)),
                pl.BlockSpec((None, None, D_MODEL, D_EXPERT), eidx),
                pl.BlockSpec((None, None, D_MODEL, D_EXPERT), eidx),
                pl.BlockSpec((None, None, D_EXPERT, D_MODEL), eidx),
            ],
            out_specs=pl.BlockSpec(memory_space=pl.ANY),
            scratch_shapes=[
                row_buf, row_buf, row_buf, row_buf,
                pltpu.SemaphoreType.DMA, pltpu.SemaphoreType.DMA,
                pltpu.SemaphoreType.DMA, pltpu.SemaphoreType.DMA,
                pltpu.VMEM((D_MODEL, D_EXPERT), BF16),
                pltpu.VMEM((D_MODEL, D_EXPERT), BF16),
                pltpu.VMEM((D_EXPERT, D_MODEL), BF16),
            ],
        ),
        out_shape=jax.ShapeDtypeStruct((N_ASSIGN + 2 * TB, D_MODEL), F32),
        compiler_params=pltpu.CompilerParams(
            dimension_semantics=("arbitrary",), vmem_limit_bytes=VMEM_LIMIT),
        name="experts",
    )(block_e, n_active, inv, h1, g2, w_gate, w_up, w_down)


def _combine_body(final, h1_ref, rw_ref, gf_ref, y_first_ref, y_second_ref, out_ref):
    w = rw_ref[...]
    h2 = h1_ref[...] + (w[:, 0:1] * y_first_ref[...] + w[:, 1:2] * y_second_ref[...])
    if final:
        ms = jnp.mean(h2 * h2, axis=-1, keepdims=True)
        h2 = (h2 * lax.rsqrt(ms + EPS)) * gf_ref[...]
    out_ref[...] = h2


def _combine(final, h1, rw, gf, y2):
    return pl.pallas_call(
        functools.partial(_combine_body, final),
        grid=(N_TOK // TM,),
        in_specs=[
            pl.BlockSpec((TM, D_MODEL), lambda i: (i, 0)),
            pl.BlockSpec((TM, LANES), lambda i: (i, 0)),
            pl.BlockSpec((1, D_MODEL), lambda i: (0, 0)),
            pl.BlockSpec((TM, D_MODEL), lambda i: (i, 0)),
            pl.BlockSpec((TM, D_MODEL), lambda i: (N_TOK // TM + i, 0)),
        ],
        out_specs=pl.BlockSpec((TM, D_MODEL), lambda i: (i, 0)),
        out_shape=jax.ShapeDtypeStruct((N_TOK, D_MODEL), F32),
        compiler_params=pltpu.CompilerParams(
            dimension_semantics=("arbitrary",), vmem_limit_bytes=VMEM_LIMIT),
        name="combine",
    )(h1, rw, gf, y2, y2)


def _route_tables(cnt, ri):
    counts = cnt[0, N_GROUPS:N_GROUPS + N_EXPERTS].astype(jnp.int32)
    padded = (counts + TB - 1) // TB * TB
    pad_end = jnp.cumsum(padded)
    pad_start = pad_end - padded
    experts = jnp.arange(N_EXPERTS, dtype=jnp.int32)
    start_of = jnp.sum(jnp.where(ri[:, 0:TOP_K, None] == experts, pad_start, 0), axis=-1)
    dest = (start_of + ri[:, TOP_K:2 * TOP_K]).reshape(N_ASSIGN)
    n_active = (pad_end[-1:] // TB).astype(jnp.int32)
    block_row = jnp.arange(N_BLOCKS, dtype=jnp.int32)[:, None] * TB
    block_e = jnp.minimum(jnp.sum((pad_end[None, :] <= block_row).astype(jnp.int32), axis=1), N_EXPERTS - 1)
    return dest, block_e, n_active


def kernel(x, norm1_g, w_in, b_gate, conv_w, ln_v_g, ln_v_b, sgu_w, sgu_b, w_branch, w_out, norm2_g,
           router_g, router_g_b, router_e, router_e_b, w_gate, w_up, w_down, final_g):
    h = x.reshape(N_TOK, D_MODEL)

    w_in_b = w_in.astype(BF16)
    w_branch_b = w_branch.astype(BF16)
    w_out_b = w_out.astype(BF16)
    g1 = norm1_g.reshape(DEPTH, 1, D_MODEL)
    g2 = norm2_g.reshape(DEPTH, 1, D_MODEL)
    gf = final_g.reshape(1, D_MODEL)
    ln_g = ln_v_g.reshape(DEPTH, 1, WIDTH)
    ln_b = ln_v_b.reshape(DEPTH, 1, WIDTH)
    bg = b_gate.reshape(DEPTH, 1, 2 * D_MODEL)
    sgu_bias = jnp.repeat(jnp.swapaxes(sgu_b, 1, 2), GROUP_DIM, axis=2)
    pad = LANES - N_GROUPS - N_EXPERTS
    w_r = jnp.concatenate([router_g, router_e, jnp.zeros((DEPTH, D_MODEL, pad), F32)], axis=2)
    wr_hi = w_r.astype(BF16)
    wr_lo = (w_r - wr_hi.astype(F32)).astype(BF16)
    r_bias = jnp.concatenate([router_g_b, router_e_b, jnp.zeros((DEPTH, pad), F32)], axis=1)
    r_bias = r_bias.reshape(DEPTH, 1, LANES)

    for l in range(DEPTH):
        ya, yb = _mixer_in(l, h, g1, w_in_b, conv_w, ln_g, ln_b, sgu_w, sgu_bias)
        h1, ri, rw, cnt = _mixer_out(l, h, g1, ya, yb, w_in_b, bg, w_branch_b, w_out_b, g2,
                                     wr_hi, wr_lo, r_bias)
        dest, block_e, n_active = _route_tables(cnt, ri)
        y2 = _experts(l, block_e, n_active, _invert(dest), h1, g2, w_gate, w_up, w_down)
        h = _combine(l == DEPTH - 1, h1, rw, gf, y2)
    return h.reshape(BATCH, SEQ, D_MODEL)
```

```python
import functools

import jax
import jax.numpy as jnp
from jax import lax
from jax.experimental import pallas as pl
from jax.experimental.pallas import tpu as pltpu

F32 = jnp.float32
BF16 = jnp.bfloat16

D_MODEL = 2048
BATCH = 4
SEQ = 4096
DEPTH = 2
N_TOK = BATCH * SEQ
WIDTH = D_MODEL // 2
CONV_K = 3
SGU_GROUPS = 8
CHUNK = 128
GROUP_DIM = WIDTH // SGU_GROUPS
D_IN = 3 * WIDTH + 2 * WIDTH + 2 * D_MODEL
N_GROUPS = 4
PER_GROUP = 8
N_EXPERTS = N_GROUPS * PER_GROUP
TOP_K = 2
assert N_TOK & (N_TOK - 1) == 0
N_ASSIGN = N_TOK * TOP_K
D_EXPERT = D_MODEL // 4
EPS = 1e-6

LANES = 128
SUBLANES = 8
VMEM_LIMIT = 56 * 1024 * 1024

COL_C, COL_B, COL_XA = 0, WIDTH, 2 * WIDTH
COL_U, COL_V = 3 * WIDTH, 4 * WIDTH
COL_GA, COL_GB = 5 * WIDTH, 5 * WIDTH + D_MODEL

TM = 512
TN = 512
TILES_PER_SEQ = SEQ // TM
JC = WIDTH // TN
JZ = WIDTH // TN
J_IN = JC + JZ
J1 = D_MODEL // TN
J2 = D_MODEL // TN
J_OUT = J1 + J2

TB = 256
P_SLOTS = N_ASSIGN + N_EXPERTS * TB
N_BLOCKS = P_SLOTS // TB
INVERT_UNROLL = 16


def _dot(a, b):
    return jnp.dot(a, b, preferred_element_type=F32)


def _mixer_in_body(h_ref, g1_ref, wc_ref, wb_ref, wa_ref, wu_ref, wv_ref, cw_ref, lng_ref, lnb_ref,
                   sw_ref, sb_ref, ya_ref, yb_ref, xn_scr, xc_scr, carry_scr, u_scr, v_scr):
    i = pl.program_id(0)
    j = pl.program_id(1)

    @pl.when(j == 0)
    def _():
        x = h_ref[...]
        ms = jnp.mean(x * x, axis=-1, keepdims=True)
        xn_scr[...] = ((x * lax.rsqrt(ms + EPS)) * g1_ref[...]).astype(BF16)

    @pl.when(j < JC)
    def _():
        xn = xn_scr[...]
        xc = _dot(xn, wc_ref[...]) * _dot(xn, wa_ref[...])

        @pl.when(i % TILES_PER_SEQ == 0)
        def _():
            xc_scr[0:SUBLANES, :] = jnp.zeros((SUBLANES, TN), F32)

        @pl.when(i % TILES_PER_SEQ != 0)
        def _():
            xc_scr[0:SUBLANES, :] = carry_scr[j]

        xc_scr[SUBLANES:SUBLANES + TM, :] = xc
        x1 = xc_scr[SUBLANES - 1:SUBLANES - 1 + TM, :]
        x2 = xc_scr[SUBLANES - 2:SUBLANES - 2 + TM, :]
        cw = cw_ref[...]
        conv = cw[0:1, :] * x2 + cw[1:2, :] * x1 + cw[2:3, :] * xc
        carry_scr[j] = xc_scr[TM:TM + SUBLANES, :]
        ya_ref[...] = (_dot(xn, wb_ref[...]) * conv).astype(BF16)

    @pl.when(j >= JC)
    def _():
        jj = j - JC
        xn = xn_scr[...]
        u_scr[jj] = jax.nn.gelu(_dot(xn, wu_ref[...]))
        v_scr[jj] = jax.nn.gelu(_dot(xn, wv_ref[...]))

    @pl.when(j == J_IN - 1)
    def _():
        s1 = jnp.zeros((TM, 1), F32)
        for k in range(JZ):
            s1 = s1 + jnp.sum(v_scr[k], axis=-1, keepdims=True)
        mu = s1 / WIDTH
        s2 = jnp.zeros((TM, 1), F32)
        for k in range(JZ):
            d = v_scr[k] - mu
            s2 = s2 + jnp.sum(d * d, axis=-1, keepdims=True)
        rstd = lax.rsqrt(s2 / WIDTH + EPS)
        row = lax.broadcasted_iota(jnp.int32, (CHUNK, CHUNK), 0)
        col = lax.broadcasted_iota(jnp.int32, (CHUNK, CHUNK), 1)
        causal = col <= row
        gpt = TN // GROUP_DIM
        for k in range(JZ):
            vn = (((v_scr[k] - mu) * rstd) * lng_ref[:, k * TN:(k + 1) * TN]
                  + lnb_ref[:, k * TN:(k + 1) * TN]).astype(BF16)
            for gl in range(gpt):
                g = k * gpt + gl
                w = jnp.where(causal, sw_ref[g], 0.0).astype(BF16)
                bias = sb_ref[:, g * GROUP_DIM:(g + 1) * GROUP_DIM]
                for n in range(TM // CHUNK):
                    rows = slice(n * CHUNK, (n + 1) * CHUNK)
                    mixed = _dot(w, vn[rows, gl * GROUP_DIM:(gl + 1) * GROUP_DIM])
                    u = u_scr[k, rows, gl * GROUP_DIM:(gl + 1) * GROUP_DIM]
                    yb_ref[rows, g * GROUP_DIM:(g + 1) * GROUP_DIM] = (u * (mixed + bias)).astype(BF16)


def _mixer_in(l, h, g1, w_in, conv_w, ln_g, ln_b, sgu_w, sgu_bias):
    cj = lambda j: jnp.minimum(j, JC - 1)
    zj = lambda j: jnp.clip(j - JC, 0, JZ - 1)
    wspec = lambda off, f: pl.BlockSpec((None, D_MODEL, TN), lambda i, j: (l, 0, off // TN + f(j)))
    return pl.pallas_call(
        _mixer_in_body,
        grid=(N_TOK // TM, J_IN),
        in_specs=[
            pl.BlockSpec((TM, D_MODEL), lambda i, j: (i, 0)),
            pl.BlockSpec((None, 1, D_MODEL), lambda i, j: (l, 0, 0)),
            wspec(COL_C, cj), wspec(COL_B, cj), wspec(COL_XA, cj), wspec(COL_U, zj), wspec(COL_V, zj),
            pl.BlockSpec((None, CONV_K, TN), lambda i, j: (l, 0, cj(j))),
            pl.BlockSpec((None, 1, WIDTH), lambda i, j: (l, 0, 0)),
            pl.BlockSpec((None, 1, WIDTH), lambda i, j: (l, 0, 0)),
            pl.BlockSpec((None, SGU_GROUPS, CHUNK, CHUNK), lambda i, j: (l, 0, 0, 0)),
            pl.BlockSpec((None, CHUNK, WIDTH), lambda i, j: (l, 0, 0)),
        ],
        out_specs=[
            pl.BlockSpec((TM, TN), lambda i, j: (i, cj(j))),
            pl.BlockSpec((TM, WIDTH), lambda i, j: (i, 0)),
        ],
        out_shape=[jax.ShapeDtypeStruct((N_TOK, WIDTH), BF16),
                   jax.ShapeDtypeStruct((N_TOK, WIDTH), BF16)],
        scratch_shapes=[
            pltpu.VMEM((TM, D_MODEL), BF16),
            pltpu.VMEM((TM + SUBLANES, TN), F32),
            pltpu.VMEM((JC, SUBLANES, TN), F32),
            pltpu.VMEM((JZ, TM, TN), F32),
            pltpu.VMEM((JZ, TM, TN), F32),
        ],
        compiler_params=pltpu.CompilerParams(
            dimension_semantics=("arbitrary", "arbitrary"), vmem_limit_bytes=VMEM_LIMIT),
        name="mixer_in",
    )(h, g1, w_in, w_in, w_in, w_in, w_in, conv_w, ln_g, ln_b, sgu_w, sgu_bias)


def _mixer_out_body(h_ref, g1_ref, ya_ref, yb_ref, wga_ref, wgb_ref, bga_ref, bgb_ref, wba_ref, wbb_ref,
                    wo_ref, g2_ref, wr_ref, rb_ref,
                    h1_ref, ri_ref, rw_ref, cnt_ref,
                    xn_scr, mg_scr, out_scr, cnt_scr):
    i = pl.program_id(0)
    j = pl.program_id(1)

    @pl.when(j == 0)
    def _():
        x = h_ref[...]
        ms = jnp.mean(x * x, axis=-1, keepdims=True)
        xn_scr[...] = ((x * lax.rsqrt(ms + EPS)) * g1_ref[...]).astype(BF16)

    @pl.when((i == 0) & (j == 0))
    def _():
        cnt_scr[...] = jnp.zeros((1, LANES), F32)

    @pl.when(j < J1)
    def _():
        xn = xn_scr[...]
        ga = jax.nn.sigmoid(_dot(xn, wga_ref[...]) + bga_ref[...])
        gb = jax.nn.sigmoid(_dot(xn, wgb_ref[...]) + bgb_ref[...])
        mg_scr[j] = (ga * _dot(ya_ref[...], wba_ref[...]) + gb * _dot(yb_ref[...], wbb_ref[...])).astype(BF16)

    @pl.when(j >= J1)
    def _():
        acc = _dot(mg_scr[0], wo_ref[0:TN, :])
        for k in range(1, J1):
            acc = acc + _dot(mg_scr[k], wo_ref[k * TN:(k + 1) * TN, :])
        out_scr[j - J1] = acc

    @pl.when(j == J_OUT - 1)
    def _():
        ss = jnp.zeros((TM, 1), F32)
        for k in range(J2):
            cols = slice(k * TN, (k + 1) * TN)
            hk = h_ref[:, cols] + out_scr[k]
            h1_ref[:, cols] = hk
            ss = ss + jnp.sum(hk * hk, axis=-1, keepdims=True)
        rstd = lax.rsqrt(ss / D_MODEL + EPS)

        acc = jnp.zeros((TM, 2 * LANES), F32)
        for k in range(J2):
            cols = slice(k * TN, (k + 1) * TN)
            xk = (h1_ref[:, cols] * rstd) * g2_ref[:, cols]
            hi = xk.astype(BF16)
            lo = (xk - hi.astype(F32)).astype(BF16)
            acc = acc + (_dot(hi, wr_ref[cols, :]) + _dot(lo, wr_ref[cols, :]))
        lg = (acc[:, 0:LANES] + acc[:, LANES:2 * LANES]) + rb_ref[...]

        lane_i = lax.broadcasted_iota(jnp.int32, (TM, LANES), 1)
        lane = lane_i.astype(F32)
        neg = -jnp.inf
        big = float(LANES)
        is_g = lane_i < N_GROUPS
        gl = jnp.where(is_g, lg, neg)
        gmax = jnp.max(gl, axis=-1, keepdims=True)
        g_idx = jnp.min(jnp.where(gl == gmax, lane, big), axis=-1, keepdims=True)
        g_w = 1.0 / jnp.sum(jnp.exp(gl - gmax), axis=-1, keepdims=True)

        first = N_GROUPS + g_idx * PER_GROUP
        in_grp = (lane >= first) & (lane < first + PER_GROUP)
        el = jnp.where(in_grp, lg, neg)
        m1 = jnp.max(el, axis=-1, keepdims=True)
        i1 = jnp.min(jnp.where(in_grp & (el == m1), lane, big), axis=-1, keepdims=True)
        el2 = jnp.where(lane == i1, neg, el)
        m2 = jnp.max(el2, axis=-1, keepdims=True)
        i2 = jnp.min(jnp.where(in_grp & (lane != i1) & (el2 == m2), lane, big), axis=-1, keepdims=True)
        t = jnp.exp(m2 - m1)
        w1 = g_w * (1.0 / (1.0 + t))
        w2 = g_w * (t / (1.0 + t))

        onehot = jnp.where((lane == i1) | (lane == i2), 1.0, 0.0)
        r_i = lax.broadcasted_iota(jnp.int32, (TM, TM), 0)
        c_i = lax.broadcasted_iota(jnp.int32, (TM, TM), 1)
        tri = jnp.where(c_i < r_i, 1.0, 0.0).astype(BF16)
        before = cnt_scr[...] + _dot(tri, onehot.astype(BF16))
        rank1 = jnp.sum(jnp.where(lane == i1, before, 0.0), axis=-1, keepdims=True)
        rank2 = jnp.sum(jnp.where(lane == i2, before, 0.0), axis=-1, keepdims=True)
        cnt_scr[...] = cnt_scr[...] + jnp.sum(onehot, axis=0, keepdims=True)

        ri = jnp.where(lane_i == 0, i1 - N_GROUPS,
                       jnp.where(lane_i == 1, i2 - N_GROUPS,
                                 jnp.where(lane_i == 2, rank1, jnp.where(lane_i == 3, rank2, 0.0))))
        ri_ref[...] = ri.astype(jnp.int32)
        rw_ref[...] = jnp.where(lane_i == 0, w1, jnp.where(lane_i == 1, w2, 0.0))
        cnt_ref[...] = jnp.broadcast_to(cnt_scr[...], (SUBLANES, LANES))


def _mixer_out(l, h, g1, ya, yb, w_in, b_gate, w_branch, w_out, g2, w_router, r_bias):
    mj = lambda j: jnp.minimum(j, J1 - 1)
    oj = lambda j: jnp.clip(j - J1, 0, J2 - 1)
    const = lambda *blk: pl.BlockSpec((None,) + blk, lambda i, j: (l,) + (0,) * len(blk))
    return pl.pallas_call(
        _mixer_out_body,
        grid=(N_TOK // TM, J_OUT),
        in_specs=[
            pl.BlockSpec((TM, D_MODEL), lambda i, j: (i, 0)),
            const(1, D_MODEL),
            pl.BlockSpec((TM, WIDTH), lambda i, j: (i, 0)),
            pl.BlockSpec((TM, WIDTH), lambda i, j: (i, 0)),
            pl.BlockSpec((None, D_MODEL, TN), lambda i, j: (l, 0, COL_GA // TN + mj(j))),
            pl.BlockSpec((None, D_MODEL, TN), lambda i, j: (l, 0, COL_GB // TN + mj(j))),
            pl.BlockSpec((None, 1, TN), lambda i, j: (l, 0, mj(j))),
            pl.BlockSpec((None, 1, TN), lambda i, j: (l, 0, D_MODEL // TN + mj(j))),
            pl.BlockSpec((None, None, WIDTH, TN), lambda i, j: (l, 0, 0, mj(j))),
            pl.BlockSpec((None, None, WIDTH, TN), lambda i, j: (l, 1, 0, mj(j))),
            pl.BlockSpec((None, D_MODEL, TN), lambda i, j: (l, 0, oj(j))),
            const(1, D_MODEL),
            const(D_MODEL, 2 * LANES), const(1, LANES),
        ],
        out_specs=[
            pl.BlockSpec((TM, D_MODEL), lambda i, j: (i, 0)),
            pl.BlockSpec((TM, LANES), lambda i, j: (i, 0)),
            pl.BlockSpec((TM, LANES), lambda i, j: (i, 0)),
            pl.BlockSpec((SUBLANES, LANES), lambda i, j: (0, 0)),
        ],
        out_shape=[jax.ShapeDtypeStruct((N_TOK, D_MODEL), F32),
                   jax.ShapeDtypeStruct((N_TOK, LANES), jnp.int32),
                   jax.ShapeDtypeStruct((N_TOK, LANES), F32),
                   jax.ShapeDtypeStruct((SUBLANES, LANES), F32)],
        scratch_shapes=[
            pltpu.VMEM((TM, D_MODEL), BF16),
            pltpu.VMEM((J1, TM, TN), BF16),
            pltpu.VMEM((J2, TM, TN), F32),
            pltpu.VMEM((1, LANES), F32),
        ],
        compiler_params=pltpu.CompilerParams(
            dimension_semantics=("arbitrary", "arbitrary"), vmem_limit_bytes=VMEM_LIMIT),
        name="mixer_out",
    )(h, g1, ya, yb, w_in, w_in, b_gate, b_gate, w_branch, w_branch, w_out, g2, w_router, r_bias)


def _invert_body(dest_ref, spare_ref, inv_ref, sem):
    fill = pltpu.make_async_copy(spare_ref, inv_ref, sem)
    fill.start()
    fill.wait()

    def place(c, carry):
        t0 = c * INVERT_UNROLL
        for u in range(INVERT_UNROLL):
            for k in range(TOP_K):
                inv_ref[dest_ref[(t0 + u) * TOP_K + k]] = k * N_TOK + t0 + u
        return carry

    lax.fori_loop(0, N_TOK // INVERT_UNROLL, place, 0)


def _invert(dest):
    spare = N_ASSIGN + jnp.arange(P_SLOTS, dtype=jnp.int32) % (2 * TB)
    return pl.pallas_call(
        _invert_body,
        in_specs=[pl.BlockSpec(memory_space=pltpu.SMEM), pl.BlockSpec(memory_space=pl.ANY)],
        out_specs=pl.BlockSpec(memory_space=pltpu.SMEM),
        out_shape=jax.ShapeDtypeStruct((P_SLOTS,), jnp.int32),
        scratch_shapes=[pltpu.SemaphoreType.DMA],
        name="invert",
    )(dest, spare)


def _experts_body(l, be_ref, nx_ref, na_ref, inv_ref, h1_ref, g2_ref, wg_ref, wu_ref, wd_ref, y2_ref,
                  xbuf0, xbuf1, ybuf0, ybuf1, gsem0, gsem1, ssem0, ssem1,
                  wg_f, wu_f, wd_f, wsem, wslot, wg_s, wu_s, wd_s):
    b = pl.program_id(0)
    n_act = na_ref[0]
    bufs = ((xbuf0, ybuf0, gsem0, ssem0), (xbuf1, ybuf1, gsem1, ssem1))

    def weight_copies(e, s):
        return [pltpu.make_async_copy(src.at[l, e], dst.at[s], wsem.at[s])
                for src, dst in ((wg_ref, wg_f), (wu_ref, wu_f), (wd_ref, wd_f))]

    def gather(blk, r, xdst, sem):
        tok = inv_ref[blk * TB + r] & (N_TOK - 1)
        return pltpu.make_async_copy(h1_ref.at[pl.ds(tok, 1)], xdst.at[pl.ds(r, 1)], sem)

    def scatter(blk, r, ysrc, sem):
        return pltpu.make_async_copy(
            ysrc.at[pl.ds(r, 1)], y2_ref.at[pl.ds(inv_ref[blk * TB + r], 1)], sem)

    def wait_gather(xdst, sem):
        pltpu.make_async_copy(h1_ref.at[pl.ds(0, TB)], xdst, sem).wait()

    def wait_scatter(ysrc, sem):
        pltpu.make_async_copy(ysrc, y2_ref.at[pl.ds(0, TB)], sem).wait()

    def start_rows(copy):
        def body(r, carry):
            copy(r).start()
            return carry
        lax.fori_loop(0, TB, body, 0)

    def step(par, first):
        xcur, ycur, gcur, scur = bufs[par]
        xnxt, yprv, gnxt, sprv = bufs[1 - par]
        wait_gather(xcur, gcur)
        if not first:
            @pl.when(b >= 2)
            def _():
                wait_scatter(ycur, scur)

        @pl.when((b == 0) | (be_ref[b] != be_ref[jnp.maximum(b - 1, 0)]))
        def _():
            s = wslot[0]
            for c in weight_copies(be_ref[b], s):
                c.wait()
            wg_s[...] = wg_f[s].astype(BF16)
            wu_s[...] = wu_f[s].astype(BF16)
            wd_s[...] = wd_f[s].astype(BF16)

            @pl.when(nx_ref[b] >= 0)
            def _():
                for c in weight_copies(nx_ref[b], 1 - s):
                    c.start()

            wslot[0] = 1 - s

        for r in range(TB):
            gather(b + 1, r, xnxt, gnxt).start(priority=r % 2)
        if not first:
            for r in range(TB):
                scatter(b - 1, r, yprv, sprv).start(priority=r % 2)
        x = xcur[...]
        ms = jnp.mean(x * x, axis=-1, keepdims=True)
        xn = ((x * lax.rsqrt(ms + EPS)) * g2_ref[...]).astype(BF16)
        hm = (jax.nn.silu(_dot(xn, wg_s[...])) * _dot(xn, wu_s[...])).astype(BF16)
        ycur[...] = _dot(hm, wd_s[...])

    @pl.when(b == 0)
    def _():
        ybuf0[...] = jnp.zeros((TB, D_MODEL), F32)
        spare = [pltpu.make_async_copy(ybuf0, y2_ref.at[pl.ds(N_ASSIGN + s * TB, TB)], sem)
                 for s, sem in enumerate((ssem0, ssem1))]
        for c in spare:
            c.start()
        for c in spare:
            c.wait()
        start_rows(lambda r: gather(0, r, xbuf0, gsem0))
        wslot[0] = 0
        for c in weight_copies(be_ref[0], 0):
            c.start()
        step(0, first=True)

    for par in (0, 1):
        xcur, ycur, gcur, scur = bufs[par]
        _, yprv, _, sprv = bufs[1 - par]
        mine = b % 2 == par

        @pl.when(mine & (b > 0) & (b < n_act))
        def _():
            step(par, first=False)

        @pl.when(mine & (b == n_act))
        def _():
            wait_gather(xcur, gcur)

            @pl.when(b >= 2)
            def _():
                wait_scatter(ycur, scur)

            start_rows(lambda r: scatter(b - 1, r, yprv, sprv))

        @pl.when(mine & (b == n_act + 1))
        def _():
            wait_scatter(ycur, scur)


def _experts(l, block_e, next_e, n_active, inv, h1, g2, w_gate, w_up, w_down):
    row_buf = pltpu.VMEM((TB, D_MODEL), F32)
    in_out = (D_MODEL, D_EXPERT)
    out_in = (D_EXPERT, D_MODEL)
    return pl.pallas_call(
        functools.partial(_experts_body, l),
        grid_spec=pltpu.PrefetchScalarGridSpec(
            num_scalar_prefetch=4,
            grid=(N_BLOCKS + 2,),
            in_specs=[
                pl.BlockSpec(memory_space=pl.ANY),
                pl.BlockSpec((None, 1, D_MODEL), lambda b, be, nx, na, iv: (l, 0, 0)),
                pl.BlockSpec(memory_space=pl.ANY),
                pl.BlockSpec(memory_space=pl.ANY),
                pl.BlockSpec(memory_space=pl.ANY),
            ],
            out_specs=pl.BlockSpec(memory_space=pl.ANY),
            scratch_shapes=[
                row_buf, row_buf, row_buf, row_buf,
                pltpu.SemaphoreType.DMA, pltpu.SemaphoreType.DMA,
                pltpu.SemaphoreType.DMA, pltpu.SemaphoreType.DMA,
                pltpu.VMEM((2,) + in_out, F32), pltpu.VMEM((2,) + in_out, F32), pltpu.VMEM((2,) + out_in, F32),
                pltpu.SemaphoreType.DMA((2,)),
                pltpu.SMEM((1,), jnp.int32),
                pltpu.VMEM(in_out, BF16), pltpu.VMEM(in_out, BF16), pltpu.VMEM(out_in, BF16),
            ],
        ),
        out_shape=jax.ShapeDtypeStruct((N_ASSIGN + 2 * TB, D_MODEL), F32),
        compiler_params=pltpu.CompilerParams(
            dimension_semantics=("arbitrary",), vmem_limit_bytes=VMEM_LIMIT),
        name="experts",
    )(block_e, next_e, n_active, inv, h1, g2, w_gate, w_up, w_down)


def _combine_body(final, h1_ref, rw_ref, gf_ref, y_first_ref, y_second_ref, out_ref):
    w = rw_ref[...]
    h2 = h1_ref[...] + (w[:, 0:1] * y_first_ref[...] + w[:, 1:2] * y_second_ref[...])
    if final:
        ms = jnp.mean(h2 * h2, axis=-1, keepdims=True)
        h2 = (h2 * lax.rsqrt(ms + EPS)) * gf_ref[...]
    out_ref[...] = h2


def _combine(final, h1, rw, gf, y2):
    return pl.pallas_call(
        functools.partial(_combine_body, final),
        grid=(N_TOK // TM,),
        in_specs=[
            pl.BlockSpec((TM, D_MODEL), lambda i: (i, 0)),
            pl.BlockSpec((TM, LANES), lambda i: (i, 0)),
            pl.BlockSpec((1, D_MODEL), lambda i: (0, 0)),
            pl.BlockSpec((TM, D_MODEL), lambda i: (i, 0)),
            pl.BlockSpec((TM, D_MODEL), lambda i: (N_TOK // TM + i, 0)),
        ],
        out_specs=pl.BlockSpec((TM, D_MODEL), lambda i: (i, 0)),
        out_shape=jax.ShapeDtypeStruct((N_TOK, D_MODEL), F32),
        compiler_params=pltpu.CompilerParams(
            dimension_semantics=("arbitrary",), vmem_limit_bytes=VMEM_LIMIT),
        name="combine",
    )(h1, rw, gf, y2, y2)


def _route_tables(cnt, ri):
    counts = cnt[0, N_GROUPS:N_GROUPS + N_EXPERTS].astype(jnp.int32)
    padded = (counts + TB - 1) // TB * TB
    pad_end = jnp.cumsum(padded)
    pad_start = pad_end - padded
    experts = jnp.arange(N_EXPERTS, dtype=jnp.int32)
    start_of = jnp.sum(jnp.where(ri[:, 0:TOP_K, None] == experts, pad_start, 0), axis=-1)
    dest = (start_of + ri[:, TOP_K:2 * TOP_K]).reshape(N_ASSIGN)
    n_active = (pad_end[-1:] // TB).astype(jnp.int32)
    block_row = jnp.arange(N_BLOCKS, dtype=jnp.int32)[:, None] * TB
    block_e = jnp.minimum(jnp.sum((pad_end[None, :] <= block_row).astype(jnp.int32), axis=1), N_EXPERTS - 1)
    run_end = jnp.sum(jnp.where(block_e[:, None] == experts, pad_end, 0), axis=-1) // TB
    next_e = jnp.where(run_end < n_active[0], block_e[jnp.minimum(run_end, N_BLOCKS - 1)], -1)
    return dest, block_e, next_e.astype(jnp.int32), n_active


def kernel(x, norm1_g, w_in, b_gate, conv_w, ln_v_g, ln_v_b, sgu_w, sgu_b, w_branch, w_out, norm2_g,
           router_g, router_g_b, router_e, router_e_b, w_gate, w_up, w_down, final_g):
    h = x.reshape(N_TOK, D_MODEL)

    w_in_b = w_in.astype(BF16)
    w_branch_b = w_branch.astype(BF16)
    w_out_b = w_out.astype(BF16)
    g1 = norm1_g.reshape(DEPTH, 1, D_MODEL)
    g2 = norm2_g.reshape(DEPTH, 1, D_MODEL)
    gf = final_g.reshape(1, D_MODEL)
    ln_g = ln_v_g.reshape(DEPTH, 1, WIDTH)
    ln_b = ln_v_b.reshape(DEPTH, 1, WIDTH)
    bg = b_gate.reshape(DEPTH, 1, 2 * D_MODEL)
    sgu_bias = jnp.repeat(jnp.swapaxes(sgu_b, 1, 2), GROUP_DIM, axis=2)
    pad = LANES - N_GROUPS - N_EXPERTS
    w_r = jnp.concatenate([router_g, router_e, jnp.zeros((DEPTH, D_MODEL, pad), F32)], axis=2)
    wr_hi = w_r.astype(BF16)
    w_router = jnp.concatenate([wr_hi, (w_r - wr_hi.astype(F32)).astype(BF16)], axis=2)
    r_bias = jnp.concatenate([router_g_b, router_e_b, jnp.zeros((DEPTH, pad), F32)], axis=1)
    r_bias = r_bias.reshape(DEPTH, 1, LANES)

    for l in range(DEPTH):
        ya, yb = _mixer_in(l, h, g1, w_in_b, conv_w, ln_g, ln_b, sgu_w, sgu_bias)
        h1, ri, rw, cnt = _mixer_out(l, h, g1, ya, yb, w_in_b, bg, w_branch_b, w_out_b, g2,
                                     w_router, r_bias)
        dest, block_e, next_e, n_active = _route_tables(cnt, ri)
        y2 = _experts(l, block_e, next_e, n_active, _invert(dest), h1, g2, w_gate, w_up, w_down)
        h = _combine(l == DEPTH - 1, h1, rw, gf, y2)
    return h.reshape(BATCH, SEQ, D_MODEL)
```

```python
import functools

import jax
import jax.numpy as jnp
from jax import lax
from jax.experimental import pallas as pl
from jax.experimental.pallas import tpu as pltpu

F32 = jnp.float32
BF16 = jnp.bfloat16

D_MODEL = 2048
BATCH = 4
SEQ = 4096
DEPTH = 2
N_TOK = BATCH * SEQ
WIDTH = D_MODEL // 2
CONV_K = 3
SGU_GROUPS = 8
CHUNK = 128
GROUP_DIM = WIDTH // SGU_GROUPS
D_IN = 3 * WIDTH + 2 * WIDTH + 2 * D_MODEL
N_GROUPS = 4
PER_GROUP = 8
N_EXPERTS = N_GROUPS * PER_GROUP
TOP_K = 2
assert N_TOK & (N_TOK - 1) == 0
N_ASSIGN = N_TOK * TOP_K
D_EXPERT = D_MODEL // 4
EPS = 1e-6

LANES = 128
SUBLANES = 8
VMEM_LIMIT = 56 * 1024 * 1024

COL_C, COL_B, COL_XA = 0, WIDTH, 2 * WIDTH
COL_U, COL_V = 3 * WIDTH, 4 * WIDTH
COL_GA, COL_GB = 5 * WIDTH, 5 * WIDTH + D_MODEL

TM = 512
TN = 512
TILES_PER_SEQ = SEQ // TM
JC = WIDTH // TN
JZ = WIDTH // TN
J_IN = JC + JZ
J1 = D_MODEL // TN
J2 = D_MODEL // TN
J_OUT = J1 + J2

PACKED = D_MODEL // 2
assert (J2 // 2) * TN == PACKED
TB = 256
P_SLOTS = N_ASSIGN + N_EXPERTS * TB
N_BLOCKS = P_SLOTS // TB
INVERT_UNROLL = 16


def _dot(a, b):
    return jnp.dot(a, b, preferred_element_type=F32)


def _pack_halves(lo, hi):
    lo_bits = pltpu.bitcast(lo.astype(BF16).astype(F32), jnp.uint32)
    hi_bits = pltpu.bitcast(hi.astype(BF16).astype(F32), jnp.uint32)
    return lax.shift_right_logical(lo_bits, jnp.uint32(16)) | (hi_bits & jnp.uint32(0xFFFF0000))


def _unpack_halves(words):
    lo = pltpu.bitcast(lax.shift_left(words, jnp.uint32(16)), F32)
    hi = pltpu.bitcast(words & jnp.uint32(0xFFFF0000), F32)
    return lo, hi


def _mixer_in_body(h_ref, g1_ref, wc_ref, wb_ref, wa_ref, wu_ref, wv_ref, cw_ref, lng_ref, lnb_ref,
                   sw_ref, sb_ref, ya_ref, yb_ref, xn_scr, xc_scr, carry_scr, u_scr, v_scr):
    i = pl.program_id(0)
    j = pl.program_id(1)

    @pl.when(j == 0)
    def _():
        x = h_ref[...]
        ms = jnp.mean(x * x, axis=-1, keepdims=True)
        xn_scr[...] = ((x * lax.rsqrt(ms + EPS)) * g1_ref[...]).astype(BF16)

    @pl.when(j < JC)
    def _():
        xn = xn_scr[...]
        xc = _dot(xn, wc_ref[...]) * _dot(xn, wa_ref[...])

        @pl.when(i % TILES_PER_SEQ == 0)
        def _():
            xc_scr[0:SUBLANES, :] = jnp.zeros((SUBLANES, TN), F32)

        @pl.when(i % TILES_PER_SEQ != 0)
        def _():
            xc_scr[0:SUBLANES, :] = carry_scr[j]

        xc_scr[SUBLANES:SUBLANES + TM, :] = xc
        x1 = xc_scr[SUBLANES - 1:SUBLANES - 1 + TM, :]
        x2 = xc_scr[SUBLANES - 2:SUBLANES - 2 + TM, :]
        cw = cw_ref[...]
        conv = cw[0:1, :] * x2 + cw[1:2, :] * x1 + cw[2:3, :] * xc
        carry_scr[j] = xc_scr[TM:TM + SUBLANES, :]
        ya_ref[...] = (_dot(xn, wb_ref[...]) * conv).astype(BF16)

    @pl.when(j >= JC)
    def _():
        jj = j - JC
        xn = xn_scr[...]
        u_scr[jj] = jax.nn.gelu(_dot(xn, wu_ref[...]))
        v_scr[jj] = jax.nn.gelu(_dot(xn, wv_ref[...]))

    @pl.when(j == J_IN - 1)
    def _():
        s1 = jnp.zeros((TM, 1), F32)
        for k in range(JZ):
            s1 = s1 + jnp.sum(v_scr[k], axis=-1, keepdims=True)
        mu = s1 / WIDTH
        s2 = jnp.zeros((TM, 1), F32)
        for k in range(JZ):
            d = v_scr[k] - mu
            s2 = s2 + jnp.sum(d * d, axis=-1, keepdims=True)
        rstd = lax.rsqrt(s2 / WIDTH + EPS)
        row = lax.broadcasted_iota(jnp.int32, (CHUNK, CHUNK), 0)
        col = lax.broadcasted_iota(jnp.int32, (CHUNK, CHUNK), 1)
        causal = col <= row
        gpt = TN // GROUP_DIM
        for k in range(JZ):
            vn = (((v_scr[k] - mu) * rstd) * lng_ref[:, k * TN:(k + 1) * TN]
                  + lnb_ref[:, k * TN:(k + 1) * TN]).astype(BF16)
            for gl in range(gpt):
                g = k * gpt + gl
                w = jnp.where(causal, sw_ref[g], 0.0).astype(BF16)
                bias = sb_ref[:, g * GROUP_DIM:(g + 1) * GROUP_DIM]
                for n in range(TM // CHUNK):
                    rows = slice(n * CHUNK, (n + 1) * CHUNK)
                    mixed = _dot(w, vn[rows, gl * GROUP_DIM:(gl + 1) * GROUP_DIM])
                    u = u_scr[k, rows, gl * GROUP_DIM:(gl + 1) * GROUP_DIM]
                    yb_ref[rows, g * GROUP_DIM:(g + 1) * GROUP_DIM] = (u * (mixed + bias)).astype(BF16)


def _mixer_in(l, h, g1, w_in, conv_w, ln_g, ln_b, sgu_w, sgu_bias):
    cj = lambda j: jnp.minimum(j, JC - 1)
    zj = lambda j: jnp.clip(j - JC, 0, JZ - 1)
    wspec = lambda off, f: pl.BlockSpec((None, D_MODEL, TN), lambda i, j: (l, 0, off // TN + f(j)))
    return pl.pallas_call(
        _mixer_in_body,
        grid=(N_TOK // TM, J_IN),
        in_specs=[
            pl.BlockSpec((TM, D_MODEL), lambda i, j: (i, 0)),
            pl.BlockSpec((None, 1, D_MODEL), lambda i, j: (l, 0, 0)),
            wspec(COL_C, cj), wspec(COL_B, cj), wspec(COL_XA, cj), wspec(COL_U, zj), wspec(COL_V, zj),
            pl.BlockSpec((None, CONV_K, TN), lambda i, j: (l, 0, cj(j))),
            pl.BlockSpec((None, 1, WIDTH), lambda i, j: (l, 0, 0)),
            pl.BlockSpec((None, 1, WIDTH), lambda i, j: (l, 0, 0)),
            pl.BlockSpec((None, SGU_GROUPS, CHUNK, CHUNK), lambda i, j: (l, 0, 0, 0)),
            pl.BlockSpec((None, CHUNK, WIDTH), lambda i, j: (l, 0, 0)),
        ],
        out_specs=[
            pl.BlockSpec((TM, TN), lambda i, j: (i, cj(j))),
            pl.BlockSpec((TM, WIDTH), lambda i, j: (i, 0)),
        ],
        out_shape=[jax.ShapeDtypeStruct((N_TOK, WIDTH), BF16),
                   jax.ShapeDtypeStruct((N_TOK, WIDTH), BF16)],
        scratch_shapes=[
            pltpu.VMEM((TM, D_MODEL), BF16),
            pltpu.VMEM((TM + SUBLANES, TN), F32),
            pltpu.VMEM((JC, SUBLANES, TN), F32),
            pltpu.VMEM((JZ, TM, TN), F32),
            pltpu.VMEM((JZ, TM, TN), F32),
        ],
        compiler_params=pltpu.CompilerParams(
            dimension_semantics=("arbitrary", "arbitrary"), vmem_limit_bytes=VMEM_LIMIT),
        name="mixer_in",
    )(h, g1, w_in, w_in, w_in, w_in, w_in, conv_w, ln_g, ln_b, sgu_w, sgu_bias)


def _mixer_out_body(h_ref, g1_ref, ya_ref, yb_ref, wga_ref, wgb_ref, bga_ref, bgb_ref, wba_ref, wbb_ref,
                    wo_ref, g2_ref, wr_ref, rb_ref,
                    h1_ref, xp_ref, ri_ref, rw_ref, cnt_ref,
                    xn_scr, mg_scr, out_scr, cnt_scr):
    i = pl.program_id(0)
    j = pl.program_id(1)

    @pl.when(j == 0)
    def _():
        x = h_ref[...]
        ms = jnp.mean(x * x, axis=-1, keepdims=True)
        xn_scr[...] = ((x * lax.rsqrt(ms + EPS)) * g1_ref[...]).astype(BF16)

    @pl.when((i == 0) & (j == 0))
    def _():
        cnt_scr[...] = jnp.zeros((1, LANES), F32)

    @pl.when(j < J1)
    def _():
        xn = xn_scr[...]
        ga = jax.nn.sigmoid(_dot(xn, wga_ref[...]) + bga_ref[...])
        gb = jax.nn.sigmoid(_dot(xn, wgb_ref[...]) + bgb_ref[...])
        mg_scr[j] = (ga * _dot(ya_ref[...], wba_ref[...]) + gb * _dot(yb_ref[...], wbb_ref[...])).astype(BF16)

    @pl.when(j >= J1)
    def _():
        acc = _dot(mg_scr[0], wo_ref[0:TN, :])
        for k in range(1, J1):
            acc = acc + _dot(mg_scr[k], wo_ref[k * TN:(k + 1) * TN, :])
        out_scr[j - J1] = acc

    @pl.when(j == J_OUT - 1)
    def _():
        ss = jnp.zeros((TM, 1), F32)
        for k in range(J2):
            cols = slice(k * TN, (k + 1) * TN)
            hk = h_ref[:, cols] + out_scr[k]
            h1_ref[:, cols] = hk
            ss = ss + jnp.sum(hk * hk, axis=-1, keepdims=True)
        rstd = lax.rsqrt(ss / D_MODEL + EPS)

        acc = jnp.zeros((TM, 2 * LANES), F32)
        for k in range(J2):
            cols = slice(k * TN, (k + 1) * TN)
            xk = (h1_ref[:, cols] * rstd) * g2_ref[:, cols]
            hi = xk.astype(BF16)
            lo = (xk - hi.astype(F32)).astype(BF16)
            acc = acc + (_dot(hi, wr_ref[cols, :]) + _dot(lo, wr_ref[cols, :]))
            bits = pltpu.bitcast(hi.astype(F32), jnp.uint32)
            if k < J2 // 2:
                xp_ref[:, cols] = lax.shift_right_logical(bits, jnp.uint32(16))
            else:
                pcols = slice(k * TN - PACKED, (k + 1) * TN - PACKED)
                xp_ref[:, pcols] = xp_ref[:, pcols] | (bits & jnp.uint32(0xFFFF0000))
        lg = (acc[:, 0:LANES] + acc[:, LANES:2 * LANES]) + rb_ref[...]

        lane_i = lax.broadcasted_iota(jnp.int32, (TM, LANES), 1)
        lane = lane_i.astype(F32)
        neg = -jnp.inf
        big = float(LANES)
        is_g = lane_i < N_GROUPS
        gl = jnp.where(is_g, lg, neg)
        gmax = jnp.max(gl, axis=-1, keepdims=True)
        g_idx = jnp.min(jnp.where(gl == gmax, lane, big), axis=-1, keepdims=True)
        g_w = 1.0 / jnp.sum(jnp.exp(gl - gmax), axis=-1, keepdims=True)

        first = N_GROUPS + g_idx * PER_GROUP
        in_grp = (lane >= first) & (lane < first + PER_GROUP)
        el = jnp.where(in_grp, lg, neg)
        m1 = jnp.max(el, axis=-1, keepdims=True)
        i1 = jnp.min(jnp.where(in_grp & (el == m1), lane, big), axis=-1, keepdims=True)
        el2 = jnp.where(lane == i1, neg, el)
        m2 = jnp.max(el2, axis=-1, keepdims=True)
        i2 = jnp.min(jnp.where(in_grp & (lane != i1) & (el2 == m2), lane, big), axis=-1, keepdims=True)
        t = jnp.exp(m2 - m1)
        w1 = g_w * (1.0 / (1.0 + t))
        w2 = g_w * (t / (1.0 + t))

        onehot = jnp.where((lane == i1) | (lane == i2), 1.0, 0.0)
        r_i = lax.broadcasted_iota(jnp.int32, (TM, TM), 0)
        c_i = lax.broadcasted_iota(jnp.int32, (TM, TM), 1)
        tri = jnp.where(c_i < r_i, 1.0, 0.0).astype(BF16)
        before = cnt_scr[...] + _dot(tri, onehot.astype(BF16))
        rank1 = jnp.sum(jnp.where(lane == i1, before, 0.0), axis=-1, keepdims=True)
        rank2 = jnp.sum(jnp.where(lane == i2, before, 0.0), axis=-1, keepdims=True)
        cnt_scr[...] = cnt_scr[...] + jnp.sum(onehot, axis=0, keepdims=True)

        ri = jnp.where(lane_i == 0, i1 - N_GROUPS,
                       jnp.where(lane_i == 1, i2 - N_GROUPS,
                                 jnp.where(lane_i == 2, rank1, jnp.where(lane_i == 3, rank2, 0.0))))
        ri_ref[...] = ri.astype(jnp.int32)
        rw_ref[...] = jnp.where(lane_i == 0, w1, jnp.where(lane_i == 1, w2, 0.0))
        cnt_ref[...] = jnp.broadcast_to(cnt_scr[...], (SUBLANES, LANES))


def _mixer_out(l, h, g1, ya, yb, w_in, b_gate, w_branch, w_out, g2, w_router, r_bias):
    mj = lambda j: jnp.minimum(j, J1 - 1)
    oj = lambda j: jnp.clip(j - J1, 0, J2 - 1)
    const = lambda *blk: pl.BlockSpec((None,) + blk, lambda i, j: (l,) + (0,) * len(blk))
    return pl.pallas_call(
        _mixer_out_body,
        grid=(N_TOK // TM, J_OUT),
        in_specs=[
            pl.BlockSpec((TM, D_MODEL), lambda i, j: (i, 0)),
            const(1, D_MODEL),
            pl.BlockSpec((TM, WIDTH), lambda i, j: (i, 0)),
            pl.BlockSpec((TM, WIDTH), lambda i, j: (i, 0)),
            pl.BlockSpec((None, D_MODEL, TN), lambda i, j: (l, 0, COL_GA // TN + mj(j))),
            pl.BlockSpec((None, D_MODEL, TN), lambda i, j: (l, 0, COL_GB // TN + mj(j))),
            pl.BlockSpec((None, 1, TN), lambda i, j: (l, 0, mj(j))),
            pl.BlockSpec((None, 1, TN), lambda i, j: (l, 0, D_MODEL // TN + mj(j))),
            pl.BlockSpec((None, None, WIDTH, TN), lambda i, j: (l, 0, 0, mj(j))),
            pl.BlockSpec((None, None, WIDTH, TN), lambda i, j: (l, 1, 0, mj(j))),
            pl.BlockSpec((None, D_MODEL, TN), lambda i, j: (l, 0, oj(j))),
            const(1, D_MODEL),
            const(D_MODEL, 2 * LANES), const(1, LANES),
        ],
        out_specs=[
            pl.BlockSpec((TM, D_MODEL), lambda i, j: (i, 0)),
            pl.BlockSpec((TM, PACKED), lambda i, j: (i, 0)),
            pl.BlockSpec((TM, LANES), lambda i, j: (i, 0)),
            pl.BlockSpec((TM, LANES), lambda i, j: (i, 0)),
            pl.BlockSpec((SUBLANES, LANES), lambda i, j: (0, 0)),
        ],
        out_shape=[jax.ShapeDtypeStruct((N_TOK, D_MODEL), F32),
                   jax.ShapeDtypeStruct((N_TOK, PACKED), jnp.uint32),
                   jax.ShapeDtypeStruct((N_TOK, LANES), jnp.int32),
                   jax.ShapeDtypeStruct((N_TOK, LANES), F32),
                   jax.ShapeDtypeStruct((SUBLANES, LANES), F32)],
        scratch_shapes=[
            pltpu.VMEM((TM, D_MODEL), BF16),
            pltpu.VMEM((J1, TM, TN), BF16),
            pltpu.VMEM((J2, TM, TN), F32),
            pltpu.VMEM((1, LANES), F32),
        ],
        compiler_params=pltpu.CompilerParams(
            dimension_semantics=("arbitrary", "arbitrary"), vmem_limit_bytes=VMEM_LIMIT),
        name="mixer_out",
    )(h, g1, ya, yb, w_in, w_in, b_gate, b_gate, w_branch, w_branch, w_out, g2, w_router, r_bias)


def _invert_body(dest_ref, spare_ref, inv_ref, sem):
    fill = pltpu.make_async_copy(spare_ref, inv_ref, sem)
    fill.start()
    fill.wait()

    def place(c, carry):
        t0 = c * INVERT_UNROLL
        for u in range(INVERT_UNROLL):
            for k in range(TOP_K):
                inv_ref[dest_ref[(t0 + u) * TOP_K + k]] = k * N_TOK + t0 + u
        return carry

    lax.fori_loop(0, N_TOK // INVERT_UNROLL, place, 0)


def _invert(dest):
    spare = N_ASSIGN + jnp.arange(P_SLOTS, dtype=jnp.int32) % (2 * TB)
    return pl.pallas_call(
        _invert_body,
        in_specs=[pl.BlockSpec(memory_space=pltpu.SMEM), pl.BlockSpec(memory_space=pl.ANY)],
        out_specs=pl.BlockSpec(memory_space=pltpu.SMEM),
        out_shape=jax.ShapeDtypeStruct((P_SLOTS,), jnp.int32),
        scratch_shapes=[pltpu.SemaphoreType.DMA],
        name="invert",
    )(dest, spare)


def _experts_body(l, be_ref, nx_ref, na_ref, inv_ref, xp_ref, wg_ref, wu_ref, wd_ref, y2_ref,
                  xbuf0, xbuf1, ybuf0, ybuf1, gsem0, gsem1, ssem0, ssem1,
                  wg_f, wu_f, wd_f, wsem, wslot, wg_s, wu_s, wd_s):
    b = pl.program_id(0)
    n_act = na_ref[0]
    bufs = ((xbuf0, ybuf0, gsem0, ssem0), (xbuf1, ybuf1, gsem1, ssem1))

    def weight_copies(e, s):
        return [pltpu.make_async_copy(src.at[l, e], dst.at[s], wsem.at[s])
                for src, dst in ((wg_ref, wg_f), (wu_ref, wu_f), (wd_ref, wd_f))]

    def gather(blk, r, xdst, sem):
        tok = inv_ref[blk * TB + r] & (N_TOK - 1)
        return pltpu.make_async_copy(xp_ref.at[pl.ds(tok, 1)], xdst.at[pl.ds(r, 1)], sem)

    def scatter(blk, r, ysrc, sem):
        return pltpu.make_async_copy(
            ysrc.at[pl.ds(r, 1)], y2_ref.at[pl.ds(inv_ref[blk * TB + r], 1)], sem)

    def wait_gather(xdst, sem):
        pltpu.make_async_copy(xp_ref.at[pl.ds(0, TB)], xdst, sem).wait()

    def wait_scatter(ysrc, sem):
        pltpu.make_async_copy(ysrc, y2_ref.at[pl.ds(0, TB)], sem).wait()

    def start_rows(copy):
        def body(r, carry):
            copy(r).start()
            return carry
        lax.fori_loop(0, TB, body, 0)

    def step(par, first):
        xcur, ycur, gcur, scur = bufs[par]
        xnxt, yprv, gnxt, sprv = bufs[1 - par]
        wait_gather(xcur, gcur)
        if not first:
            @pl.when(b >= 2)
            def _():
                wait_scatter(ycur, scur)

        @pl.when((b == 0) | (be_ref[b] != be_ref[jnp.maximum(b - 1, 0)]))
        def _():
            s = wslot[0]
            for c in weight_copies(be_ref[b], s):
                c.wait()
            wg_s[...] = wg_f[s].astype(BF16)
            wu_s[...] = wu_f[s].astype(BF16)
            wd_s[...] = wd_f[s].astype(BF16)

            @pl.when(nx_ref[b] >= 0)
            def _():
                for c in weight_copies(nx_ref[b], 1 - s):
                    c.start(priority=1)

            wslot[0] = 1 - s

        for r in range(TB):
            gather(b + 1, r, xnxt, gnxt).start()
        if not first:
            for r in range(TB):
                scatter(b - 1, r, yprv, sprv).start(priority=r % 2)
        x_lo, x_hi = _unpack_halves(xcur[...])
        x_lo, x_hi = x_lo.astype(BF16), x_hi.astype(BF16)
        gate = _dot(x_lo, wg_s[0:PACKED, :]) + _dot(x_hi, wg_s[PACKED:D_MODEL, :])
        up = _dot(x_lo, wu_s[0:PACKED, :]) + _dot(x_hi, wu_s[PACKED:D_MODEL, :])
        hm = (jax.nn.silu(gate) * up).astype(BF16)
        ycur[...] = _pack_halves(_dot(hm, wd_s[:, 0:PACKED]), _dot(hm, wd_s[:, PACKED:D_MODEL]))

    @pl.when(b == 0)
    def _():
        ybuf0[...] = jnp.zeros((TB, PACKED), jnp.uint32)
        spare = [pltpu.make_async_copy(ybuf0, y2_ref.at[pl.ds(N_ASSIGN + s * TB, TB)], sem)
                 for s, sem in enumerate((ssem0, ssem1))]
        for c in spare:
            c.start()
        for c in spare:
            c.wait()
        start_rows(lambda r: gather(0, r, xbuf0, gsem0))
        wslot[0] = 0
        for c in weight_copies(be_ref[0], 0):
            c.start(priority=1)
        step(0, first=True)

    for par in (0, 1):
        xcur, ycur, gcur, scur = bufs[par]
        _, yprv, _, sprv = bufs[1 - par]
        mine = b % 2 == par

        @pl.when(mine & (b > 0) & (b < n_act))
        def _():
            step(par, first=False)

        @pl.when(mine & (b == n_act))
        def _():
            wait_gather(xcur, gcur)

            @pl.when(b >= 2)
            def _():
                wait_scatter(ycur, scur)

            start_rows(lambda r: scatter(b - 1, r, yprv, sprv))

        @pl.when(mine & (b == n_act + 1))
        def _():
            wait_scatter(ycur, scur)


def _experts(l, block_e, next_e, n_active, inv, xp, w_gate, w_up, w_down):
    row_buf = pltpu.VMEM((TB, PACKED), jnp.uint32)
    in_out = (D_MODEL, D_EXPERT)
    out_in = (D_EXPERT, D_MODEL)
    return pl.pallas_call(
        functools.partial(_experts_body, l),
        grid_spec=pltpu.PrefetchScalarGridSpec(
            num_scalar_prefetch=4,
            grid=(N_BLOCKS + 2,),
            in_specs=[
                pl.BlockSpec(memory_space=pl.ANY),
                pl.BlockSpec(memory_space=pl.ANY),
                pl.BlockSpec(memory_space=pl.ANY),
                pl.BlockSpec(memory_space=pl.ANY),
            ],
            out_specs=pl.BlockSpec(memory_space=pl.ANY),
            scratch_shapes=[
                row_buf, row_buf, row_buf, row_buf,
                pltpu.SemaphoreType.DMA, pltpu.SemaphoreType.DMA,
                pltpu.SemaphoreType.DMA, pltpu.SemaphoreType.DMA,
                pltpu.VMEM((2,) + in_out, F32), pltpu.VMEM((2,) + in_out, F32), pltpu.VMEM((2,) + out_in, F32),
                pltpu.SemaphoreType.DMA((2,)),
                pltpu.SMEM((1,), jnp.int32),
                pltpu.VMEM(in_out, BF16), pltpu.VMEM(in_out, BF16), pltpu.VMEM(out_in, BF16),
            ],
        ),
        out_shape=jax.ShapeDtypeStruct((N_ASSIGN + 2 * TB, PACKED), jnp.uint32),
        compiler_params=pltpu.CompilerParams(
            dimension_semantics=("arbitrary",), vmem_limit_bytes=VMEM_LIMIT),
        name="experts",
    )(block_e, next_e, n_active, inv, xp, w_gate, w_up, w_down)


def _combine_body(final, h1_ref, rw_ref, gf_ref, y_first_ref, y_second_ref, out_ref):
    w = rw_ref[...]
    w1, w2 = w[:, 0:1], w[:, 1:2]
    a_lo, a_hi = _unpack_halves(y_first_ref[...])
    b_lo, b_hi = _unpack_halves(y_second_ref[...])
    lo = h1_ref[:, 0:PACKED] + (w1 * a_lo + w2 * b_lo)
    hi = h1_ref[:, PACKED:D_MODEL] + (w1 * a_hi + w2 * b_hi)
    if final:
        ms = (jnp.sum(lo * lo, axis=-1, keepdims=True) + jnp.sum(hi * hi, axis=-1, keepdims=True)) / D_MODEL
        scale = lax.rsqrt(ms + EPS)
        lo = (lo * scale) * gf_ref[:, 0:PACKED]
        hi = (hi * scale) * gf_ref[:, PACKED:D_MODEL]
    out_ref[:, 0:PACKED] = lo
    out_ref[:, PACKED:D_MODEL] = hi


def _combine(final, h1, rw, gf, y2):
    return pl.pallas_call(
        functools.partial(_combine_body, final),
        grid=(N_TOK // TM,),
        in_specs=[
            pl.BlockSpec((TM, D_MODEL), lambda i: (i, 0)),
            pl.BlockSpec((TM, LANES), lambda i: (i, 0)),
            pl.BlockSpec((1, D_MODEL), lambda i: (0, 0)),
            pl.BlockSpec((TM, PACKED), lambda i: (i, 0)),
            pl.BlockSpec((TM, PACKED), lambda i: (N_TOK // TM + i, 0)),
        ],
        out_specs=pl.BlockSpec((TM, D_MODEL), lambda i: (i, 0)),
        out_shape=jax.ShapeDtypeStruct((N_TOK, D_MODEL), F32),
        compiler_params=pltpu.CompilerParams(
            dimension_semantics=("arbitrary",), vmem_limit_bytes=VMEM_LIMIT),
        name="combine",
    )(h1, rw, gf, y2, y2)


def _route_tables(cnt, ri):
    counts = cnt[0, N_GROUPS:N_GROUPS + N_EXPERTS].astype(jnp.int32)
    padded = (counts + TB - 1) // TB * TB
    pad_end = jnp.cumsum(padded)
    pad_start = pad_end - padded
    experts = jnp.arange(N_EXPERTS, dtype=jnp.int32)
    start_of = jnp.sum(jnp.where(ri[:, 0:TOP_K, None] == experts, pad_start, 0), axis=-1)
    dest = (start_of + ri[:, TOP_K:2 * TOP_K]).reshape(N_ASSIGN)
    n_active = (pad_end[-1:] // TB).astype(jnp.int32)
    block_row = jnp.arange(N_BLOCKS, dtype=jnp.int32)[:, None] * TB
    block_e = jnp.minimum(jnp.sum((pad_end[None, :] <= block_row).astype(jnp.int32), axis=1), N_EXPERTS - 1)
    run_end = jnp.sum(jnp.where(block_e[:, None] == experts, pad_end, 0), axis=-1) // TB
    next_e = jnp.where(run_end < n_active[0], block_e[jnp.minimum(run_end, N_BLOCKS - 1)], -1)
    return dest, block_e, next_e.astype(jnp.int32), n_active


def kernel(x, norm1_g, w_in, b_gate, conv_w, ln_v_g, ln_v_b, sgu_w, sgu_b, w_branch, w_out, norm2_g,
           router_g, router_g_b, router_e, router_e_b, w_gate, w_up, w_down, final_g):
    h = x.reshape(N_TOK, D_MODEL)

    w_in_b = w_in.astype(BF16)
    w_branch_b = w_branch.astype(BF16)
    w_out_b = w_out.astype(BF16)
    g1 = norm1_g.reshape(DEPTH, 1, D_MODEL)
    g2 = norm2_g.reshape(DEPTH, 1, D_MODEL)
    gf = final_g.reshape(1, D_MODEL)
    ln_g = ln_v_g.reshape(DEPTH, 1, WIDTH)
    ln_b = ln_v_b.reshape(DEPTH, 1, WIDTH)
    bg = b_gate.reshape(DEPTH, 1, 2 * D_MODEL)
    sgu_bias = jnp.repeat(jnp.swapaxes(sgu_b, 1, 2), GROUP_DIM, axis=2)
    pad = LANES - N_GROUPS - N_EXPERTS
    w_r = jnp.concatenate([router_g, router_e, jnp.zeros((DEPTH, D_MODEL, pad), F32)], axis=2)
    wr_hi = w_r.astype(BF16)
    w_router = jnp.concatenate([wr_hi, (w_r - wr_hi.astype(F32)).astype(BF16)], axis=2)
    r_bias = jnp.concatenate([router_g_b, router_e_b, jnp.zeros((DEPTH, pad), F32)], axis=1)
    r_bias = r_bias.reshape(DEPTH, 1, LANES)

    for l in range(DEPTH):
        ya, yb = _mixer_in(l, h, g1, w_in_b, conv_w, ln_g, ln_b, sgu_w, sgu_bias)
        h1, xp, ri, rw, cnt = _mixer_out(l, h, g1, ya, yb, w_in_b, bg, w_branch_b, w_out_b, g2,
                                         w_router, r_bias)
        dest, block_e, next_e, n_active = _route_tables(cnt, ri)
        y2 = _experts(l, block_e, next_e, n_active, _invert(dest), xp, w_gate, w_up, w_down)
        h = _combine(l == DEPTH - 1, h1, rw, gf, y2)
    return h.reshape(BATCH, SEQ, D_MODEL)
```

```python
import functools

import jax
import jax.numpy as jnp
from jax import lax
from jax.experimental import pallas as pl
from jax.experimental.pallas import tpu as pltpu

F32 = jnp.float32
BF16 = jnp.bfloat16

D_MODEL = 2048
BATCH = 4
SEQ = 4096
DEPTH = 2
N_TOK = BATCH * SEQ
WIDTH = D_MODEL // 2
CONV_K = 3
SGU_GROUPS = 8
CHUNK = 128
GROUP_DIM = WIDTH // SGU_GROUPS
D_IN = 3 * WIDTH + 2 * WIDTH + 2 * D_MODEL
N_GROUPS = 4
PER_GROUP = 8
N_EXPERTS = N_GROUPS * PER_GROUP
TOP_K = 2
N_ASSIGN = N_TOK * TOP_K
D_EXPERT = D_MODEL // 4
EPS = 1e-6

LANES = 128
SUBLANES = 8
VMEM_LIMIT = 56 * 1024 * 1024

COL_C, COL_B, COL_XA = 0, WIDTH, 2 * WIDTH
COL_U, COL_V = 3 * WIDTH, 4 * WIDTH
COL_GA, COL_GB = 5 * WIDTH, 5 * WIDTH + D_MODEL

TM = 512
TN = 512
TILES_PER_SEQ = SEQ // TM
JC = WIDTH // TN
JZ = WIDTH // TN
J_IN = JC + JZ
J1 = D_MODEL // TN
J2 = D_MODEL // TN
J_OUT = J1 + J2

PACKED = D_MODEL // 2
assert (J2 // 2) * TN == PACKED
TB = 256
P_SLOTS = N_ASSIGN + N_EXPERTS * TB
N_BLOCKS = P_SLOTS // TB
INVERT_UNROLL = 16
TC = 256
N_CTILES = N_TOK // TC
assert N_CTILES % 2 == 0


def _dot(a, b):
    return jnp.dot(a, b, preferred_element_type=F32)


def _pack_halves(lo, hi):
    lo_bits = pltpu.bitcast(lo.astype(BF16).astype(F32), jnp.uint32)
    hi_bits = pltpu.bitcast(hi.astype(BF16).astype(F32), jnp.uint32)
    return lax.shift_right_logical(lo_bits, jnp.uint32(16)) | (hi_bits & jnp.uint32(0xFFFF0000))


def _unpack_halves(words):
    lo = pltpu.bitcast(lax.shift_left(words, jnp.uint32(16)), F32)
    hi = pltpu.bitcast(words & jnp.uint32(0xFFFF0000), F32)
    return lo, hi


def _mixer_in_body(h_ref, g1_ref, wc_ref, wb_ref, wa_ref, wu_ref, wv_ref, cw_ref, lng_ref, lnb_ref,
                   sw_ref, sb_ref, ya_ref, yb_ref, xn_scr, xc_scr, carry_scr, u_scr, v_scr):
    i = pl.program_id(0)
    j = pl.program_id(1)

    @pl.when(j == 0)
    def _():
        x = h_ref[...]
        ms = jnp.mean(x * x, axis=-1, keepdims=True)
        xn_scr[...] = ((x * lax.rsqrt(ms + EPS)) * g1_ref[...]).astype(BF16)

    @pl.when(j < JC)
    def _():
        xn = xn_scr[...]
        xc = _dot(xn, wc_ref[...]) * _dot(xn, wa_ref[...])

        @pl.when(i % TILES_PER_SEQ == 0)
        def _():
            xc_scr[0:SUBLANES, :] = jnp.zeros((SUBLANES, TN), F32)

        @pl.when(i % TILES_PER_SEQ != 0)
        def _():
            xc_scr[0:SUBLANES, :] = carry_scr[j]

        xc_scr[SUBLANES:SUBLANES + TM, :] = xc
        x1 = xc_scr[SUBLANES - 1:SUBLANES - 1 + TM, :]
        x2 = xc_scr[SUBLANES - 2:SUBLANES - 2 + TM, :]
        cw = cw_ref[...]
        conv = cw[0:1, :] * x2 + cw[1:2, :] * x1 + cw[2:3, :] * xc
        carry_scr[j] = xc_scr[TM:TM + SUBLANES, :]
        ya_ref[...] = (_dot(xn, wb_ref[...]) * conv).astype(BF16)

    @pl.when(j >= JC)
    def _():
        jj = j - JC
        xn = xn_scr[...]
        u_scr[jj] = jax.nn.gelu(_dot(xn, wu_ref[...]))
        v_scr[jj] = jax.nn.gelu(_dot(xn, wv_ref[...]))

    @pl.when(j == J_IN - 1)
    def _():
        s1 = jnp.zeros((TM, 1), F32)
        for k in range(JZ):
            s1 = s1 + jnp.sum(v_scr[k], axis=-1, keepdims=True)
        mu = s1 / WIDTH
        s2 = jnp.zeros((TM, 1), F32)
        for k in range(JZ):
            d = v_scr[k] - mu
            s2 = s2 + jnp.sum(d * d, axis=-1, keepdims=True)
        rstd = lax.rsqrt(s2 / WIDTH + EPS)
        row = lax.broadcasted_iota(jnp.int32, (CHUNK, CHUNK), 0)
        col = lax.broadcasted_iota(jnp.int32, (CHUNK, CHUNK), 1)
        causal = col <= row
        gpt = TN // GROUP_DIM
        for k in range(JZ):
            vn = (((v_scr[k] - mu) * rstd) * lng_ref[:, k * TN:(k + 1) * TN]
                  + lnb_ref[:, k * TN:(k + 1) * TN]).astype(BF16)
            for gl in range(gpt):
                g = k * gpt + gl
                w = jnp.where(causal, sw_ref[g], 0.0).astype(BF16)
                bias = sb_ref[:, g * GROUP_DIM:(g + 1) * GROUP_DIM]
                for n in range(TM // CHUNK):
                    rows = slice(n * CHUNK, (n + 1) * CHUNK)
                    mixed = _dot(w, vn[rows, gl * GROUP_DIM:(gl + 1) * GROUP_DIM])
                    u = u_scr[k, rows, gl * GROUP_DIM:(gl + 1) * GROUP_DIM]
                    yb_ref[rows, g * GROUP_DIM:(g + 1) * GROUP_DIM] = (u * (mixed + bias)).astype(BF16)


def _mixer_in(l, h, g1, w_in, conv_w, ln_g, ln_b, sgu_w, sgu_bias):
    cj = lambda j: jnp.minimum(j, JC - 1)
    zj = lambda j: jnp.clip(j - JC, 0, JZ - 1)
    wspec = lambda off, f: pl.BlockSpec((None, D_MODEL, TN), lambda i, j: (l, 0, off // TN + f(j)))
    return pl.pallas_call(
        _mixer_in_body,
        grid=(N_TOK // TM, J_IN),
        in_specs=[
            pl.BlockSpec((TM, D_MODEL), lambda i, j: (i, 0)),
            pl.BlockSpec((None, 1, D_MODEL), lambda i, j: (l, 0, 0)),
            wspec(COL_C, cj), wspec(COL_B, cj), wspec(COL_XA, cj), wspec(COL_U, zj), wspec(COL_V, zj),
            pl.BlockSpec((None, CONV_K, TN), lambda i, j: (l, 0, cj(j))),
            pl.BlockSpec((None, 1, WIDTH), lambda i, j: (l, 0, 0)),
            pl.BlockSpec((None, 1, WIDTH), lambda i, j: (l, 0, 0)),
            pl.BlockSpec((None, SGU_GROUPS, CHUNK, CHUNK), lambda i, j: (l, 0, 0, 0)),
            pl.BlockSpec((None, CHUNK, WIDTH), lambda i, j: (l, 0, 0)),
        ],
        out_specs=[
            pl.BlockSpec((TM, TN), lambda i, j: (i, cj(j))),
            pl.BlockSpec((TM, WIDTH), lambda i, j: (i, 0)),
        ],
        out_shape=[jax.ShapeDtypeStruct((N_TOK, WIDTH), BF16),
                   jax.ShapeDtypeStruct((N_TOK, WIDTH), BF16)],
        scratch_shapes=[
            pltpu.VMEM((TM, D_MODEL), BF16),
            pltpu.VMEM((TM + SUBLANES, TN), F32),
            pltpu.VMEM((JC, SUBLANES, TN), F32),
            pltpu.VMEM((JZ, TM, TN), F32),
            pltpu.VMEM((JZ, TM, TN), F32),
        ],
        compiler_params=pltpu.CompilerParams(
            dimension_semantics=("arbitrary", "arbitrary"), vmem_limit_bytes=VMEM_LIMIT),
        name="mixer_in",
    )(h, g1, w_in, w_in, w_in, w_in, w_in, conv_w, ln_g, ln_b, sgu_w, sgu_bias)


def _mixer_out_body(h_ref, g1_ref, ya_ref, yb_ref, wga_ref, wgb_ref, bga_ref, bgb_ref, wba_ref, wbb_ref,
                    wo_ref, g2_ref, wr_ref, rb_ref,
                    h1_ref, xp_ref, ri_ref, rw_ref, cnt_ref,
                    xn_scr, mg_scr, out_scr, cnt_scr):
    i = pl.program_id(0)
    j = pl.program_id(1)

    @pl.when(j == 0)
    def _():
        x = h_ref[...]
        ms = jnp.mean(x * x, axis=-1, keepdims=True)
        xn_scr[...] = ((x * lax.rsqrt(ms + EPS)) * g1_ref[...]).astype(BF16)

    @pl.when((i == 0) & (j == 0))
    def _():
        cnt_scr[...] = jnp.zeros((1, LANES), F32)

    @pl.when(j < J1)
    def _():
        xn = xn_scr[...]
        ga = jax.nn.sigmoid(_dot(xn, wga_ref[...]) + bga_ref[...])
        gb = jax.nn.sigmoid(_dot(xn, wgb_ref[...]) + bgb_ref[...])
        mg_scr[j] = (ga * _dot(ya_ref[...], wba_ref[...]) + gb * _dot(yb_ref[...], wbb_ref[...])).astype(BF16)

    @pl.when(j >= J1)
    def _():
        acc = _dot(mg_scr[0], wo_ref[0:TN, :])
        for k in range(1, J1):
            acc = acc + _dot(mg_scr[k], wo_ref[k * TN:(k + 1) * TN, :])
        out_scr[j - J1] = acc

    @pl.when(j == J_OUT - 1)
    def _():
        ss = jnp.zeros((TM, 1), F32)
        for k in range(J2):
            cols = slice(k * TN, (k + 1) * TN)
            hk = h_ref[:, cols] + out_scr[k]
            h1_ref[:, cols] = hk
            ss = ss + jnp.sum(hk * hk, axis=-1, keepdims=True)
        rstd = lax.rsqrt(ss / D_MODEL + EPS)

        acc = jnp.zeros((TM, 2 * LANES), F32)
        for k in range(J2):
            cols = slice(k * TN, (k + 1) * TN)
            xk = (h1_ref[:, cols] * rstd) * g2_ref[:, cols]
            hi = xk.astype(BF16)
            lo = (xk - hi.astype(F32)).astype(BF16)
            acc = acc + (_dot(hi, wr_ref[cols, :]) + _dot(lo, wr_ref[cols, :]))
            bits = pltpu.bitcast(hi.astype(F32), jnp.uint32)
            if k < J2 // 2:
                xp_ref[:, cols] = lax.shift_right_logical(bits, jnp.uint32(16))
            else:
                pcols = slice(k * TN - PACKED, (k + 1) * TN - PACKED)
                xp_ref[:, pcols] = xp_ref[:, pcols] | (bits & jnp.uint32(0xFFFF0000))
        lg = (acc[:, 0:LANES] + acc[:, LANES:2 * LANES]) + rb_ref[...]

        lane_i = lax.broadcasted_iota(jnp.int32, (TM, LANES), 1)
        lane = lane_i.astype(F32)
        neg = -jnp.inf
        big = float(LANES)
        is_g = lane_i < N_GROUPS
        gl = jnp.where(is_g, lg, neg)
        gmax = jnp.max(gl, axis=-1, keepdims=True)
        g_idx = jnp.min(jnp.where(gl == gmax, lane, big), axis=-1, keepdims=True)
        g_w = 1.0 / jnp.sum(jnp.exp(gl - gmax), axis=-1, keepdims=True)

        first = N_GROUPS + g_idx * PER_GROUP
        in_grp = (lane >= first) & (lane < first + PER_GROUP)
        el = jnp.where(in_grp, lg, neg)
        m1 = jnp.max(el, axis=-1, keepdims=True)
        i1 = jnp.min(jnp.where(in_grp & (el == m1), lane, big), axis=-1, keepdims=True)
        el2 = jnp.where(lane == i1, neg, el)
        m2 = jnp.max(el2, axis=-1, keepdims=True)
        i2 = jnp.min(jnp.where(in_grp & (lane != i1) & (el2 == m2), lane, big), axis=-1, keepdims=True)
        t = jnp.exp(m2 - m1)
        w1 = g_w * (1.0 / (1.0 + t))
        w2 = g_w * (t / (1.0 + t))

        onehot = jnp.where((lane == i1) | (lane == i2), 1.0, 0.0)
        r_i = lax.broadcasted_iota(jnp.int32, (TM, TM), 0)
        c_i = lax.broadcasted_iota(jnp.int32, (TM, TM), 1)
        tri = jnp.where(c_i < r_i, 1.0, 0.0).astype(BF16)
        before = cnt_scr[...] + _dot(tri, onehot.astype(BF16))
        rank1 = jnp.sum(jnp.where(lane == i1, before, 0.0), axis=-1, keepdims=True)
        rank2 = jnp.sum(jnp.where(lane == i2, before, 0.0), axis=-1, keepdims=True)
        cnt_scr[...] = cnt_scr[...] + jnp.sum(onehot, axis=0, keepdims=True)

        ri = jnp.where(lane_i == 0, i1 - N_GROUPS,
                       jnp.where(lane_i == 1, i2 - N_GROUPS,
                                 jnp.where(lane_i == 2, rank1, jnp.where(lane_i == 3, rank2, 0.0))))
        ri_ref[...] = ri.astype(jnp.int32)
        rw_ref[...] = jnp.where(lane_i == 0, w1, jnp.where(lane_i == 1, w2, 0.0))
        cnt_ref[...] = jnp.broadcast_to(cnt_scr[...], (SUBLANES, LANES))


def _mixer_out(l, h, g1, ya, yb, w_in, b_gate, w_branch, w_out, g2, w_router, r_bias):
    mj = lambda j: jnp.minimum(j, J1 - 1)
    oj = lambda j: jnp.clip(j - J1, 0, J2 - 1)
    const = lambda *blk: pl.BlockSpec((None,) + blk, lambda i, j: (l,) + (0,) * len(blk))
    return pl.pallas_call(
        _mixer_out_body,
        grid=(N_TOK // TM, J_OUT),
        in_specs=[
            pl.BlockSpec((TM, D_MODEL), lambda i, j: (i, 0)),
            const(1, D_MODEL),
            pl.BlockSpec((TM, WIDTH), lambda i, j: (i, 0)),
            pl.BlockSpec((TM, WIDTH), lambda i, j: (i, 0)),
            pl.BlockSpec((None, D_MODEL, TN), lambda i, j: (l, 0, COL_GA // TN + mj(j))),
            pl.BlockSpec((None, D_MODEL, TN), lambda i, j: (l, 0, COL_GB // TN + mj(j))),
            pl.BlockSpec((None, 1, TN), lambda i, j: (l, 0, mj(j))),
            pl.BlockSpec((None, 1, TN), lambda i, j: (l, 0, D_MODEL // TN + mj(j))),
            pl.BlockSpec((None, None, WIDTH, TN), lambda i, j: (l, 0, 0, mj(j))),
            pl.BlockSpec((None, None, WIDTH, TN), lambda i, j: (l, 1, 0, mj(j))),
            pl.BlockSpec((None, D_MODEL, TN), lambda i, j: (l, 0, oj(j))),
            const(1, D_MODEL),
            const(D_MODEL, 2 * LANES), const(1, LANES),
        ],
        out_specs=[
            pl.BlockSpec((TM, D_MODEL), lambda i, j: (i, 0)),
            pl.BlockSpec((TM, PACKED), lambda i, j: (i, 0)),
            pl.BlockSpec((TM, LANES), lambda i, j: (i, 0)),
            pl.BlockSpec((TM, LANES), lambda i, j: (i, 0)),
            pl.BlockSpec((SUBLANES, LANES), lambda i, j: (0, 0)),
        ],
        out_shape=[jax.ShapeDtypeStruct((N_TOK, D_MODEL), F32),
                   jax.ShapeDtypeStruct((N_TOK, PACKED), jnp.uint32),
                   jax.ShapeDtypeStruct((N_TOK, LANES), jnp.int32),
                   jax.ShapeDtypeStruct((N_TOK, LANES), F32),
                   jax.ShapeDtypeStruct((SUBLANES, LANES), F32)],
        scratch_shapes=[
            pltpu.VMEM((TM, D_MODEL), BF16),
            pltpu.VMEM((J1, TM, TN), BF16),
            pltpu.VMEM((J2, TM, TN), F32),
            pltpu.VMEM((1, LANES), F32),
        ],
        compiler_params=pltpu.CompilerParams(
            dimension_semantics=("arbitrary", "arbitrary"), vmem_limit_bytes=VMEM_LIMIT),
        name="mixer_out",
    )(h, g1, ya, yb, w_in, w_in, b_gate, b_gate, w_branch, w_branch, w_out, g2, w_router, r_bias)


def _invert_body(dest_ref, spare_ref, inv_ref, sem):
    fill = pltpu.make_async_copy(spare_ref, inv_ref, sem)
    fill.start()
    fill.wait()

    def place(c, carry):
        t0 = c * INVERT_UNROLL
        for u in range(INVERT_UNROLL):
            for k in range(TOP_K):
                inv_ref[dest_ref[(t0 + u) * TOP_K + k]] = t0 + u
        return carry

    lax.fori_loop(0, N_TOK // INVERT_UNROLL, place, 0)


def _invert(dest):
    spare = jnp.arange(P_SLOTS, dtype=jnp.int32) % N_TOK
    return pl.pallas_call(
        _invert_body,
        in_specs=[pl.BlockSpec(memory_space=pltpu.SMEM), pl.BlockSpec(memory_space=pl.ANY)],
        out_specs=pl.BlockSpec(memory_space=pltpu.SMEM),
        out_shape=jax.ShapeDtypeStruct((P_SLOTS,), jnp.int32),
        scratch_shapes=[pltpu.SemaphoreType.DMA],
        name="invert",
    )(dest, spare)


def _experts_body(l, be_ref, nx_ref, na_ref, inv_ref, xp_ref, wg_ref, wu_ref, wd_ref, y_ref,
                  xbuf0, xbuf1, gsem0, gsem1, wg_f, wu_f, wd_f, wsem, wslot, wg_s, wu_s, wd_s):
    b = pl.program_id(0)
    n_act = na_ref[0]
    bufs = ((xbuf0, gsem0), (xbuf1, gsem1))

    def weight_copies(e, s):
        return [pltpu.make_async_copy(src.at[l, e], dst.at[s], wsem.at[s])
                for src, dst in ((wg_ref, wg_f), (wu_ref, wu_f), (wd_ref, wd_f))]

    def gather(blk, r, xdst, sem):
        return pltpu.make_async_copy(
            xp_ref.at[pl.ds(inv_ref[blk * TB + r], 1)], xdst.at[pl.ds(r, 1)], sem)

    def wait_gather(xdst, sem):
        pltpu.make_async_copy(xp_ref.at[pl.ds(0, TB)], xdst, sem).wait()

    def step(par):
        xcur, gcur = bufs[par]
        xnxt, gnxt = bufs[1 - par]
        wait_gather(xcur, gcur)

        @pl.when((b == 0) | (be_ref[b] != be_ref[jnp.maximum(b - 1, 0)]))
        def _():
            s = wslot[0]
            for c in weight_copies(be_ref[b], s):
                c.wait()
            wg_s[...] = wg_f[s].astype(BF16)
            wu_s[...] = wu_f[s].astype(BF16)
            wd_s[...] = wd_f[s].astype(BF16)

            @pl.when(nx_ref[b] >= 0)
            def _():
                for c in weight_copies(nx_ref[b], 1 - s):
                    c.start(priority=1)

            wslot[0] = 1 - s

        for r in range(TB):
            gather(b + 1, r, xnxt, gnxt).start()
        x_lo, x_hi = _unpack_halves(xcur[...])
        x_lo, x_hi = x_lo.astype(BF16), x_hi.astype(BF16)
        gate = _dot(x_lo, wg_s[0:PACKED, :]) + _dot(x_hi, wg_s[PACKED:D_MODEL, :])
        up = _dot(x_lo, wu_s[0:PACKED, :]) + _dot(x_hi, wu_s[PACKED:D_MODEL, :])
        hm = (jax.nn.silu(gate) * up).astype(BF16)
        y_ref[...] = _pack_halves(_dot(hm, wd_s[:, 0:PACKED]), _dot(hm, wd_s[:, PACKED:D_MODEL]))

    @pl.when(b == 0)
    def _():
        def first_rows(r, carry):
            gather(0, r, xbuf0, gsem0).start()
            return carry
        lax.fori_loop(0, TB, first_rows, 0)
        wslot[0] = 0
        for c in weight_copies(be_ref[0], 0):
            c.start(priority=1)

    for par in (0, 1):
        @pl.when((b % 2 == par) & (b < n_act))
        def _():
            step(par)

        @pl.when((b % 2 == par) & (b == n_act))
        def _():
            wait_gather(*bufs[par])

    @pl.when(b >= n_act)
    def _():
        y_ref[...] = jnp.zeros((TB, PACKED), jnp.uint32)


def _experts(l, block_e, next_e, n_active, inv, xp, w_gate, w_up, w_down):
    row_buf = pltpu.VMEM((TB, PACKED), jnp.uint32)
    in_out = (D_MODEL, D_EXPERT)
    out_in = (D_EXPERT, D_MODEL)
    return pl.pallas_call(
        functools.partial(_experts_body, l),
        grid_spec=pltpu.PrefetchScalarGridSpec(
            num_scalar_prefetch=4,
            grid=(N_BLOCKS,),
            in_specs=[
                pl.BlockSpec(memory_space=pl.ANY),
                pl.BlockSpec(memory_space=pl.ANY),
                pl.BlockSpec(memory_space=pl.ANY),
                pl.BlockSpec(memory_space=pl.ANY),
            ],
            out_specs=pl.BlockSpec((TB, PACKED), lambda b, be, nx, na, iv: (b, 0)),
            scratch_shapes=[
                row_buf, row_buf,
                pltpu.SemaphoreType.DMA, pltpu.SemaphoreType.DMA,
                pltpu.VMEM((2,) + in_out, F32), pltpu.VMEM((2,) + in_out, F32), pltpu.VMEM((2,) + out_in, F32),
                pltpu.SemaphoreType.DMA((2,)),
                pltpu.SMEM((1,), jnp.int32),
                pltpu.VMEM(in_out, BF16), pltpu.VMEM(in_out, BF16), pltpu.VMEM(out_in, BF16),
            ],
        ),
        out_shape=jax.ShapeDtypeStruct((P_SLOTS, PACKED), jnp.uint32),
        compiler_params=pltpu.CompilerParams(
            dimension_semantics=("arbitrary",), vmem_limit_bytes=VMEM_LIMIT),
        name="experts",
    )(block_e, next_e, n_active, inv, xp, w_gate, w_up, w_down)


def _combine_body(final, dest_ref, h1_ref, rw_ref, gf_ref, y_ref, out_ref,
                  ya0, yb0, ya1, yb1, sem0, sem1):
    i = pl.program_id(0)
    bufs = ((ya0, yb0, sem0), (ya1, yb1, sem1))

    def fetch(tile, r, k, dst, sem):
        slot = dest_ref[(tile * TC + r) * TOP_K + k]
        return pltpu.make_async_copy(y_ref.at[pl.ds(slot, 1)], dst.at[pl.ds(r, 1)], sem)

    def step(par, last):
        ya, yb, sem = bufs[par]
        for dst in (ya, yb):
            pltpu.make_async_copy(y_ref.at[pl.ds(0, TC)], dst, sem).wait()
        if not last:
            na, nb, nsem = bufs[1 - par]
            for r in range(TC):
                fetch(i + 1, r, 0, na, nsem).start()
                fetch(i + 1, r, 1, nb, nsem).start(priority=1)
        w = rw_ref[...]
        w1, w2 = w[:, 0:1], w[:, 1:2]
        a_lo, a_hi = _unpack_halves(ya[...])
        b_lo, b_hi = _unpack_halves(yb[...])
        lo = h1_ref[:, 0:PACKED] + (w1 * a_lo + w2 * b_lo)
        hi = h1_ref[:, PACKED:D_MODEL] + (w1 * a_hi + w2 * b_hi)
        if final:
            ms = (jnp.sum(lo * lo, axis=-1, keepdims=True)
                  + jnp.sum(hi * hi, axis=-1, keepdims=True)) / D_MODEL
            scale = lax.rsqrt(ms + EPS)
            lo = (lo * scale) * gf_ref[:, 0:PACKED]
            hi = (hi * scale) * gf_ref[:, PACKED:D_MODEL]
        out_ref[:, 0:PACKED] = lo
        out_ref[:, PACKED:D_MODEL] = hi

    @pl.when(i == 0)
    def _():
        def first_rows(r, carry):
            fetch(0, r, 0, ya0, sem0).start()
            fetch(0, r, 1, yb0, sem0).start()
            return carry
        lax.fori_loop(0, TC, first_rows, 0)

    @pl.when(i % 2 == 0)
    def _():
        step(0, last=False)

    @pl.when((i % 2 == 1) & (i < N_CTILES - 1))
    def _():
        step(1, last=False)

    @pl.when(i == N_CTILES - 1)
    def _():
        step(1, last=True)


def _combine(final, dest, h1, rw, gf, y):
    row_buf = pltpu.VMEM((TC, PACKED), jnp.uint32)
    return pl.pallas_call(
        functools.partial(_combine_body, final),
        grid_spec=pltpu.PrefetchScalarGridSpec(
            num_scalar_prefetch=1,
            grid=(N_CTILES,),
            in_specs=[
                pl.BlockSpec((TC, D_MODEL), lambda i, d: (i, 0)),
                pl.BlockSpec((TC, LANES), lambda i, d: (i, 0)),
                pl.BlockSpec((1, D_MODEL), lambda i, d: (0, 0)),
                pl.BlockSpec(memory_space=pl.ANY),
            ],
            out_specs=pl.BlockSpec((TC, D_MODEL), lambda i, d: (i, 0)),
            scratch_shapes=[row_buf, row_buf, row_buf, row_buf,
                            pltpu.SemaphoreType.DMA, pltpu.SemaphoreType.DMA],
        ),
        out_shape=jax.ShapeDtypeStruct((N_TOK, D_MODEL), F32),
        compiler_params=pltpu.CompilerParams(
            dimension_semantics=("arbitrary",), vmem_limit_bytes=VMEM_LIMIT),
        name="combine",
    )(dest, h1, rw, gf, y)


def _route_tables(cnt, ri):
    counts = cnt[0, N_GROUPS:N_GROUPS + N_EXPERTS].astype(jnp.int32)
    padded = (counts + TB - 1) // TB * TB
    pad_end = jnp.cumsum(padded)
    pad_start = pad_end - padded
    experts = jnp.arange(N_EXPERTS, dtype=jnp.int32)
    start_of = jnp.sum(jnp.where(ri[:, 0:TOP_K, None] == experts, pad_start, 0), axis=-1)
    dest = (start_of + ri[:, TOP_K:2 * TOP_K]).reshape(N_ASSIGN)
    n_active = (pad_end[-1:] // TB).astype(jnp.int32)
    block_row = jnp.arange(N_BLOCKS, dtype=jnp.int32)[:, None] * TB
    block_e = jnp.minimum(jnp.sum((pad_end[None, :] <= block_row).astype(jnp.int32), axis=1), N_EXPERTS - 1)
    run_end = jnp.sum(jnp.where(block_e[:, None] == experts, pad_end, 0), axis=-1) // TB
    next_e = jnp.where(run_end < n_active[0], block_e[jnp.minimum(run_end, N_BLOCKS - 1)], -1)
    return dest, block_e, next_e.astype(jnp.int32), n_active


def kernel(x, norm1_g, w_in, b_gate, conv_w, ln_v_g, ln_v_b, sgu_w, sgu_b, w_branch, w_out, norm2_g,
           router_g, router_g_b, router_e, router_e_b, w_gate, w_up, w_down, final_g):
    h = x.reshape(N_TOK, D_MODEL)

    w_in_b = w_in.astype(BF16)
    w_branch_b = w_branch.astype(BF16)
    w_out_b = w_out.astype(BF16)
    g1 = norm1_g.reshape(DEPTH, 1, D_MODEL)
    g2 = norm2_g.reshape(DEPTH, 1, D_MODEL)
    gf = final_g.reshape(1, D_MODEL)
    ln_g = ln_v_g.reshape(DEPTH, 1, WIDTH)
    ln_b = ln_v_b.reshape(DEPTH, 1, WIDTH)
    bg = b_gate.reshape(DEPTH, 1, 2 * D_MODEL)
    sgu_bias = jnp.repeat(jnp.swapaxes(sgu_b, 1, 2), GROUP_DIM, axis=2)
    pad = LANES - N_GROUPS - N_EXPERTS
    w_r = jnp.concatenate([router_g, router_e, jnp.zeros((DEPTH, D_MODEL, pad), F32)], axis=2)
    wr_hi = w_r.astype(BF16)
    w_router = jnp.concatenate([wr_hi, (w_r - wr_hi.astype(F32)).astype(BF16)], axis=2)
    r_bias = jnp.concatenate([router_g_b, router_e_b, jnp.zeros((DEPTH, pad), F32)], axis=1)
    r_bias = r_bias.reshape(DEPTH, 1, LANES)

    for l in range(DEPTH):
        ya, yb = _mixer_in(l, h, g1, w_in_b, conv_w, ln_g, ln_b, sgu_w, sgu_bias)
        h1, xp, ri, rw, cnt = _mixer_out(l, h, g1, ya, yb, w_in_b, bg, w_branch_b, w_out_b, g2,
                                         w_router, r_bias)
        dest, block_e, next_e, n_active = _route_tables(cnt, ri)
        y = _experts(l, block_e, next_e, n_active, _invert(dest), xp, w_gate, w_up, w_down)
        h = _combine(l == DEPTH - 1, dest, h1, rw, gf, y)
    return h.reshape(BATCH, SEQ, D_MODEL)
```

```python
import functools

import jax
import jax.numpy as jnp
from jax import lax
from jax.experimental import pallas as pl
from jax.experimental.pallas import tpu as pltpu

F32 = jnp.float32
BF16 = jnp.bfloat16

D_MODEL = 2048
BATCH = 4
SEQ = 4096
DEPTH = 2
N_TOK = BATCH * SEQ
WIDTH = D_MODEL // 2
CONV_K = 3
SGU_GROUPS = 8
CHUNK = 128
GROUP_DIM = WIDTH // SGU_GROUPS
D_IN = 3 * WIDTH + 2 * WIDTH + 2 * D_MODEL
N_GROUPS = 4
PER_GROUP = 8
N_EXPERTS = N_GROUPS * PER_GROUP
TOP_K = 2
N_ASSIGN = N_TOK * TOP_K
D_EXPERT = D_MODEL // 4
EPS = 1e-6

LANES = 128
SUBLANES = 8
VMEM_LIMIT = 56 * 1024 * 1024

COL_C, COL_B, COL_XA = 0, WIDTH, 2 * WIDTH
COL_U, COL_V = 3 * WIDTH, 4 * WIDTH
COL_GA, COL_GB = 5 * WIDTH, 5 * WIDTH + D_MODEL

TM = 512
TN = 512
TILES_PER_SEQ = SEQ // TM
J1 = D_MODEL // TN
J2 = D_MODEL // TN
J_OUT = J1 + J2

PACKED = D_MODEL // 2
assert (J2 // 2) * TN == PACKED
TB = 256
P_SLOTS = N_ASSIGN + N_EXPERTS * TB
N_BLOCKS = P_SLOTS // TB
INVERT_UNROLL = 16
TC = 256
N_CTILES = N_TOK // TC
assert N_CTILES % 2 == 0


def _dot(a, b):
    return jnp.dot(a, b, preferred_element_type=F32)


def _pack_halves(lo, hi):
    lo_bits = pltpu.bitcast(lo.astype(BF16).astype(F32), jnp.uint32)
    hi_bits = pltpu.bitcast(hi.astype(BF16).astype(F32), jnp.uint32)
    return lax.shift_right_logical(lo_bits, jnp.uint32(16)) | (hi_bits & jnp.uint32(0xFFFF0000))


def _unpack_halves(words):
    lo = pltpu.bitcast(lax.shift_left(words, jnp.uint32(16)), F32)
    hi = pltpu.bitcast(words & jnp.uint32(0xFFFF0000), F32)
    return lo, hi


def _mixer_in_body(tn, fused, *refs):
    if fused:
        (dest_ref, h1_ref, rw_ref, y_ref, g1_ref, wc_ref, wb_ref, wa_ref, wu_ref, wv_ref, cw_ref, lng_ref,
         lnb_ref, sw_ref, sb_ref, ya_ref, yb_ref, h_ref, xn_scr, xc_scr, carry_scr, u_scr, v_scr,
         fa, fb, fsem) = refs
    else:
        (h_ref, g1_ref, wc_ref, wb_ref, wa_ref, wu_ref, wv_ref, cw_ref, lng_ref, lnb_ref, sw_ref, sb_ref,
         ya_ref, yb_ref, xn_scr, xc_scr, carry_scr, u_scr, v_scr) = refs
    jc = WIDTH // tn
    jz = WIDTH // tn
    n_tiles = N_TOK // TM
    rows_per_step = TM // (jc + jz)
    i = pl.program_id(0)
    j = pl.program_id(1)

    def fetch(tile, r, k):
        src = y_ref.at[pl.ds(dest_ref[(tile * TM + r) * TOP_K + k], 1)]
        return pltpu.make_async_copy(src, (fa, fb)[k].at[pl.ds(r, 1)], fsem)

    def wait_fetch():
        for dst in (fa, fb):
            pltpu.make_async_copy(y_ref.at[pl.ds(0, TM)], dst, fsem).wait()

    def issue_fetch(step):
        if fused:
            nxt = jnp.where(i + 1 == n_tiles, 0, i + 1)
            for r in range(step * rows_per_step, (step + 1) * rows_per_step):
                fetch(nxt, r, 0).start()
                fetch(nxt, r, 1).start(priority=1)

    if fused:
        @pl.when((i == 0) & (j == 0))
        def _():
            def first_rows(r, carry):
                fetch(0, r, 0).start()
                fetch(0, r, 1).start()
                return carry
            lax.fori_loop(0, TM, first_rows, 0)

    @pl.when(j == 0)
    def _():
        if fused:
            wait_fetch()
            w = rw_ref[...]
            w1, w2 = w[:, 0:1], w[:, 1:2]
            a_lo, a_hi = _unpack_halves(fa[...])
            b_lo, b_hi = _unpack_halves(fb[...])
            halves = (h1_ref[:, 0:PACKED] + (w1 * a_lo + w2 * b_lo),
                      h1_ref[:, PACKED:D_MODEL] + (w1 * a_hi + w2 * b_hi))
            ms = sum(jnp.sum(v * v, axis=-1, keepdims=True) for v in halves) / D_MODEL
            scale = lax.rsqrt(ms + EPS)
            for c, v in enumerate(halves):
                cols = slice(c * PACKED, (c + 1) * PACKED)
                h_ref[:, cols] = v
                xn_scr[:, cols] = ((v * scale) * g1_ref[:, cols]).astype(BF16)
        else:
            x = h_ref[...]
            ms = jnp.mean(x * x, axis=-1, keepdims=True)
            xn_scr[...] = ((x * lax.rsqrt(ms + EPS)) * g1_ref[...]).astype(BF16)

    def conv_step(jv):
        issue_fetch(jv)
        xn = xn_scr[...]
        xc = _dot(xn, wc_ref[...]) * _dot(xn, wa_ref[...])

        @pl.when(i % TILES_PER_SEQ == 0)
        def _():
            xc_scr[0:SUBLANES, :] = jnp.zeros((SUBLANES, tn), F32)

        @pl.when(i % TILES_PER_SEQ != 0)
        def _():
            xc_scr[0:SUBLANES, :] = carry_scr[jv]

        xc_scr[SUBLANES:SUBLANES + TM, :] = xc
        x1 = xc_scr[SUBLANES - 1:SUBLANES - 1 + TM, :]
        x2 = xc_scr[SUBLANES - 2:SUBLANES - 2 + TM, :]
        cw = cw_ref[...]
        conv = cw[0:1, :] * x2 + cw[1:2, :] * x1 + cw[2:3, :] * xc
        carry_scr[jv] = xc_scr[TM:TM + SUBLANES, :]
        ya_ref[...] = (_dot(xn, wb_ref[...]) * conv).astype(BF16)

    def gating_step(jv):
        issue_fetch(jc + jv)
        xn = xn_scr[...]
        u_scr[jv] = jax.nn.gelu(_dot(xn, wu_ref[...]))
        v_scr[jv] = jax.nn.gelu(_dot(xn, wv_ref[...]))

    for jv in range(jc):
        pl.when(j == jv)(functools.partial(conv_step, jv))
    for jv in range(jz):
        pl.when(j == jc + jv)(functools.partial(gating_step, jv))

    @pl.when(j == jc + jz - 1)
    def _():
        s1 = jnp.zeros((TM, 1), F32)
        for k in range(jz):
            s1 = s1 + jnp.sum(v_scr[k], axis=-1, keepdims=True)
        mu = s1 / WIDTH
        s2 = jnp.zeros((TM, 1), F32)
        for k in range(jz):
            d = v_scr[k] - mu
            s2 = s2 + jnp.sum(d * d, axis=-1, keepdims=True)
        rstd = lax.rsqrt(s2 / WIDTH + EPS)
        row = lax.broadcasted_iota(jnp.int32, (CHUNK, CHUNK), 0)
        col = lax.broadcasted_iota(jnp.int32, (CHUNK, CHUNK), 1)
        causal = col <= row
        gpt = tn // GROUP_DIM
        for k in range(jz):
            vn = (((v_scr[k] - mu) * rstd) * lng_ref[:, k * tn:(k + 1) * tn]
                  + lnb_ref[:, k * tn:(k + 1) * tn]).astype(BF16)
            for gl in range(gpt):
                g = k * gpt + gl
                w = jnp.where(causal, sw_ref[g], 0.0).astype(BF16)
                bias = sb_ref[:, g * GROUP_DIM:(g + 1) * GROUP_DIM]
                for n in range(TM // CHUNK):
                    rows = slice(n * CHUNK, (n + 1) * CHUNK)
                    mixed = _dot(w, vn[rows, gl * GROUP_DIM:(gl + 1) * GROUP_DIM])
                    u = u_scr[k, rows, gl * GROUP_DIM:(gl + 1) * GROUP_DIM]
                    yb_ref[rows, g * GROUP_DIM:(g + 1) * GROUP_DIM] = (u * (mixed + bias)).astype(BF16)

    if fused:
        @pl.when((i == n_tiles - 1) & (j == jc + jz - 1))
        def _():
            wait_fetch()


def _mixer_in(l, tn, h, g1, w_in, conv_w, ln_g, ln_b, sgu_w, sgu_bias, moe=None):
    fused = moe is not None
    jc = jz = WIDTH // tn
    cj = lambda j: jnp.minimum(j, jc - 1)
    zj = lambda j: jnp.clip(j - jc, 0, jz - 1)
    wspec = lambda off, f: pl.BlockSpec((None, D_MODEL, tn), lambda i, j, *_: (l, 0, off // tn + f(j)))
    row_tile = pl.BlockSpec((TM, D_MODEL), lambda i, j, *_: (i, 0))
    in_specs = [
        pl.BlockSpec((None, 1, D_MODEL), lambda i, j, *_: (l, 0, 0)),
        wspec(COL_C, cj), wspec(COL_B, cj), wspec(COL_XA, cj), wspec(COL_U, zj), wspec(COL_V, zj),
        pl.BlockSpec((None, CONV_K, tn), lambda i, j, *_: (l, 0, cj(j))),
        pl.BlockSpec((None, 1, WIDTH), lambda i, j, *_: (l, 0, 0)),
        pl.BlockSpec((None, 1, WIDTH), lambda i, j, *_: (l, 0, 0)),
        pl.BlockSpec((None, SGU_GROUPS, CHUNK, CHUNK), lambda i, j, *_: (l, 0, 0, 0)),
        pl.BlockSpec((None, CHUNK, WIDTH), lambda i, j, *_: (l, 0, 0)),
    ]
    out_specs = [
        pl.BlockSpec((TM, tn), lambda i, j, *_: (i, cj(j))),
        pl.BlockSpec((TM, WIDTH), lambda i, j, *_: (i, 0)),
    ]
    out_shape = [jax.ShapeDtypeStruct((N_TOK, WIDTH), BF16), jax.ShapeDtypeStruct((N_TOK, WIDTH), BF16)]
    scratch = [
        pltpu.VMEM((TM, D_MODEL), BF16),
        pltpu.VMEM((TM + SUBLANES, tn), F32),
        pltpu.VMEM((jc, SUBLANES, tn), F32),
        pltpu.VMEM((jz, TM, tn), F32),
        pltpu.VMEM((jz, TM, tn), F32),
    ]
    weights = (g1, w_in, w_in, w_in, w_in, w_in, conv_w, ln_g, ln_b, sgu_w, sgu_bias)
    if fused:
        dest, h1, rw, y = moe
        in_specs = [row_tile, pl.BlockSpec((TM, LANES), lambda i, j, *_: (i, 0)),
                    pl.BlockSpec(memory_space=pl.ANY)] + in_specs
        out_specs.append(row_tile)
        out_shape.append(jax.ShapeDtypeStruct((N_TOK, D_MODEL), F32))
        scratch += [pltpu.VMEM((TM, PACKED), jnp.uint32), pltpu.VMEM((TM, PACKED), jnp.uint32),
                    pltpu.SemaphoreType.DMA]
        operands = (dest, h1, rw, y) + weights
    else:
        in_specs = [row_tile] + in_specs
        operands = (h,) + weights
    return pl.pallas_call(
        functools.partial(_mixer_in_body, tn, fused),
        grid_spec=pltpu.PrefetchScalarGridSpec(
            num_scalar_prefetch=1 if fused else 0,
            grid=(N_TOK // TM, jc + jz),
            in_specs=in_specs, out_specs=out_specs, scratch_shapes=scratch),
        out_shape=out_shape,
        compiler_params=pltpu.CompilerParams(
            dimension_semantics=("arbitrary", "arbitrary"), vmem_limit_bytes=VMEM_LIMIT),
        name="mixer_in",
    )(*operands)


def _mixer_out_body(h_ref, g1_ref, ya_ref, yb_ref, wga_ref, wgb_ref, bga_ref, bgb_ref, wba_ref, wbb_ref,
                    wo_ref, g2_ref, wr_ref, rb_ref,
                    h1_ref, xp_ref, ri_ref, rw_ref, cnt_ref,
                    xn_scr, mg_scr, out_scr, cnt_scr):
    i = pl.program_id(0)
    j = pl.program_id(1)

    @pl.when(j == 0)
    def _():
        x = h_ref[...]
        ms = jnp.mean(x * x, axis=-1, keepdims=True)
        xn_scr[...] = ((x * lax.rsqrt(ms + EPS)) * g1_ref[...]).astype(BF16)

    @pl.when((i == 0) & (j == 0))
    def _():
        cnt_scr[...] = jnp.zeros((1, LANES), F32)

    @pl.when(j < J1)
    def _():
        xn = xn_scr[...]
        ga = jax.nn.sigmoid(_dot(xn, wga_ref[...]) + bga_ref[...])
        gb = jax.nn.sigmoid(_dot(xn, wgb_ref[...]) + bgb_ref[...])
        mg_scr[j] = (ga * _dot(ya_ref[...], wba_ref[...]) + gb * _dot(yb_ref[...], wbb_ref[...])).astype(BF16)

    @pl.when(j >= J1)
    def _():
        acc = _dot(mg_scr[0], wo_ref[0:TN, :])
        for k in range(1, J1):
            acc = acc + _dot(mg_scr[k], wo_ref[k * TN:(k + 1) * TN, :])
        out_scr[j - J1] = acc

    @pl.when(j == J_OUT - 1)
    def _():
        ss = jnp.zeros((TM, 1), F32)
        for k in range(J2):
            cols = slice(k * TN, (k + 1) * TN)
            hk = h_ref[:, cols] + out_scr[k]
            h1_ref[:, cols] = hk
            ss = ss + jnp.sum(hk * hk, axis=-1, keepdims=True)
        rstd = lax.rsqrt(ss / D_MODEL + EPS)

        acc = jnp.zeros((TM, 2 * LANES), F32)
        for k in range(J2):
            cols = slice(k * TN, (k + 1) * TN)
            xk = (h1_ref[:, cols] * rstd) * g2_ref[:, cols]
            hi = xk.astype(BF16)
            lo = (xk - hi.astype(F32)).astype(BF16)
            acc = acc + (_dot(hi, wr_ref[cols, :]) + _dot(lo, wr_ref[cols, :]))
            bits = pltpu.bitcast(hi.astype(F32), jnp.uint32)
            if k < J2 // 2:
                xp_ref[:, cols] = lax.shift_right_logical(bits, jnp.uint32(16))
            else:
                pcols = slice(k * TN - PACKED, (k + 1) * TN - PACKED)
                xp_ref[:, pcols] = xp_ref[:, pcols] | (bits & jnp.uint32(0xFFFF0000))
        lg = (acc[:, 0:LANES] + acc[:, LANES:2 * LANES]) + rb_ref[...]

        lane_i = lax.broadcasted_iota(jnp.int32, (TM, LANES), 1)
        lane = lane_i.astype(F32)
        neg = -jnp.inf
        big = float(LANES)
        is_g = lane_i < N_GROUPS
        gl = jnp.where(is_g, lg, neg)
        gmax = jnp.max(gl, axis=-1, keepdims=True)
        g_idx = jnp.min(jnp.where(gl == gmax, lane, big), axis=-1, keepdims=True)
        g_w = 1.0 / jnp.sum(jnp.exp(gl - gmax), axis=-1, keepdims=True)

        first = N_GROUPS + g_idx * PER_GROUP
        in_grp = (lane >= first) & (lane < first + PER_GROUP)
        el = jnp.where(in_grp, lg, neg)
        m1 = jnp.max(el, axis=-1, keepdims=True)
        i1 = jnp.min(jnp.where(in_grp & (el == m1), lane, big), axis=-1, keepdims=True)
        el2 = jnp.where(lane == i1, neg, el)
        m2 = jnp.max(el2, axis=-1, keepdims=True)
        i2 = jnp.min(jnp.where(in_grp & (lane != i1) & (el2 == m2), lane, big), axis=-1, keepdims=True)
        t = jnp.exp(m2 - m1)
        w1 = g_w * (1.0 / (1.0 + t))
        w2 = g_w * (t / (1.0 + t))

        onehot = jnp.where((lane == i1) | (lane == i2), 1.0, 0.0)
        r_i = lax.broadcasted_iota(jnp.int32, (TM, TM), 0)
        c_i = lax.broadcasted_iota(jnp.int32, (TM, TM), 1)
        tri = jnp.where(c_i < r_i, 1.0, 0.0).astype(BF16)
        before = cnt_scr[...] + _dot(tri, onehot.astype(BF16))
        rank1 = jnp.sum(jnp.where(lane == i1, before, 0.0), axis=-1, keepdims=True)
        rank2 = jnp.sum(jnp.where(lane == i2, before, 0.0), axis=-1, keepdims=True)
        cnt_scr[...] = cnt_scr[...] + jnp.sum(onehot, axis=0, keepdims=True)

        ri = jnp.where(lane_i == 0, i1 - N_GROUPS,
                       jnp.where(lane_i == 1, i2 - N_GROUPS,
                                 jnp.where(lane_i == 2, rank1, jnp.where(lane_i == 3, rank2, 0.0))))
        ri_ref[...] = ri.astype(jnp.int32)
        rw_ref[...] = jnp.where(lane_i == 0, w1, jnp.where(lane_i == 1, w2, 0.0))
        cnt_ref[...] = jnp.broadcast_to(cnt_scr[...], (SUBLANES, LANES))


def _mixer_out(l, h, g1, ya, yb, w_in, b_gate, w_branch, w_out, g2, w_router, r_bias):
    mj = lambda j: jnp.minimum(j, J1 - 1)
    oj = lambda j: jnp.clip(j - J1, 0, J2 - 1)
    const = lambda *blk: pl.BlockSpec((None,) + blk, lambda i, j: (l,) + (0,) * len(blk))
    return pl.pallas_call(
        _mixer_out_body,
        grid=(N_TOK // TM, J_OUT),
        in_specs=[
            pl.BlockSpec((TM, D_MODEL), lambda i, j: (i, 0)),
            const(1, D_MODEL),
            pl.BlockSpec((TM, WIDTH), lambda i, j: (i, 0)),
            pl.BlockSpec((TM, WIDTH), lambda i, j: (i, 0)),
            pl.BlockSpec((None, D_MODEL, TN), lambda i, j: (l, 0, COL_GA // TN + mj(j))),
            pl.BlockSpec((None, D_MODEL, TN), lambda i, j: (l, 0, COL_GB // TN + mj(j))),
            pl.BlockSpec((None, 1, TN), lambda i, j: (l, 0, mj(j))),
            pl.BlockSpec((None, 1, TN), lambda i, j: (l, 0, D_MODEL // TN + mj(j))),
            pl.BlockSpec((None, None, WIDTH, TN), lambda i, j: (l, 0, 0, mj(j))),
            pl.BlockSpec((None, None, WIDTH, TN), lambda i, j: (l, 1, 0, mj(j))),
            pl.BlockSpec((None, D_MODEL, TN), lambda i, j: (l, 0, oj(j))),
            const(1, D_MODEL),
            const(D_MODEL, 2 * LANES), const(1, LANES),
        ],
        out_specs=[
            pl.BlockSpec((TM, D_MODEL), lambda i, j: (i, 0)),
            pl.BlockSpec((TM, PACKED), lambda i, j: (i, 0)),
            pl.BlockSpec((TM, LANES), lambda i, j: (i, 0)),
            pl.BlockSpec((TM, LANES), lambda i, j: (i, 0)),
            pl.BlockSpec((SUBLANES, LANES), lambda i, j: (0, 0)),
        ],
        out_shape=[jax.ShapeDtypeStruct((N_TOK, D_MODEL), F32),
                   jax.ShapeDtypeStruct((N_TOK, PACKED), jnp.uint32),
                   jax.ShapeDtypeStruct((N_TOK, LANES), jnp.int32),
                   jax.ShapeDtypeStruct((N_TOK, LANES), F32),
                   jax.ShapeDtypeStruct((SUBLANES, LANES), F32)],
        scratch_shapes=[
            pltpu.VMEM((TM, D_MODEL), BF16),
            pltpu.VMEM((J1, TM, TN), BF16),
            pltpu.VMEM((J2, TM, TN), F32),
            pltpu.VMEM((1, LANES), F32),
        ],
        compiler_params=pltpu.CompilerParams(
            dimension_semantics=("arbitrary", "arbitrary"), vmem_limit_bytes=VMEM_LIMIT),
        name="mixer_out",
    )(h, g1, ya, yb, w_in, w_in, b_gate, b_gate, w_branch, w_branch, w_out, g2, w_router, r_bias)


def _invert_body(dest_ref, spare_ref, inv_ref, sem):
    fill = pltpu.make_async_copy(spare_ref, inv_ref, sem)
    fill.start()
    fill.wait()

    def place(c, carry):
        t0 = c * INVERT_UNROLL
        for u in range(INVERT_UNROLL):
            for k in range(TOP_K):
                inv_ref[dest_ref[(t0 + u) * TOP_K + k]] = t0 + u
        return carry

    lax.fori_loop(0, N_TOK // INVERT_UNROLL, place, 0)


def _invert(dest):
    spare = jnp.arange(P_SLOTS, dtype=jnp.int32) % N_TOK
    return pl.pallas_call(
        _invert_body,
        in_specs=[pl.BlockSpec(memory_space=pltpu.SMEM), pl.BlockSpec(memory_space=pl.ANY)],
        out_specs=pl.BlockSpec(memory_space=pltpu.SMEM),
        out_shape=jax.ShapeDtypeStruct((P_SLOTS,), jnp.int32),
        scratch_shapes=[pltpu.SemaphoreType.DMA],
        name="invert",
    )(dest, spare)


def _experts_body(l, be_ref, nx_ref, na_ref, inv_ref, xp_ref, wg_ref, wu_ref, wd_ref, y_ref,
                  xbuf0, xbuf1, gsem0, gsem1, wg_f, wu_f, wd_f, wsem, wslot, wg_s, wu_s, wd_s):
    b = pl.program_id(0)
    n_act = na_ref[0]
    bufs = ((xbuf0, gsem0), (xbuf1, gsem1))

    def weight_copies(e, s):
        return [pltpu.make_async_copy(src.at[l, e], dst.at[s], wsem.at[s])
                for src, dst in ((wg_ref, wg_f), (wu_ref, wu_f), (wd_ref, wd_f))]

    def gather(blk, r, xdst, sem):
        return pltpu.make_async_copy(
            xp_ref.at[pl.ds(inv_ref[blk * TB + r], 1)], xdst.at[pl.ds(r, 1)], sem)

    def wait_gather(xdst, sem):
        pltpu.make_async_copy(xp_ref.at[pl.ds(0, TB)], xdst, sem).wait()

    def step(par):
        xcur, gcur = bufs[par]
        xnxt, gnxt = bufs[1 - par]
        wait_gather(xcur, gcur)

        @pl.when((b == 0) | (be_ref[b] != be_ref[jnp.maximum(b - 1, 0)]))
        def _():
            s = wslot[0]
            for c in weight_copies(be_ref[b], s):
                c.wait()
            wg_s[...] = wg_f[s].astype(BF16)
            wu_s[...] = wu_f[s].astype(BF16)
            wd_s[...] = wd_f[s].astype(BF16)

            @pl.when(nx_ref[b] >= 0)
            def _():
                for c in weight_copies(nx_ref[b], 1 - s):
                    c.start(priority=1)

            wslot[0] = 1 - s

        for r in range(TB):
            gather(b + 1, r, xnxt, gnxt).start()
        x_lo, x_hi = _unpack_halves(xcur[...])
        x_lo, x_hi = x_lo.astype(BF16), x_hi.astype(BF16)
        gate = _dot(x_lo, wg_s[0:PACKED, :]) + _dot(x_hi, wg_s[PACKED:D_MODEL, :])
        up = _dot(x_lo, wu_s[0:PACKED, :]) + _dot(x_hi, wu_s[PACKED:D_MODEL, :])
        hm = (jax.nn.silu(gate) * up).astype(BF16)
        y_ref[...] = _pack_halves(_dot(hm, wd_s[:, 0:PACKED]), _dot(hm, wd_s[:, PACKED:D_MODEL]))

    @pl.when(b == 0)
    def _():
        def first_rows(r, carry):
            gather(0, r, xbuf0, gsem0).start()
            return carry
        lax.fori_loop(0, TB, first_rows, 0)
        wslot[0] = 0
        for c in weight_copies(be_ref[0], 0):
            c.start(priority=1)

    for par in (0, 1):
        @pl.when((b % 2 == par) & (b < n_act))
        def _():
            step(par)

        @pl.when((b % 2 == par) & (b == n_act))
        def _():
            wait_gather(*bufs[par])

    @pl.when(b >= n_act)
    def _():
        y_ref[...] = jnp.zeros((TB, PACKED), jnp.uint32)


def _experts(l, block_e, next_e, n_active, inv, xp, w_gate, w_up, w_down):
    row_buf = pltpu.VMEM((TB, PACKED), jnp.uint32)
    in_out = (D_MODEL, D_EXPERT)
    out_in = (D_EXPERT, D_MODEL)
    return pl.pallas_call(
        functools.partial(_experts_body, l),
        grid_spec=pltpu.PrefetchScalarGridSpec(
            num_scalar_prefetch=4,
            grid=(N_BLOCKS,),
            in_specs=[
                pl.BlockSpec(memory_space=pl.ANY),
                pl.BlockSpec(memory_space=pl.ANY),
                pl.BlockSpec(memory_space=pl.ANY),
                pl.BlockSpec(memory_space=pl.ANY),
            ],
            out_specs=pl.BlockSpec((TB, PACKED), lambda b, be, nx, na, iv: (b, 0)),
            scratch_shapes=[
                row_buf, row_buf,
                pltpu.SemaphoreType.DMA, pltpu.SemaphoreType.DMA,
                pltpu.VMEM((2,) + in_out, F32), pltpu.VMEM((2,) + in_out, F32), pltpu.VMEM((2,) + out_in, F32),
                pltpu.SemaphoreType.DMA((2,)),
                pltpu.SMEM((1,), jnp.int32),
                pltpu.VMEM(in_out, BF16), pltpu.VMEM(in_out, BF16), pltpu.VMEM(out_in, BF16),
            ],
        ),
        out_shape=jax.ShapeDtypeStruct((P_SLOTS, PACKED), jnp.uint32),
        compiler_params=pltpu.CompilerParams(
            dimension_semantics=("arbitrary",), vmem_limit_bytes=VMEM_LIMIT),
        name="experts",
    )(block_e, next_e, n_active, inv, xp, w_gate, w_up, w_down)


def _combine_body(dest_ref, h1_ref, rw_ref, gf_ref, y_ref, out_ref,
                  ya0, yb0, ya1, yb1, sem0, sem1):
    i = pl.program_id(0)
    bufs = ((ya0, yb0, sem0), (ya1, yb1, sem1))

    def fetch(tile, r, k, dst, sem):
        slot = dest_ref[(tile * TC + r) * TOP_K + k]
        return pltpu.make_async_copy(y_ref.at[pl.ds(slot, 1)], dst.at[pl.ds(r, 1)], sem)

    def step(par, last):
        ya, yb, sem = bufs[par]
        for dst in (ya, yb):
            pltpu.make_async_copy(y_ref.at[pl.ds(0, TC)], dst, sem).wait()
        if not last:
            na, nb, nsem = bufs[1 - par]
            for r in range(TC):
                fetch(i + 1, r, 0, na, nsem).start()
                fetch(i + 1, r, 1, nb, nsem).start(priority=1)
        w = rw_ref[...]
        w1, w2 = w[:, 0:1], w[:, 1:2]
        a_lo, a_hi = _unpack_halves(ya[...])
        b_lo, b_hi = _unpack_halves(yb[...])
        lo = h1_ref[:, 0:PACKED] + (w1 * a_lo + w2 * b_lo)
        hi = h1_ref[:, PACKED:D_MODEL] + (w1 * a_hi + w2 * b_hi)
        ms = (jnp.sum(lo * lo, axis=-1, keepdims=True)
              + jnp.sum(hi * hi, axis=-1, keepdims=True)) / D_MODEL
        scale = lax.rsqrt(ms + EPS)
        out_ref[:, 0:PACKED] = (lo * scale) * gf_ref[:, 0:PACKED]
        out_ref[:, PACKED:D_MODEL] = (hi * scale) * gf_ref[:, PACKED:D_MODEL]

    @pl.when(i == 0)
    def _():
        def first_rows(r, carry):
            fetch(0, r, 0, ya0, sem0).start()
            fetch(0, r, 1, yb0, sem0).start()
            return carry
        lax.fori_loop(0, TC, first_rows, 0)

    @pl.when(i % 2 == 0)
    def _():
        step(0, last=False)

    @pl.when((i % 2 == 1) & (i < N_CTILES - 1))
    def _():
        step(1, last=False)

    @pl.when(i == N_CTILES - 1)
    def _():
        step(1, last=True)


def _combine(dest, h1, rw, gf, y):
    row_buf = pltpu.VMEM((TC, PACKED), jnp.uint32)
    return pl.pallas_call(
        _combine_body,
        grid_spec=pltpu.PrefetchScalarGridSpec(
            num_scalar_prefetch=1,
            grid=(N_CTILES,),
            in_specs=[
                pl.BlockSpec((TC, D_MODEL), lambda i, d: (i, 0)),
                pl.BlockSpec((TC, LANES), lambda i, d: (i, 0)),
                pl.BlockSpec((1, D_MODEL), lambda i, d: (0, 0)),
                pl.BlockSpec(memory_space=pl.ANY),
            ],
            out_specs=pl.BlockSpec((TC, D_MODEL), lambda i, d: (i, 0)),
            scratch_shapes=[row_buf, row_buf, row_buf, row_buf,
                            pltpu.SemaphoreType.DMA, pltpu.SemaphoreType.DMA],
        ),
        out_shape=jax.ShapeDtypeStruct((N_TOK, D_MODEL), F32),
        compiler_params=pltpu.CompilerParams(
            dimension_semantics=("arbitrary",), vmem_limit_bytes=VMEM_LIMIT),
        name="combine",
    )(dest, h1, rw, gf, y)


def _route_tables(cnt, ri):
    counts = cnt[0, N_GROUPS:N_GROUPS + N_EXPERTS].astype(jnp.int32)
    padded = (counts + TB - 1) // TB * TB
    pad_end = jnp.cumsum(padded)
    pad_start = pad_end - padded
    experts = jnp.arange(N_EXPERTS, dtype=jnp.int32)
    start_of = jnp.sum(jnp.where(ri[:, 0:TOP_K, None] == experts, pad_start, 0), axis=-1)
    dest = (start_of + ri[:, TOP_K:2 * TOP_K]).reshape(N_ASSIGN)
    n_active = (pad_end[-1:] // TB).astype(jnp.int32)
    block_row = jnp.arange(N_BLOCKS, dtype=jnp.int32)[:, None] * TB
    block_e = jnp.minimum(jnp.sum((pad_end[None, :] <= block_row).astype(jnp.int32), axis=1), N_EXPERTS - 1)
    run_end = jnp.sum(jnp.where(block_e[:, None] == experts, pad_end, 0), axis=-1) // TB
    next_e = jnp.where(run_end < n_active[0], block_e[jnp.minimum(run_end, N_BLOCKS - 1)], -1)
    return dest, block_e, next_e.astype(jnp.int32), n_active


def kernel(x, norm1_g, w_in, b_gate, conv_w, ln_v_g, ln_v_b, sgu_w, sgu_b, w_branch, w_out, norm2_g,
           router_g, router_g_b, router_e, router_e_b, w_gate, w_up, w_down, final_g):
    h = x.reshape(N_TOK, D_MODEL)

    w_in_b = w_in.astype(BF16)
    w_branch_b = w_branch.astype(BF16)
    w_out_b = w_out.astype(BF16)
    g1 = norm1_g.reshape(DEPTH, 1, D_MODEL)
    g2 = norm2_g.reshape(DEPTH, 1, D_MODEL)
    gf = final_g.reshape(1, D_MODEL)
    ln_g = ln_v_g.reshape(DEPTH, 1, WIDTH)
    ln_b = ln_v_b.reshape(DEPTH, 1, WIDTH)
    bg = b_gate.reshape(DEPTH, 1, 2 * D_MODEL)
    sgu_bias = jnp.repeat(jnp.swapaxes(sgu_b, 1, 2), GROUP_DIM, axis=2)
    pad = LANES - N_GROUPS - N_EXPERTS
    w_r = jnp.concatenate([router_g, router_e, jnp.zeros((DEPTH, D_MODEL, pad), F32)], axis=2)
    wr_hi = w_r.astype(BF16)
    w_router = jnp.concatenate([wr_hi, (w_r - wr_hi.astype(F32)).astype(BF16)], axis=2)
    r_bias = jnp.concatenate([router_g_b, router_e_b, jnp.zeros((DEPTH, pad), F32)], axis=1)
    r_bias = r_bias.reshape(DEPTH, 1, LANES)

    moe = None
    for l in range(DEPTH):
        if moe is None:
            ya, yb = _mixer_in(l, TN, h, g1, w_in_b, conv_w, ln_g, ln_b, sgu_w, sgu_bias)
        else:
            ya, yb, h = _mixer_in(l, TN, None, g1, w_in_b, conv_w, ln_g, ln_b, sgu_w, sgu_bias, moe)
        h1, xp, ri, rw, cnt = _mixer_out(l, h, g1, ya, yb, w_in_b, bg, w_branch_b, w_out_b, g2,
                                         w_router, r_bias)
        dest, block_e, next_e, n_active = _route_tables(cnt, ri)
        y = _experts(l, block_e, next_e, n_active, _invert(dest), xp, w_gate, w_up, w_down)
        moe = (dest, h1, rw, y)
    return _combine(*moe[:3], gf, moe[3]).reshape(BATCH, SEQ, D_MODEL)
```

```python
import functools

import jax
import jax.numpy as jnp
from jax import lax
from jax.experimental import pallas as pl
from jax.experimental.pallas import tpu as pltpu

F32 = jnp.float32
BF16 = jnp.bfloat16

D_MODEL = 2048
BATCH = 4
SEQ = 4096
DEPTH = 2
N_TOK = BATCH * SEQ
WIDTH = D_MODEL // 2
CONV_K = 3
SGU_GROUPS = 8
CHUNK = 128
GROUP_DIM = WIDTH // SGU_GROUPS
D_IN = 3 * WIDTH + 2 * WIDTH + 2 * D_MODEL
N_GROUPS = 4
PER_GROUP = 8
N_EXPERTS = N_GROUPS * PER_GROUP
TOP_K = 2
N_ASSIGN = N_TOK * TOP_K
D_EXPERT = D_MODEL // 4
EPS = 1e-6

LANES = 128
SUBLANES = 8
VMEM_LIMIT = 56 * 1024 * 1024

COL_C, COL_B, COL_XA = 0, WIDTH, 2 * WIDTH
COL_U, COL_V = 3 * WIDTH, 4 * WIDTH
COL_GA, COL_GB = 5 * WIDTH, 5 * WIDTH + D_MODEL

TM = 512
TN = 512
TILES_PER_SEQ = SEQ // TM
J1 = D_MODEL // TN
J2 = D_MODEL // TN
J_OUT = J1 + J2

PACKED = D_MODEL // 2
assert (J2 // 2) * TN == PACKED
TB = 256
P_SLOTS = N_ASSIGN + N_EXPERTS * TB
N_BLOCKS = P_SLOTS // TB
BLOCKS_PER_STEP = 2
assert N_BLOCKS % BLOCKS_PER_STEP == 0
INVERT_UNROLL = 16
TC = 512
N_CTILES = N_TOK // TC
assert N_CTILES % 2 == 0


def _dot(a, b):
    return jnp.dot(a, b, preferred_element_type=F32)


def _pack_halves(lo, hi):
    lo_bits = pltpu.bitcast(lo.astype(BF16).astype(F32), jnp.uint32)
    hi_bits = pltpu.bitcast(hi.astype(BF16).astype(F32), jnp.uint32)
    return lax.shift_right_logical(lo_bits, jnp.uint32(16)) | (hi_bits & jnp.uint32(0xFFFF0000))


def _unpack_halves(words):
    lo = pltpu.bitcast(lax.shift_left(words, jnp.uint32(16)), F32)
    hi = pltpu.bitcast(words & jnp.uint32(0xFFFF0000), F32)
    return lo, hi


def _mixer_in_body(tn, fused, *refs):
    if fused:
        (dest_ref, h1_ref, rw_ref, y_ref, g1_ref, wc_ref, wb_ref, wa_ref, wu_ref, wv_ref, cw_ref, lng_ref,
         lnb_ref, sw_ref, sb_ref, ya_ref, yb_ref, h_ref, xn_scr, xc_scr, carry_scr, u_scr, v_scr,
         fa, fb, fsem) = refs
    else:
        (h_ref, g1_ref, wc_ref, wb_ref, wa_ref, wu_ref, wv_ref, cw_ref, lng_ref, lnb_ref, sw_ref, sb_ref,
         ya_ref, yb_ref, xn_scr, xc_scr, carry_scr, u_scr, v_scr) = refs
    jc = WIDTH // tn
    jz = WIDTH // tn
    n_tiles = N_TOK // TM
    rows_per_step = TM // (jc + jz)
    i = pl.program_id(0)
    j = pl.program_id(1)

    def fetch(tile, r, k):
        src = y_ref.at[pl.ds(dest_ref[(tile * TM + r) * TOP_K + k], 1)]
        return pltpu.make_async_copy(src, (fa, fb)[k].at[pl.ds(r, 1)], fsem)

    def wait_fetch():
        for dst in (fa, fb):
            pltpu.make_async_copy(y_ref.at[pl.ds(0, TM)], dst, fsem).wait()

    def issue_fetch(step):
        if fused:
            nxt = jnp.where(i + 1 == n_tiles, 0, i + 1)
            for r in range(step * rows_per_step, (step + 1) * rows_per_step):
                fetch(nxt, r, 0).start()
                fetch(nxt, r, 1).start(priority=1)

    if fused:
        @pl.when((i == 0) & (j == 0))
        def _():
            def first_rows(r, carry):
                fetch(0, r, 0).start()
                fetch(0, r, 1).start()
                return carry
            lax.fori_loop(0, TM, first_rows, 0)

    @pl.when(j == 0)
    def _():
        if fused:
            wait_fetch()
            w = rw_ref[...]
            w1, w2 = w[:, 0:1], w[:, 1:2]
            a_lo, a_hi = _unpack_halves(fa[...])
            b_lo, b_hi = _unpack_halves(fb[...])
            halves = (h1_ref[:, 0:PACKED] + (w1 * a_lo + w2 * b_lo),
                      h1_ref[:, PACKED:D_MODEL] + (w1 * a_hi + w2 * b_hi))
            ms = sum(jnp.sum(v * v, axis=-1, keepdims=True) for v in halves) / D_MODEL
            scale = lax.rsqrt(ms + EPS)
            for c, v in enumerate(halves):
                cols = slice(c * PACKED, (c + 1) * PACKED)
                h_ref[:, cols] = v
                xn_scr[:, cols] = ((v * scale) * g1_ref[:, cols]).astype(BF16)
        else:
            x = h_ref[...]
            ms = jnp.mean(x * x, axis=-1, keepdims=True)
            xn_scr[...] = ((x * lax.rsqrt(ms + EPS)) * g1_ref[...]).astype(BF16)

    def conv_step(jv):
        issue_fetch(jv)
        xn = xn_scr[...]
        xc = _dot(xn, wc_ref[...]) * _dot(xn, wa_ref[...])

        @pl.when(i % TILES_PER_SEQ == 0)
        def _():
            xc_scr[0:SUBLANES, :] = jnp.zeros((SUBLANES, tn), F32)

        @pl.when(i % TILES_PER_SEQ != 0)
        def _():
            xc_scr[0:SUBLANES, :] = carry_scr[jv]

        xc_scr[SUBLANES:SUBLANES + TM, :] = xc
        x1 = xc_scr[SUBLANES - 1:SUBLANES - 1 + TM, :]
        x2 = xc_scr[SUBLANES - 2:SUBLANES - 2 + TM, :]
        cw = cw_ref[...]
        conv = cw[0:1, :] * x2 + cw[1:2, :] * x1 + cw[2:3, :] * xc
        carry_scr[jv] = xc_scr[TM:TM + SUBLANES, :]
        ya_ref[...] = (_dot(xn, wb_ref[...]) * conv).astype(BF16)

    def gating_step(jv):
        issue_fetch(jc + jv)
        xn = xn_scr[...]
        u_scr[jv] = jax.nn.gelu(_dot(xn, wu_ref[...]))
        v_scr[jv] = jax.nn.gelu(_dot(xn, wv_ref[...]))

    for jv in range(jc):
        pl.when(j == jv)(functools.partial(conv_step, jv))
    for jv in range(jz):
        pl.when(j == jc + jv)(functools.partial(gating_step, jv))

    @pl.when(j == jc + jz - 1)
    def _():
        s1 = jnp.zeros((TM, 1), F32)
        for k in range(jz):
            s1 = s1 + jnp.sum(v_scr[k], axis=-1, keepdims=True)
        mu = s1 / WIDTH
        s2 = jnp.zeros((TM, 1), F32)
        for k in range(jz):
            d = v_scr[k] - mu
            s2 = s2 + jnp.sum(d * d, axis=-1, keepdims=True)
        rstd = lax.rsqrt(s2 / WIDTH + EPS)
        row = lax.broadcasted_iota(jnp.int32, (CHUNK, CHUNK), 0)
        col = lax.broadcasted_iota(jnp.int32, (CHUNK, CHUNK), 1)
        causal = col <= row
        gpt = tn // GROUP_DIM
        for k in range(jz):
            vn = (((v_scr[k] - mu) * rstd) * lng_ref[:, k * tn:(k + 1) * tn]
                  + lnb_ref[:, k * tn:(k + 1) * tn]).astype(BF16)
            for gl in range(gpt):
                g = k * gpt + gl
                w = jnp.where(causal, sw_ref[g], 0.0).astype(BF16)
                bias = sb_ref[:, g * GROUP_DIM:(g + 1) * GROUP_DIM]
                for n in range(TM // CHUNK):
                    rows = slice(n * CHUNK, (n + 1) * CHUNK)
                    mixed = _dot(w, vn[rows, gl * GROUP_DIM:(gl + 1) * GROUP_DIM])
                    u = u_scr[k, rows, gl * GROUP_DIM:(gl + 1) * GROUP_DIM]
                    yb_ref[rows, g * GROUP_DIM:(g + 1) * GROUP_DIM] = (u * (mixed + bias)).astype(BF16)

    if fused:
        @pl.when((i == n_tiles - 1) & (j == jc + jz - 1))
        def _():
            wait_fetch()


def _mixer_in(l, tn, h, g1, w_in, conv_w, ln_g, ln_b, sgu_w, sgu_bias, moe=None):
    fused = moe is not None
    jc = jz = WIDTH // tn
    cj = lambda j: jnp.minimum(j, jc - 1)
    zj = lambda j: jnp.clip(j - jc, 0, jz - 1)
    wspec = lambda off, f: pl.BlockSpec((None, D_MODEL, tn), lambda i, j, *_: (l, 0, off // tn + f(j)))
    row_tile = pl.BlockSpec((TM, D_MODEL), lambda i, j, *_: (i, 0))
    in_specs = [
        pl.BlockSpec((None, 1, D_MODEL), lambda i, j, *_: (l, 0, 0)),
        wspec(COL_C, cj), wspec(COL_B, cj), wspec(COL_XA, cj), wspec(COL_U, zj), wspec(COL_V, zj),
        pl.BlockSpec((None, CONV_K, tn), lambda i, j, *_: (l, 0, cj(j))),
        pl.BlockSpec((None, 1, WIDTH), lambda i, j, *_: (l, 0, 0)),
        pl.BlockSpec((None, 1, WIDTH), lambda i, j, *_: (l, 0, 0)),
        pl.BlockSpec((None, SGU_GROUPS, CHUNK, CHUNK), lambda i, j, *_: (l, 0, 0, 0)),
        pl.BlockSpec((None, CHUNK, WIDTH), lambda i, j, *_: (l, 0, 0)),
    ]
    out_specs = [
        pl.BlockSpec((TM, tn), lambda i, j, *_: (i, cj(j))),
        pl.BlockSpec((TM, WIDTH), lambda i, j, *_: (i, 0)),
    ]
    out_shape = [jax.ShapeDtypeStruct((N_TOK, WIDTH), BF16), jax.ShapeDtypeStruct((N_TOK, WIDTH), BF16)]
    scratch = [
        pltpu.VMEM((TM, D_MODEL), BF16),
        pltpu.VMEM((TM + SUBLANES, tn), F32),
        pltpu.VMEM((jc, SUBLANES, tn), F32),
        pltpu.VMEM((jz, TM, tn), F32),
        pltpu.VMEM((jz, TM, tn), F32),
    ]
    weights = (g1, w_in, w_in, w_in, w_in, w_in, conv_w, ln_g, ln_b, sgu_w, sgu_bias)
    if fused:
        dest, h1, rw, y = moe
        in_specs = [row_tile, pl.BlockSpec((TM, LANES), lambda i, j, *_: (i, 0)),
                    pl.BlockSpec(memory_space=pl.ANY)] + in_specs
        out_specs.append(row_tile)
        out_shape.append(jax.ShapeDtypeStruct((N_TOK, D_MODEL), F32))
        scratch += [pltpu.VMEM((TM, PACKED), jnp.uint32), pltpu.VMEM((TM, PACKED), jnp.uint32),
                    pltpu.SemaphoreType.DMA]
        operands = (dest, h1, rw, y) + weights
    else:
        in_specs = [row_tile] + in_specs
        operands = (h,) + weights
    return pl.pallas_call(
        functools.partial(_mixer_in_body, tn, fused),
        grid_spec=pltpu.PrefetchScalarGridSpec(
            num_scalar_prefetch=1 if fused else 0,
            grid=(N_TOK // TM, jc + jz),
            in_specs=in_specs, out_specs=out_specs, scratch_shapes=scratch),
        out_shape=out_shape,
        compiler_params=pltpu.CompilerParams(
            dimension_semantics=("arbitrary", "arbitrary"), vmem_limit_bytes=VMEM_LIMIT),
        name="mixer_in",
    )(*operands)


def _mixer_out_body(h_ref, g1_ref, ya_ref, yb_ref, wga_ref, wgb_ref, bga_ref, bgb_ref, wba_ref, wbb_ref,
                    wo_ref, g2_ref, wr_ref, rb_ref,
                    h1_ref, xp_ref, ri_ref, rw_ref, cnt_ref,
                    xn_scr, mg_scr, out_scr, cnt_scr):
    i = pl.program_id(0)
    j = pl.program_id(1)

    @pl.when(j == 0)
    def _():
        x = h_ref[...]
        ms = jnp.mean(x * x, axis=-1, keepdims=True)
        xn_scr[...] = ((x * lax.rsqrt(ms + EPS)) * g1_ref[...]).astype(BF16)

    @pl.when((i == 0) & (j == 0))
    def _():
        cnt_scr[...] = jnp.zeros((1, LANES), F32)

    @pl.when(j < J1)
    def _():
        xn = xn_scr[...]
        ga = jax.nn.sigmoid(_dot(xn, wga_ref[...]) + bga_ref[...])
        gb = jax.nn.sigmoid(_dot(xn, wgb_ref[...]) + bgb_ref[...])
        mg_scr[j] = (ga * _dot(ya_ref[...], wba_ref[...]) + gb * _dot(yb_ref[...], wbb_ref[...])).astype(BF16)

    @pl.when(j >= J1)
    def _():
        acc = _dot(mg_scr[0], wo_ref[0:TN, :])
        for k in range(1, J1):
            acc = acc + _dot(mg_scr[k], wo_ref[k * TN:(k + 1) * TN, :])
        out_scr[j - J1] = acc

    @pl.when(j == J_OUT - 1)
    def _():
        ss = jnp.zeros((TM, 1), F32)
        for k in range(J2):
            cols = slice(k * TN, (k + 1) * TN)
            hk = h_ref[:, cols] + out_scr[k]
            h1_ref[:, cols] = hk
            ss = ss + jnp.sum(hk * hk, axis=-1, keepdims=True)
        rstd = lax.rsqrt(ss / D_MODEL + EPS)

        acc = jnp.zeros((TM, 2 * LANES), F32)
        for k in range(J2):
            cols = slice(k * TN, (k + 1) * TN)
            xk = (h1_ref[:, cols] * rstd) * g2_ref[:, cols]
            hi = xk.astype(BF16)
            lo = (xk - hi.astype(F32)).astype(BF16)
            acc = acc + (_dot(hi, wr_ref[cols, :]) + _dot(lo, wr_ref[cols, :]))
            bits = pltpu.bitcast(hi.astype(F32), jnp.uint32)
            if k < J2 // 2:
                xp_ref[:, cols] = lax.shift_right_logical(bits, jnp.uint32(16))
            else:
                pcols = slice(k * TN - PACKED, (k + 1) * TN - PACKED)
                xp_ref[:, pcols] = xp_ref[:, pcols] | (bits & jnp.uint32(0xFFFF0000))
        lg = (acc[:, 0:LANES] + acc[:, LANES:2 * LANES]) + rb_ref[...]

        lane_i = lax.broadcasted_iota(jnp.int32, (TM, LANES), 1)
        lane = lane_i.astype(F32)
        neg = -jnp.inf
        big = float(LANES)
        is_g = lane_i < N_GROUPS
        gl = jnp.where(is_g, lg, neg)
        gmax = jnp.max(gl, axis=-1, keepdims=True)
        g_idx = jnp.min(jnp.where(gl == gmax, lane, big), axis=-1, keepdims=True)
        g_w = 1.0 / jnp.sum(jnp.exp(gl - gmax), axis=-1, keepdims=True)

        first = N_GROUPS + g_idx * PER_GROUP
        in_grp = (lane >= first) & (lane < first + PER_GROUP)
        el = jnp.where(in_grp, lg, neg)
        m1 = jnp.max(el, axis=-1, keepdims=True)
        i1 = jnp.min(jnp.where(in_grp & (el == m1), lane, big), axis=-1, keepdims=True)
        el2 = jnp.where(lane == i1, neg, el)
        m2 = jnp.max(el2, axis=-1, keepdims=True)
        i2 = jnp.min(jnp.where(in_grp & (lane != i1) & (el2 == m2), lane, big), axis=-1, keepdims=True)
        t = jnp.exp(m2 - m1)
        w1 = g_w * (1.0 / (1.0 + t))
        w2 = g_w * (t / (1.0 + t))

        onehot = jnp.where((lane == i1) | (lane == i2), 1.0, 0.0)
        r_i = lax.broadcasted_iota(jnp.int32, (TM, TM), 0)
        c_i = lax.broadcasted_iota(jnp.int32, (TM, TM), 1)
        tri = jnp.where(c_i < r_i, 1.0, 0.0).astype(BF16)
        before = cnt_scr[...] + _dot(tri, onehot.astype(BF16))
        rank1 = jnp.sum(jnp.where(lane == i1, before, 0.0), axis=-1, keepdims=True)
        rank2 = jnp.sum(jnp.where(lane == i2, before, 0.0), axis=-1, keepdims=True)
        cnt_scr[...] = cnt_scr[...] + jnp.sum(onehot, axis=0, keepdims=True)

        ri = jnp.where(lane_i == 0, i1 - N_GROUPS,
                       jnp.where(lane_i == 1, i2 - N_GROUPS,
                                 jnp.where(lane_i == 2, rank1, jnp.where(lane_i == 3, rank2, 0.0))))
        ri_ref[...] = ri.astype(jnp.int32)
        rw_ref[...] = jnp.where(lane_i == 0, w1, jnp.where(lane_i == 1, w2, 0.0))
        cnt_ref[...] = jnp.broadcast_to(cnt_scr[...], (SUBLANES, LANES))


def _mixer_out(l, h, g1, ya, yb, w_in, b_gate, w_branch, w_out, g2, w_router, r_bias):
    mj = lambda j: jnp.minimum(j, J1 - 1)
    oj = lambda j: jnp.clip(j - J1, 0, J2 - 1)
    const = lambda *blk: pl.BlockSpec((None,) + blk, lambda i, j: (l,) + (0,) * len(blk))
    return pl.pallas_call(
        _mixer_out_body,
        grid=(N_TOK // TM, J_OUT),
        in_specs=[
            pl.BlockSpec((TM, D_MODEL), lambda i, j: (i, 0)),
            const(1, D_MODEL),
            pl.BlockSpec((TM, WIDTH), lambda i, j: (i, 0)),
            pl.BlockSpec((TM, WIDTH), lambda i, j: (i, 0)),
            pl.BlockSpec((None, D_MODEL, TN), lambda i, j: (l, 0, COL_GA // TN + mj(j))),
            pl.BlockSpec((None, D_MODEL, TN), lambda i, j: (l, 0, COL_GB // TN + mj(j))),
            pl.BlockSpec((None, 1, TN), lambda i, j: (l, 0, mj(j))),
            pl.BlockSpec((None, 1, TN), lambda i, j: (l, 0, D_MODEL // TN + mj(j))),
            pl.BlockSpec((None, None, WIDTH, TN), lambda i, j: (l, 0, 0, mj(j))),
            pl.BlockSpec((None, None, WIDTH, TN), lambda i, j: (l, 1, 0, mj(j))),
            pl.BlockSpec((None, D_MODEL, TN), lambda i, j: (l, 0, oj(j))),
            const(1, D_MODEL),
            const(D_MODEL, 2 * LANES), const(1, LANES),
        ],
        out_specs=[
            pl.BlockSpec((TM, D_MODEL), lambda i, j: (i, 0)),
            pl.BlockSpec((TM, PACKED), lambda i, j: (i, 0)),
            pl.BlockSpec((TM, LANES), lambda i, j: (i, 0)),
            pl.BlockSpec((TM, LANES), lambda i, j: (i, 0)),
            pl.BlockSpec((SUBLANES, LANES), lambda i, j: (0, 0)),
        ],
        out_shape=[jax.ShapeDtypeStruct((N_TOK, D_MODEL), F32),
                   jax.ShapeDtypeStruct((N_TOK, PACKED), jnp.uint32),
                   jax.ShapeDtypeStruct((N_TOK, LANES), jnp.int32),
                   jax.ShapeDtypeStruct((N_TOK, LANES), F32),
                   jax.ShapeDtypeStruct((SUBLANES, LANES), F32)],
        scratch_shapes=[
            pltpu.VMEM((TM, D_MODEL), BF16),
            pltpu.VMEM((J1, TM, TN), BF16),
            pltpu.VMEM((J2, TM, TN), F32),
            pltpu.VMEM((1, LANES), F32),
        ],
        compiler_params=pltpu.CompilerParams(
            dimension_semantics=("arbitrary", "arbitrary"), vmem_limit_bytes=VMEM_LIMIT),
        name="mixer_out",
    )(h, g1, ya, yb, w_in, w_in, b_gate, b_gate, w_branch, w_branch, w_out, g2, w_router, r_bias)


def _invert_body(dest_ref, spare_ref, inv_ref, sem):
    fill = pltpu.make_async_copy(spare_ref, inv_ref, sem)
    fill.start()
    fill.wait()

    def place(c, carry):
        t0 = c * INVERT_UNROLL
        for u in range(INVERT_UNROLL):
            for k in range(TOP_K):
                inv_ref[dest_ref[(t0 + u) * TOP_K + k]] = t0 + u
        return carry

    lax.fori_loop(0, N_TOK // INVERT_UNROLL, place, 0)


def _invert(dest):
    spare = jnp.arange(P_SLOTS, dtype=jnp.int32) % N_TOK
    return pl.pallas_call(
        _invert_body,
        in_specs=[pl.BlockSpec(memory_space=pltpu.SMEM), pl.BlockSpec(memory_space=pl.ANY)],
        out_specs=pl.BlockSpec(memory_space=pltpu.SMEM),
        out_shape=jax.ShapeDtypeStruct((P_SLOTS,), jnp.int32),
        scratch_shapes=[pltpu.SemaphoreType.DMA],
        name="invert",
    )(dest, spare)


def _experts_body(l, be_ref, nx_ref, na_ref, inv_ref, xp_ref, wg_ref, wu_ref, wd_ref, y_ref,
                  xbuf0, xbuf1, gsem0, gsem1, wg_f, wu_f, wd_f, wsem, wslot, wg_s, wu_s, wd_s):
    n_act = na_ref[0]
    bufs = ((xbuf0, gsem0), (xbuf1, gsem1))

    def weight_copies(e, s):
        return [pltpu.make_async_copy(src.at[l, e], dst.at[s], wsem.at[s])
                for src, dst in ((wg_ref, wg_f), (wu_ref, wu_f), (wd_ref, wd_f))]

    def gather(blk, r, xdst, sem):
        return pltpu.make_async_copy(
            xp_ref.at[pl.ds(inv_ref[blk * TB + r], 1)], xdst.at[pl.ds(r, 1)], sem)

    def wait_gather(xdst, sem):
        pltpu.make_async_copy(xp_ref.at[pl.ds(0, TB)], xdst, sem).wait()

    def step(b, par):
        xcur, gcur = bufs[par]
        xnxt, gnxt = bufs[1 - par]
        wait_gather(xcur, gcur)

        @pl.when((b == 0) | (be_ref[b] != be_ref[jnp.maximum(b - 1, 0)]))
        def _():
            s = wslot[0]
            for c in weight_copies(be_ref[b], s):
                c.wait()
            wg_s[...] = wg_f[s].astype(BF16)
            wu_s[...] = wu_f[s].astype(BF16)
            wd_s[...] = wd_f[s].astype(BF16)

            @pl.when(nx_ref[b] >= 0)
            def _():
                for c in weight_copies(nx_ref[b], 1 - s):
                    c.start(priority=1)

            wslot[0] = 1 - s

        for r in range(TB):
            gather(b + 1, r, xnxt, gnxt).start()
        x_lo, x_hi = _unpack_halves(xcur[...])
        x_lo, x_hi = x_lo.astype(BF16), x_hi.astype(BF16)
        gate = _dot(x_lo, wg_s[0:PACKED, :]) + _dot(x_hi, wg_s[PACKED:D_MODEL, :])
        up = _dot(x_lo, wu_s[0:PACKED, :]) + _dot(x_hi, wu_s[PACKED:D_MODEL, :])
        hm = (jax.nn.silu(gate) * up).astype(BF16)
        y_ref[par * TB:(par + 1) * TB, :] = _pack_halves(
            _dot(hm, wd_s[:, 0:PACKED]), _dot(hm, wd_s[:, PACKED:D_MODEL]))

    @pl.when(pl.program_id(0) == 0)
    def _():
        def first_rows(r, carry):
            gather(0, r, xbuf0, gsem0).start()
            return carry
        lax.fori_loop(0, TB, first_rows, 0)
        wslot[0] = 0
        for c in weight_copies(be_ref[0], 0):
            c.start(priority=1)

    for par in (0, 1):
        b = BLOCKS_PER_STEP * pl.program_id(0) + par
        pl.when(b < n_act)(functools.partial(step, b, par))

        @pl.when(b == n_act)
        def _():
            wait_gather(*bufs[par])

        @pl.when(b >= n_act)
        def _():
            y_ref[par * TB:(par + 1) * TB, :] = jnp.zeros((TB, PACKED), jnp.uint32)


def _experts(l, block_e, next_e, n_active, inv, xp, w_gate, w_up, w_down):
    row_buf = pltpu.VMEM((TB, PACKED), jnp.uint32)
    in_out = (D_MODEL, D_EXPERT)
    out_in = (D_EXPERT, D_MODEL)
    return pl.pallas_call(
        functools.partial(_experts_body, l),
        grid_spec=pltpu.PrefetchScalarGridSpec(
            num_scalar_prefetch=4,
            grid=(N_BLOCKS // BLOCKS_PER_STEP,),
            in_specs=[
                pl.BlockSpec(memory_space=pl.ANY),
                pl.BlockSpec(memory_space=pl.ANY),
                pl.BlockSpec(memory_space=pl.ANY),
                pl.BlockSpec(memory_space=pl.ANY),
            ],
            out_specs=pl.BlockSpec((BLOCKS_PER_STEP * TB, PACKED), lambda s, be, nx, na, iv: (s, 0)),
            scratch_shapes=[
                row_buf, row_buf,
                pltpu.SemaphoreType.DMA, pltpu.SemaphoreType.DMA,
                pltpu.VMEM((2,) + in_out, F32), pltpu.VMEM((2,) + in_out, F32), pltpu.VMEM((2,) + out_in, F32),
                pltpu.SemaphoreType.DMA((2,)),
                pltpu.SMEM((1,), jnp.int32),
                pltpu.VMEM(in_out, BF16), pltpu.VMEM(in_out, BF16), pltpu.VMEM(out_in, BF16),
            ],
        ),
        out_shape=jax.ShapeDtypeStruct((P_SLOTS, PACKED), jnp.uint32),
        compiler_params=pltpu.CompilerParams(
            dimension_semantics=("arbitrary",), vmem_limit_bytes=VMEM_LIMIT),
        name="experts",
    )(block_e, next_e, n_active, inv, xp, w_gate, w_up, w_down)


def _combine_body(dest_ref, h1_ref, rw_ref, gf_ref, y_ref, out_ref,
                  ya0, yb0, ya1, yb1, sem0, sem1):
    i = pl.program_id(0)
    bufs = ((ya0, yb0, sem0), (ya1, yb1, sem1))

    def fetch(tile, r, k, dst, sem):
        slot = dest_ref[(tile * TC + r) * TOP_K + k]
        return pltpu.make_async_copy(y_ref.at[pl.ds(slot, 1)], dst.at[pl.ds(r, 1)], sem)

    def step(par, last):
        ya, yb, sem = bufs[par]
        for dst in (ya, yb):
            pltpu.make_async_copy(y_ref.at[pl.ds(0, TC)], dst, sem).wait()
        if not last:
            na, nb, nsem = bufs[1 - par]
            for r in range(TC):
                fetch(i + 1, r, 0, na, nsem).start()
                fetch(i + 1, r, 1, nb, nsem).start(priority=1)
        w = rw_ref[...]
        w1, w2 = w[:, 0:1], w[:, 1:2]
        a_lo, a_hi = _unpack_halves(ya[...])
        b_lo, b_hi = _unpack_halves(yb[...])
        lo = h1_ref[:, 0:PACKED] + (w1 * a_lo + w2 * b_lo)
        hi = h1_ref[:, PACKED:D_MODEL] + (w1 * a_hi + w2 * b_hi)
        ms = (jnp.sum(lo * lo, axis=-1, keepdims=True)
              + jnp.sum(hi * hi, axis=-1, keepdims=True)) / D_MODEL
        scale = lax.rsqrt(ms + EPS)
        out_ref[:, 0:PACKED] = (lo * scale) * gf_ref[:, 0:PACKED]
        out_ref[:, PACKED:D_MODEL] = (hi * scale) * gf_ref[:, PACKED:D_MODEL]

    @pl.when(i == 0)
    def _():
        def first_rows(r, carry):
            fetch(0, r, 0, ya0, sem0).start()
            fetch(0, r, 1, yb0, sem0).start()
            return carry
        lax.fori_loop(0, TC, first_rows, 0)

    @pl.when(i % 2 == 0)
    def _():
        step(0, last=False)

    @pl.when((i % 2 == 1) & (i < N_CTILES - 1))
    def _():
        step(1, last=False)

    @pl.when(i == N_CTILES - 1)
    def _():
        step(1, last=True)


def _combine(dest, h1, rw, gf, y):
    row_buf = pltpu.VMEM((TC, PACKED), jnp.uint32)
    return pl.pallas_call(
        _combine_body,
        grid_spec=pltpu.PrefetchScalarGridSpec(
            num_scalar_prefetch=1,
            grid=(N_CTILES,),
            in_specs=[
                pl.BlockSpec((TC, D_MODEL), lambda i, d: (i, 0)),
                pl.BlockSpec((TC, LANES), lambda i, d: (i, 0)),
                pl.BlockSpec((1, D_MODEL), lambda i, d: (0, 0)),
                pl.BlockSpec(memory_space=pl.ANY),
            ],
            out_specs=pl.BlockSpec((TC, D_MODEL), lambda i, d: (i, 0)),
            scratch_shapes=[row_buf, row_buf, row_buf, row_buf,
                            pltpu.SemaphoreType.DMA, pltpu.SemaphoreType.DMA],
        ),
        out_shape=jax.ShapeDtypeStruct((N_TOK, D_MODEL), F32),
        compiler_params=pltpu.CompilerParams(
            dimension_semantics=("arbitrary",), vmem_limit_bytes=VMEM_LIMIT),
        name="combine",
    )(dest, h1, rw, gf, y)


def _route_tables(cnt, ri):
    counts = cnt[0, N_GROUPS:N_GROUPS + N_EXPERTS].astype(jnp.int32)
    padded = (counts + TB - 1) // TB * TB
    pad_end = jnp.cumsum(padded)
    pad_start = pad_end - padded
    experts = jnp.arange(N_EXPERTS, dtype=jnp.int32)
    start_of = jnp.sum(jnp.where(ri[:, 0:TOP_K, None] == experts, pad_start, 0), axis=-1)
    dest = (start_of + ri[:, TOP_K:2 * TOP_K]).reshape(N_ASSIGN)
    n_active = (pad_end[-1:] // TB).astype(jnp.int32)
    block_row = jnp.arange(N_BLOCKS, dtype=jnp.int32)[:, None] * TB
    block_e = jnp.minimum(jnp.sum((pad_end[None, :] <= block_row).astype(jnp.int32), axis=1), N_EXPERTS - 1)
    run_end = jnp.sum(jnp.where(block_e[:, None] == experts, pad_end, 0), axis=-1) // TB
    next_e = jnp.where(run_end < n_active[0], block_e[jnp.minimum(run_end, N_BLOCKS - 1)], -1)
    return dest, block_e, next_e.astype(jnp.int32), n_active


def kernel(x, norm1_g, w_in, b_gate, conv_w, ln_v_g, ln_v_b, sgu_w, sgu_b, w_branch, w_out, norm2_g,
           router_g, router_g_b, router_e, router_e_b, w_gate, w_up, w_down, final_g):
    h = x.reshape(N_TOK, D_MODEL)

    w_in_b = w_in.astype(BF16)
    w_branch_b = w_branch.astype(BF16)
    w_out_b = w_out.astype(BF16)
    g1 = norm1_g.reshape(DEPTH, 1, D_MODEL)
    g2 = norm2_g.reshape(DEPTH, 1, D_MODEL)
    gf = final_g.reshape(1, D_MODEL)
    ln_g = ln_v_g.reshape(DEPTH, 1, WIDTH)
    ln_b = ln_v_b.reshape(DEPTH, 1, WIDTH)
    bg = b_gate.reshape(DEPTH, 1, 2 * D_MODEL)
    sgu_bias = jnp.repeat(jnp.swapaxes(sgu_b, 1, 2), GROUP_DIM, axis=2)
    pad = LANES - N_GROUPS - N_EXPERTS
    w_r = jnp.concatenate([router_g, router_e, jnp.zeros((DEPTH, D_MODEL, pad), F32)], axis=2)
    wr_hi = w_r.astype(BF16)
    w_router = jnp.concatenate([wr_hi, (w_r - wr_hi.astype(F32)).astype(BF16)], axis=2)
    r_bias = jnp.concatenate([router_g_b, router_e_b, jnp.zeros((DEPTH, pad), F32)], axis=1)
    r_bias = r_bias.reshape(DEPTH, 1, LANES)

    moe = None
    for l in range(DEPTH):
        if moe is None:
            ya, yb = _mixer_in(l, TN, h, g1, w_in_b, conv_w, ln_g, ln_b, sgu_w, sgu_bias)
        else:
            ya, yb, h = _mixer_in(l, TN, None, g1, w_in_b, conv_w, ln_g, ln_b, sgu_w, sgu_bias, moe)
        h1, xp, ri, rw, cnt = _mixer_out(l, h, g1, ya, yb, w_in_b, bg, w_branch_b, w_out_b, g2,
                                         w_router, r_bias)
        dest, block_e, next_e, n_active = _route_tables(cnt, ri)
        y = _experts(l, block_e, next_e, n_active, _invert(dest), xp, w_gate, w_up, w_down)
        moe = (dest, h1, rw, y)
    return _combine(*moe[:3], gf, moe[3]).reshape(BATCH, SEQ, D_MODEL)
```

```python
import functools

import jax
import jax.numpy as jnp
from jax import lax
from jax.experimental import pallas as pl
from jax.experimental.pallas import tpu as pltpu

F32 = jnp.float32
BF16 = jnp.bfloat16

D_MODEL = 2048
BATCH = 4
SEQ = 4096
DEPTH = 2
N_TOK = BATCH * SEQ
WIDTH = D_MODEL // 2
CONV_K = 3
SGU_GROUPS = 8
CHUNK = 128
GROUP_DIM = WIDTH // SGU_GROUPS
D_IN = 3 * WIDTH + 2 * WIDTH + 2 * D_MODEL
N_GROUPS = 4
PER_GROUP = 8
N_EXPERTS = N_GROUPS * PER_GROUP
TOP_K = 2
N_ASSIGN = N_TOK * TOP_K
D_EXPERT = D_MODEL // 4
EPS = 1e-6

LANES = 128
SUBLANES = 8
VMEM_LIMIT = 56 * 1024 * 1024

COL_C, COL_B, COL_XA = 0, WIDTH, 2 * WIDTH
COL_U, COL_V = 3 * WIDTH, 4 * WIDTH
COL_GA, COL_GB = 5 * WIDTH, 5 * WIDTH + D_MODEL

TM = 512
TN = 512
TILES_PER_SEQ = SEQ // TM
J1 = D_MODEL // TN
J2 = D_MODEL // TN
J_OUT = J1 + J2

PACKED = D_MODEL // 2
assert (J2 // 2) * TN == PACKED
TB = 256
P_SLOTS = N_ASSIGN + N_EXPERTS * TB
N_BLOCKS = P_SLOTS // TB
BLOCKS_PER_STEP = 4
GATHER_AHEAD = 2
assert N_BLOCKS % BLOCKS_PER_STEP == 0 and GATHER_AHEAD < BLOCKS_PER_STEP
INVERT_UNROLL = 16
TC = 512
N_CTILES = N_TOK // TC
assert N_CTILES % 2 == 0


def _dot(a, b):
    return jnp.dot(a, b, preferred_element_type=F32)


def _pack_halves(lo, hi):
    lo_bits = pltpu.bitcast(lo.astype(BF16).astype(F32), jnp.uint32)
    hi_bits = pltpu.bitcast(hi.astype(BF16).astype(F32), jnp.uint32)
    return lax.shift_right_logical(lo_bits, jnp.uint32(16)) | (hi_bits & jnp.uint32(0xFFFF0000))


def _unpack_halves(words):
    lo = pltpu.bitcast(lax.shift_left(words, jnp.uint32(16)), F32)
    hi = pltpu.bitcast(words & jnp.uint32(0xFFFF0000), F32)
    return lo, hi


def _mixer_in_body(tn, fused, *refs):
    if fused:
        (dest_ref, h1_ref, rw_ref, y_ref, g1_ref, wc_ref, wb_ref, wa_ref, wu_ref, wv_ref, cw_ref, lng_ref,
         lnb_ref, sw_ref, sb_ref, ya_ref, yb_ref, h_ref, xn_scr, xc_scr, carry_scr, u_scr, v_scr,
         fa, fb, fsem) = refs
    else:
        (h_ref, g1_ref, wc_ref, wb_ref, wa_ref, wu_ref, wv_ref, cw_ref, lng_ref, lnb_ref, sw_ref, sb_ref,
         ya_ref, yb_ref, xn_scr, xc_scr, carry_scr, u_scr, v_scr) = refs
    jc = WIDTH // tn
    jz = WIDTH // tn
    n_tiles = N_TOK // TM
    rows_per_step = TM // (jc + jz)
    i = pl.program_id(0)
    j = pl.program_id(1)

    def fetch(tile, r, k):
        src = y_ref.at[pl.ds(dest_ref[(tile * TM + r) * TOP_K + k], 1)]
        return pltpu.make_async_copy(src, (fa, fb)[k].at[pl.ds(r, 1)], fsem)

    def wait_fetch():
        for dst in (fa, fb):
            pltpu.make_async_copy(y_ref.at[pl.ds(0, TM)], dst, fsem).wait()

    def issue_fetch(step):
        if fused:
            nxt = jnp.where(i + 1 == n_tiles, 0, i + 1)
            for r in range(step * rows_per_step, (step + 1) * rows_per_step):
                fetch(nxt, r, 0).start()
                fetch(nxt, r, 1).start(priority=1)

    if fused:
        @pl.when((i == 0) & (j == 0))
        def _():
            def first_rows(r, carry):
                fetch(0, r, 0).start()
                fetch(0, r, 1).start()
                return carry
            lax.fori_loop(0, TM, first_rows, 0)

    @pl.when(j == 0)
    def _():
        if fused:
            wait_fetch()
            w = rw_ref[...]
            w1, w2 = w[:, 0:1], w[:, 1:2]
            a_lo, a_hi = _unpack_halves(fa[...])
            b_lo, b_hi = _unpack_halves(fb[...])
            halves = (h1_ref[:, 0:PACKED] + (w1 * a_lo + w2 * b_lo),
                      h1_ref[:, PACKED:D_MODEL] + (w1 * a_hi + w2 * b_hi))
            ms = sum(jnp.sum(v * v, axis=-1, keepdims=True) for v in halves) / D_MODEL
            scale = lax.rsqrt(ms + EPS)
            for c, v in enumerate(halves):
                cols = slice(c * PACKED, (c + 1) * PACKED)
                h_ref[:, cols] = v
                xn_scr[:, cols] = ((v * scale) * g1_ref[:, cols]).astype(BF16)
        else:
            x = h_ref[...]
            ms = jnp.mean(x * x, axis=-1, keepdims=True)
            xn_scr[...] = ((x * lax.rsqrt(ms + EPS)) * g1_ref[...]).astype(BF16)

    def conv_step(jv):
        issue_fetch(jv)
        xn = xn_scr[...]
        xc = _dot(xn, wc_ref[...]) * _dot(xn, wa_ref[...])

        @pl.when(i % TILES_PER_SEQ == 0)
        def _():
            xc_scr[0:SUBLANES, :] = jnp.zeros((SUBLANES, tn), F32)

        @pl.when(i % TILES_PER_SEQ != 0)
        def _():
            xc_scr[0:SUBLANES, :] = carry_scr[jv]

        xc_scr[SUBLANES:SUBLANES + TM, :] = xc
        x1 = xc_scr[SUBLANES - 1:SUBLANES - 1 + TM, :]
        x2 = xc_scr[SUBLANES - 2:SUBLANES - 2 + TM, :]
        cw = cw_ref[...]
        conv = cw[0:1, :] * x2 + cw[1:2, :] * x1 + cw[2:3, :] * xc
        carry_scr[jv] = xc_scr[TM:TM + SUBLANES, :]
        ya_ref[...] = (_dot(xn, wb_ref[...]) * conv).astype(BF16)

    def gating_step(jv):
        issue_fetch(jc + jv)
        xn = xn_scr[...]
        u_scr[jv] = jax.nn.gelu(_dot(xn, wu_ref[...]))
        v_scr[jv] = jax.nn.gelu(_dot(xn, wv_ref[...]))

    for jv in range(jc):
        pl.when(j == jv)(functools.partial(conv_step, jv))
    for jv in range(jz):
        pl.when(j == jc + jv)(functools.partial(gating_step, jv))

    @pl.when(j == jc + jz - 1)
    def _():
        s1 = jnp.zeros((TM, 1), F32)
        for k in range(jz):
            s1 = s1 + jnp.sum(v_scr[k], axis=-1, keepdims=True)
        mu = s1 / WIDTH
        s2 = jnp.zeros((TM, 1), F32)
        for k in range(jz):
            d = v_scr[k] - mu
            s2 = s2 + jnp.sum(d * d, axis=-1, keepdims=True)
        rstd = lax.rsqrt(s2 / WIDTH + EPS)
        row = lax.broadcasted_iota(jnp.int32, (CHUNK, CHUNK), 0)
        col = lax.broadcasted_iota(jnp.int32, (CHUNK, CHUNK), 1)
        causal = col <= row
        gpt = tn // GROUP_DIM
        for k in range(jz):
            vn = (((v_scr[k] - mu) * rstd) * lng_ref[:, k * tn:(k + 1) * tn]
                  + lnb_ref[:, k * tn:(k + 1) * tn]).astype(BF16)
            for gl in range(gpt):
                g = k * gpt + gl
                w = jnp.where(causal, sw_ref[g], 0.0).astype(BF16)
                bias = sb_ref[:, g * GROUP_DIM:(g + 1) * GROUP_DIM]
                for n in range(TM // CHUNK):
                    rows = slice(n * CHUNK, (n + 1) * CHUNK)
                    mixed = _dot(w, vn[rows, gl * GROUP_DIM:(gl + 1) * GROUP_DIM])
                    u = u_scr[k, rows, gl * GROUP_DIM:(gl + 1) * GROUP_DIM]
                    yb_ref[rows, g * GROUP_DIM:(g + 1) * GROUP_DIM] = (u * (mixed + bias)).astype(BF16)

    if fused:
        @pl.when((i == n_tiles - 1) & (j == jc + jz - 1))
        def _():
            wait_fetch()


def _mixer_in(l, tn, h, g1, w_in, conv_w, ln_g, ln_b, sgu_w, sgu_bias, moe=None):
    fused = moe is not None
    jc = jz = WIDTH // tn
    cj = lambda j: jnp.minimum(j, jc - 1)
    zj = lambda j: jnp.clip(j - jc, 0, jz - 1)
    wspec = lambda off, f: pl.BlockSpec((None, D_MODEL, tn), lambda i, j, *_: (l, 0, off // tn + f(j)))
    row_tile = pl.BlockSpec((TM, D_MODEL), lambda i, j, *_: (i, 0))
    in_specs = [
        pl.BlockSpec((None, 1, D_MODEL), lambda i, j, *_: (l, 0, 0)),
        wspec(COL_C, cj), wspec(COL_B, cj), wspec(COL_XA, cj), wspec(COL_U, zj), wspec(COL_V, zj),
        pl.BlockSpec((None, CONV_K, tn), lambda i, j, *_: (l, 0, cj(j))),
        pl.BlockSpec((None, 1, WIDTH), lambda i, j, *_: (l, 0, 0)),
        pl.BlockSpec((None, 1, WIDTH), lambda i, j, *_: (l, 0, 0)),
        pl.BlockSpec((None, SGU_GROUPS, CHUNK, CHUNK), lambda i, j, *_: (l, 0, 0, 0)),
        pl.BlockSpec((None, CHUNK, WIDTH), lambda i, j, *_: (l, 0, 0)),
    ]
    out_specs = [
        pl.BlockSpec((TM, tn), lambda i, j, *_: (i, cj(j))),
        pl.BlockSpec((TM, WIDTH), lambda i, j, *_: (i, 0)),
    ]
    out_shape = [jax.ShapeDtypeStruct((N_TOK, WIDTH), BF16), jax.ShapeDtypeStruct((N_TOK, WIDTH), BF16)]
    scratch = [
        pltpu.VMEM((TM, D_MODEL), BF16),
        pltpu.VMEM((TM + SUBLANES, tn), F32),
        pltpu.VMEM((jc, SUBLANES, tn), F32),
        pltpu.VMEM((jz, TM, tn), F32),
        pltpu.VMEM((jz, TM, tn), F32),
    ]
    weights = (g1, w_in, w_in, w_in, w_in, w_in, conv_w, ln_g, ln_b, sgu_w, sgu_bias)
    if fused:
        dest, h1, rw, y = moe
        in_specs = [row_tile, pl.BlockSpec((TM, LANES), lambda i, j, *_: (i, 0)),
                    pl.BlockSpec(memory_space=pl.ANY)] + in_specs
        out_specs.append(row_tile)
        out_shape.append(jax.ShapeDtypeStruct((N_TOK, D_MODEL), F32))
        scratch += [pltpu.VMEM((TM, PACKED), jnp.uint32), pltpu.VMEM((TM, PACKED), jnp.uint32),
                    pltpu.SemaphoreType.DMA]
        operands = (dest, h1, rw, y) + weights
    else:
        in_specs = [row_tile] + in_specs
        operands = (h,) + weights
    return pl.pallas_call(
        functools.partial(_mixer_in_body, tn, fused),
        grid_spec=pltpu.PrefetchScalarGridSpec(
            num_scalar_prefetch=1 if fused else 0,
            grid=(N_TOK // TM, jc + jz),
            in_specs=in_specs, out_specs=out_specs, scratch_shapes=scratch),
        out_shape=out_shape,
        compiler_params=pltpu.CompilerParams(
            dimension_semantics=("arbitrary", "arbitrary"), vmem_limit_bytes=VMEM_LIMIT),
        name="mixer_in",
    )(*operands)


def _mixer_out_body(h_ref, g1_ref, ya_ref, yb_ref, wga_ref, wgb_ref, bga_ref, bgb_ref, wba_ref, wbb_ref,
                    wo_ref, g2_ref, wr_ref, rb_ref,
                    h1_ref, xp_ref, ri_ref, rw_ref, cnt_ref,
                    xn_scr, mg_scr, out_scr, cnt_scr):
    i = pl.program_id(0)
    j = pl.program_id(1)

    @pl.when(j == 0)
    def _():
        x = h_ref[...]
        ms = jnp.mean(x * x, axis=-1, keepdims=True)
        xn_scr[...] = ((x * lax.rsqrt(ms + EPS)) * g1_ref[...]).astype(BF16)

    @pl.when((i == 0) & (j == 0))
    def _():
        cnt_scr[...] = jnp.zeros((1, LANES), F32)

    @pl.when(j < J1)
    def _():
        xn = xn_scr[...]
        ga = jax.nn.sigmoid(_dot(xn, wga_ref[...]) + bga_ref[...])
        gb = jax.nn.sigmoid(_dot(xn, wgb_ref[...]) + bgb_ref[...])
        mg_scr[j] = (ga * _dot(ya_ref[...], wba_ref[...]) + gb * _dot(yb_ref[...], wbb_ref[...])).astype(BF16)

    @pl.when(j >= J1)
    def _():
        acc = _dot(mg_scr[0], wo_ref[0:TN, :])
        for k in range(1, J1):
            acc = acc + _dot(mg_scr[k], wo_ref[k * TN:(k + 1) * TN, :])
        out_scr[j - J1] = acc

    @pl.when(j == J_OUT - 1)
    def _():
        ss = jnp.zeros((TM, 1), F32)
        for k in range(J2):
            cols = slice(k * TN, (k + 1) * TN)
            hk = h_ref[:, cols] + out_scr[k]
            h1_ref[:, cols] = hk
            ss = ss + jnp.sum(hk * hk, axis=-1, keepdims=True)
        rstd = lax.rsqrt(ss / D_MODEL + EPS)

        acc = jnp.zeros((TM, 2 * LANES), F32)
        for k in range(J2):
            cols = slice(k * TN, (k + 1) * TN)
            xk = (h1_ref[:, cols] * rstd) * g2_ref[:, cols]
            hi = xk.astype(BF16)
            lo = (xk - hi.astype(F32)).astype(BF16)
            acc = acc + (_dot(hi, wr_ref[cols, :]) + _dot(lo, wr_ref[cols, :]))
            bits = pltpu.bitcast(hi.astype(F32), jnp.uint32)
            if k < J2 // 2:
                xp_ref[:, cols] = lax.shift_right_logical(bits, jnp.uint32(16))
            else:
                pcols = slice(k * TN - PACKED, (k + 1) * TN - PACKED)
                xp_ref[:, pcols] = xp_ref[:, pcols] | (bits & jnp.uint32(0xFFFF0000))
        lg = (acc[:, 0:LANES] + acc[:, LANES:2 * LANES]) + rb_ref[...]

        lane_i = lax.broadcasted_iota(jnp.int32, (TM, LANES), 1)
        lane = lane_i.astype(F32)
        neg = -jnp.inf
        big = float(LANES)
        is_g = lane_i < N_GROUPS
        gl = jnp.where(is_g, lg, neg)
        gmax = jnp.max(gl, axis=-1, keepdims=True)
        g_idx = jnp.min(jnp.where(gl == gmax, lane, big), axis=-1, keepdims=True)
        g_w = 1.0 / jnp.sum(jnp.exp(gl - gmax), axis=-1, keepdims=True)

        first = N_GROUPS + g_idx * PER_GROUP
        in_grp = (lane >= first) & (lane < first + PER_GROUP)
        el = jnp.where(in_grp, lg, neg)
        m1 = jnp.max(el, axis=-1, keepdims=True)
        i1 = jnp.min(jnp.where(in_grp & (el == m1), lane, big), axis=-1, keepdims=True)
        el2 = jnp.where(lane == i1, neg, el)
        m2 = jnp.max(el2, axis=-1, keepdims=True)
        i2 = jnp.min(jnp.where(in_grp & (lane != i1) & (el2 == m2), lane, big), axis=-1, keepdims=True)
        t = jnp.exp(m2 - m1)
        w1 = g_w * (1.0 / (1.0 + t))
        w2 = g_w * (t / (1.0 + t))

        onehot = jnp.where((lane == i1) | (lane == i2), 1.0, 0.0)
        r_i = lax.broadcasted_iota(jnp.int32, (TM, TM), 0)
        c_i = lax.broadcasted_iota(jnp.int32, (TM, TM), 1)
        tri = jnp.where(c_i < r_i, 1.0, 0.0).astype(BF16)
        before = cnt_scr[...] + _dot(tri, onehot.astype(BF16))
        rank1 = jnp.sum(jnp.where(lane == i1, before, 0.0), axis=-1, keepdims=True)
        rank2 = jnp.sum(jnp.where(lane == i2, before, 0.0), axis=-1, keepdims=True)
        cnt_scr[...] = cnt_scr[...] + jnp.sum(onehot, axis=0, keepdims=True)

        ri = jnp.where(lane_i == 0, i1 - N_GROUPS,
                       jnp.where(lane_i == 1, i2 - N_GROUPS,
                                 jnp.where(lane_i == 2, rank1, jnp.where(lane_i == 3, rank2, 0.0))))
        ri_ref[...] = ri.astype(jnp.int32)
        rw_ref[...] = jnp.where(lane_i == 0, w1, jnp.where(lane_i == 1, w2, 0.0))
        cnt_ref[...] = jnp.broadcast_to(cnt_scr[...], (SUBLANES, LANES))


def _mixer_out(l, h, g1, ya, yb, w_in, b_gate, w_branch, w_out, g2, w_router, r_bias):
    mj = lambda j: jnp.minimum(j, J1 - 1)
    oj = lambda j: jnp.clip(j - J1, 0, J2 - 1)
    const = lambda *blk: pl.BlockSpec((None,) + blk, lambda i, j: (l,) + (0,) * len(blk))
    return pl.pallas_call(
        _mixer_out_body,
        grid=(N_TOK // TM, J_OUT),
        in_specs=[
            pl.BlockSpec((TM, D_MODEL), lambda i, j: (i, 0)),
            const(1, D_MODEL),
            pl.BlockSpec((TM, WIDTH), lambda i, j: (i, 0)),
            pl.BlockSpec((TM, WIDTH), lambda i, j: (i, 0)),
            pl.BlockSpec((None, D_MODEL, TN), lambda i, j: (l, 0, COL_GA // TN + mj(j))),
            pl.BlockSpec((None, D_MODEL, TN), lambda i, j: (l, 0, COL_GB // TN + mj(j))),
            pl.BlockSpec((None, 1, TN), lambda i, j: (l, 0, mj(j))),
            pl.BlockSpec((None, 1, TN), lambda i, j: (l, 0, D_MODEL // TN + mj(j))),
            pl.BlockSpec((None, None, WIDTH, TN), lambda i, j: (l, 0, 0, mj(j))),
            pl.BlockSpec((None, None, WIDTH, TN), lambda i, j: (l, 1, 0, mj(j))),
            pl.BlockSpec((None, D_MODEL, TN), lambda i, j: (l, 0, oj(j))),
            const(1, D_MODEL),
            const(D_MODEL, 2 * LANES), const(1, LANES),
        ],
        out_specs=[
            pl.BlockSpec((TM, D_MODEL), lambda i, j: (i, 0)),
            pl.BlockSpec((TM, PACKED), lambda i, j: (i, 0)),
            pl.BlockSpec((TM, LANES), lambda i, j: (i, 0)),
            pl.BlockSpec((TM, LANES), lambda i, j: (i, 0)),
            pl.BlockSpec((SUBLANES, LANES), lambda i, j: (0, 0)),
        ],
        out_shape=[jax.ShapeDtypeStruct((N_TOK, D_MODEL), F32),
                   jax.ShapeDtypeStruct((N_TOK, PACKED), jnp.uint32),
                   jax.ShapeDtypeStruct((N_TOK, LANES), jnp.int32),
                   jax.ShapeDtypeStruct((N_TOK, LANES), F32),
                   jax.ShapeDtypeStruct((SUBLANES, LANES), F32)],
        scratch_shapes=[
            pltpu.VMEM((TM, D_MODEL), BF16),
            pltpu.VMEM((J1, TM, TN), BF16),
            pltpu.VMEM((J2, TM, TN), F32),
            pltpu.VMEM((1, LANES), F32),
        ],
        compiler_params=pltpu.CompilerParams(
            dimension_semantics=("arbitrary", "arbitrary"), vmem_limit_bytes=VMEM_LIMIT),
        name="mixer_out",
    )(h, g1, ya, yb, w_in, w_in, b_gate, b_gate, w_branch, w_branch, w_out, g2, w_router, r_bias)


def _invert_body(first_ref, second_ref, spare_ref, inv_ref, sem):
    fill = pltpu.make_async_copy(spare_ref, inv_ref, sem)
    fill.start()
    fill.wait()

    def place(c, carry):
        t0 = c * INVERT_UNROLL
        for u in range(INVERT_UNROLL):
            inv_ref[first_ref[t0 + u]] = t0 + u
            inv_ref[second_ref[t0 + u]] = t0 + u
        return carry

    lax.fori_loop(0, N_TOK // INVERT_UNROLL, place, 0)


def _invert(dest):
    spare = jnp.arange(P_SLOTS, dtype=jnp.int32) % N_TOK
    smem = pl.BlockSpec(memory_space=pltpu.SMEM)
    return pl.pallas_call(
        _invert_body,
        in_specs=[smem, smem, pl.BlockSpec(memory_space=pl.ANY)],
        out_specs=pl.BlockSpec(memory_space=pltpu.SMEM),
        out_shape=jax.ShapeDtypeStruct((P_SLOTS,), jnp.int32),
        scratch_shapes=[pltpu.SemaphoreType.DMA],
        name="invert",
    )(dest[0::TOP_K], dest[1::TOP_K], spare)


def _experts_body(l, be_ref, nx_ref, na_ref, inv_ref, xp_ref, wg_ref, wu_ref, wd_ref, y_ref,
                  xbuf0, xbuf1, xbuf2, xbuf3, gsem0, gsem1, gsem2, gsem3,
                  wg_f, wu_f, wd_f, wsem, wslot, wg_s, wu_s, wd_s):
    n_act = na_ref[0]
    bufs = ((xbuf0, gsem0), (xbuf1, gsem1), (xbuf2, gsem2), (xbuf3, gsem3))

    def weight_copies(e, s):
        return [pltpu.make_async_copy(src.at[l, e], dst.at[s], wsem.at[s])
                for src, dst in ((wg_ref, wg_f), (wu_ref, wu_f), (wd_ref, wd_f))]

    def gather(blk, r, xdst, sem):
        return pltpu.make_async_copy(
            xp_ref.at[pl.ds(inv_ref[blk * TB + r], 1)], xdst.at[pl.ds(r, 1)], sem)

    def wait_gather(xdst, sem):
        pltpu.make_async_copy(xp_ref.at[pl.ds(0, TB)], xdst, sem).wait()

    def step(b, par):
        xcur, gcur = bufs[par]
        xnxt, gnxt = bufs[(par + GATHER_AHEAD) % BLOCKS_PER_STEP]
        wait_gather(xcur, gcur)

        @pl.when((b == 0) | (be_ref[b] != be_ref[jnp.maximum(b - 1, 0)]))
        def _():
            s = wslot[0]
            for c in weight_copies(be_ref[b], s):
                c.wait()
            wg_s[...] = wg_f[s].astype(BF16)
            wu_s[...] = wu_f[s].astype(BF16)
            wd_s[...] = wd_f[s].astype(BF16)

            @pl.when(nx_ref[b] >= 0)
            def _():
                for c in weight_copies(nx_ref[b], 1 - s):
                    c.start(priority=1)

            wslot[0] = 1 - s

        ahead = jnp.minimum(b + GATHER_AHEAD, N_BLOCKS - 1)
        for r in range(TB):
            gather(ahead, r, xnxt, gnxt).start()
        x_lo, x_hi = _unpack_halves(xcur[...])
        x_lo, x_hi = x_lo.astype(BF16), x_hi.astype(BF16)
        gate = _dot(x_lo, wg_s[0:PACKED, :]) + _dot(x_hi, wg_s[PACKED:D_MODEL, :])
        up = _dot(x_lo, wu_s[0:PACKED, :]) + _dot(x_hi, wu_s[PACKED:D_MODEL, :])
        hm = (jax.nn.silu(gate) * up).astype(BF16)
        y_ref[par * TB:(par + 1) * TB, :] = _pack_halves(
            _dot(hm, wd_s[:, 0:PACKED]), _dot(hm, wd_s[:, PACKED:D_MODEL]))

    @pl.when(pl.program_id(0) == 0)
    def _():
        def first_rows(r, carry):
            for k in range(GATHER_AHEAD):
                gather(k, r, *bufs[k]).start()
            return carry
        lax.fori_loop(0, TB, first_rows, 0)
        wslot[0] = 0
        for c in weight_copies(be_ref[0], 0):
            c.start(priority=1)

    for par in range(BLOCKS_PER_STEP):
        b = BLOCKS_PER_STEP * pl.program_id(0) + par
        pl.when(b < n_act)(functools.partial(step, b, par))

        @pl.when((b >= n_act) & (b < n_act + GATHER_AHEAD))
        def _():
            wait_gather(*bufs[par])

        @pl.when((b >= n_act) & (b < N_BLOCKS))
        def _():
            y_ref[par * TB:(par + 1) * TB, :] = jnp.zeros((TB, PACKED), jnp.uint32)


def _experts(l, block_e, next_e, n_active, inv, xp, w_gate, w_up, w_down):
    row_buf = pltpu.VMEM((TB, PACKED), jnp.uint32)
    in_out = (D_MODEL, D_EXPERT)
    out_in = (D_EXPERT, D_MODEL)
    n_steps = N_BLOCKS // BLOCKS_PER_STEP
    return pl.pallas_call(
        functools.partial(_experts_body, l),
        grid_spec=pltpu.PrefetchScalarGridSpec(
            num_scalar_prefetch=4,
            grid=(n_steps + 1,),
            in_specs=[
                pl.BlockSpec(memory_space=pl.ANY),
                pl.BlockSpec(memory_space=pl.ANY),
                pl.BlockSpec(memory_space=pl.ANY),
                pl.BlockSpec(memory_space=pl.ANY),
            ],
            out_specs=pl.BlockSpec((BLOCKS_PER_STEP * TB, PACKED),
                                   lambda s, be, nx, na, iv: (jnp.minimum(s, n_steps - 1), 0)),
            scratch_shapes=[
                row_buf, row_buf, row_buf, row_buf,
                pltpu.SemaphoreType.DMA, pltpu.SemaphoreType.DMA,
                pltpu.SemaphoreType.DMA, pltpu.SemaphoreType.DMA,
                pltpu.VMEM((2,) + in_out, F32), pltpu.VMEM((2,) + in_out, F32), pltpu.VMEM((2,) + out_in, F32),
                pltpu.SemaphoreType.DMA((2,)),
                pltpu.SMEM((1,), jnp.int32),
                pltpu.VMEM(in_out, BF16), pltpu.VMEM(in_out, BF16), pltpu.VMEM(out_in, BF16),
            ],
        ),
        out_shape=jax.ShapeDtypeStruct((P_SLOTS, PACKED), jnp.uint32),
        compiler_params=pltpu.CompilerParams(
            dimension_semantics=("arbitrary",), vmem_limit_bytes=VMEM_LIMIT),
        name="experts",
    )(block_e, next_e, n_active, inv, xp, w_gate, w_up, w_down)


def _combine_body(dest_ref, h1_ref, rw_ref, gf_ref, y_ref, out_ref,
                  ya0, yb0, ya1, yb1, sem0, sem1):
    i = pl.program_id(0)
    bufs = ((ya0, yb0, sem0), (ya1, yb1, sem1))

    def fetch(tile, r, k, dst, sem):
        slot = dest_ref[(tile * TC + r) * TOP_K + k]
        return pltpu.make_async_copy(y_ref.at[pl.ds(slot, 1)], dst.at[pl.ds(r, 1)], sem)

    def step(par, last):
        ya, yb, sem = bufs[par]
        for dst in (ya, yb):
            pltpu.make_async_copy(y_ref.at[pl.ds(0, TC)], dst, sem).wait()
        if not last:
            na, nb, nsem = bufs[1 - par]
            for r in range(TC):
                fetch(i + 1, r, 0, na, nsem).start()
                fetch(i + 1, r, 1, nb, nsem).start(priority=1)
        w = rw_ref[...]
        w1, w2 = w[:, 0:1], w[:, 1:2]
        a_lo, a_hi = _unpack_halves(ya[...])
        b_lo, b_hi = _unpack_halves(yb[...])
        lo = h1_ref[:, 0:PACKED] + (w1 * a_lo + w2 * b_lo)
        hi = h1_ref[:, PACKED:D_MODEL] + (w1 * a_hi + w2 * b_hi)
        ms = (jnp.sum(lo * lo, axis=-1, keepdims=True)
              + jnp.sum(hi * hi, axis=-1, keepdims=True)) / D_MODEL
        scale = lax.rsqrt(ms + EPS)
        out_ref[:, 0:PACKED] = (lo * scale) * gf_ref[:, 0:PACKED]
        out_ref[:, PACKED:D_MODEL] = (hi * scale) * gf_ref[:, PACKED:D_MODEL]

    @pl.when(i == 0)
    def _():
        def first_rows(r, carry):
            fetch(0, r, 0, ya0, sem0).start()
            fetch(0, r, 1, yb0, sem0).start()
            return carry
        lax.fori_loop(0, TC, first_rows, 0)

    @pl.when(i % 2 == 0)
    def _():
        step(0, last=False)

    @pl.when((i % 2 == 1) & (i < N_CTILES - 1))
    def _():
        step(1, last=False)

    @pl.when(i == N_CTILES - 1)
    def _():
        step(1, last=True)


def _combine(dest, h1, rw, gf, y):
    row_buf = pltpu.VMEM((TC, PACKED), jnp.uint32)
    return pl.pallas_call(
        _combine_body,
        grid_spec=pltpu.PrefetchScalarGridSpec(
            num_scalar_prefetch=1,
            grid=(N_CTILES,),
            in_specs=[
                pl.BlockSpec((TC, D_MODEL), lambda i, d: (i, 0)),
                pl.BlockSpec((TC, LANES), lambda i, d: (i, 0)),
                pl.BlockSpec((1, D_MODEL), lambda i, d: (0, 0)),
                pl.BlockSpec(memory_space=pl.ANY),
            ],
            out_specs=pl.BlockSpec((TC, D_MODEL), lambda i, d: (i, 0)),
            scratch_shapes=[row_buf, row_buf, row_buf, row_buf,
                            pltpu.SemaphoreType.DMA, pltpu.SemaphoreType.DMA],
        ),
        out_shape=jax.ShapeDtypeStruct((N_TOK, D_MODEL), F32),
        compiler_params=pltpu.CompilerParams(
            dimension_semantics=("arbitrary",), vmem_limit_bytes=VMEM_LIMIT),
        name="combine",
    )(dest, h1, rw, gf, y)


def _route_tables(cnt, ri):
    counts = cnt[0, N_GROUPS:N_GROUPS + N_EXPERTS].astype(jnp.int32)
    padded = (counts + TB - 1) // TB * TB
    pad_end = jnp.cumsum(padded)
    pad_start = pad_end - padded
    experts = jnp.arange(N_EXPERTS, dtype=jnp.int32)
    start_of = jnp.sum(jnp.where(ri[:, 0:TOP_K, None] == experts, pad_start, 0), axis=-1)
    dest = (start_of + ri[:, TOP_K:2 * TOP_K]).reshape(N_ASSIGN)
    n_active = (pad_end[-1:] // TB).astype(jnp.int32)
    block_row = jnp.arange(N_BLOCKS, dtype=jnp.int32)[:, None] * TB
    block_e = jnp.minimum(jnp.sum((pad_end[None, :] <= block_row).astype(jnp.int32), axis=1), N_EXPERTS - 1)
    run_end = jnp.sum(jnp.where(block_e[:, None] == experts, pad_end, 0), axis=-1) // TB
    next_e = jnp.where(run_end < n_active[0], block_e[jnp.minimum(run_end, N_BLOCKS - 1)], -1)
    return dest, block_e, next_e.astype(jnp.int32), n_active


def kernel(x, norm1_g, w_in, b_gate, conv_w, ln_v_g, ln_v_b, sgu_w, sgu_b, w_branch, w_out, norm2_g,
           router_g, router_g_b, router_e, router_e_b, w_gate, w_up, w_down, final_g):
    h = x.reshape(N_TOK, D_MODEL)

    w_in_b = w_in.astype(BF16)
    w_branch_b = w_branch.astype(BF16)
    w_out_b = w_out.astype(BF16)
    g1 = norm1_g.reshape(DEPTH, 1, D_MODEL)
    g2 = norm2_g.reshape(DEPTH, 1, D_MODEL)
    gf = final_g.reshape(1, D_MODEL)
    ln_g = ln_v_g.reshape(DEPTH, 1, WIDTH)
    ln_b = ln_v_b.reshape(DEPTH, 1, WIDTH)
    bg = b_gate.reshape(DEPTH, 1, 2 * D_MODEL)
    sgu_bias = jnp.repeat(jnp.swapaxes(sgu_b, 1, 2), GROUP_DIM, axis=2)
    pad = LANES - N_GROUPS - N_EXPERTS
    w_r = jnp.concatenate([router_g, router_e, jnp.zeros((DEPTH, D_MODEL, pad), F32)], axis=2)
    wr_hi = w_r.astype(BF16)
    w_router = jnp.concatenate([wr_hi, (w_r - wr_hi.astype(F32)).astype(BF16)], axis=2)
    r_bias = jnp.concatenate([router_g_b, router_e_b, jnp.zeros((DEPTH, pad), F32)], axis=1)
    r_bias = r_bias.reshape(DEPTH, 1, LANES)

    moe = None
    for l in range(DEPTH):
        if moe is None:
            ya, yb = _mixer_in(l, TN, h, g1, w_in_b, conv_w, ln_g, ln_b, sgu_w, sgu_bias)
        else:
            ya, yb, h = _mixer_in(l, TN, None, g1, w_in_b, conv_w, ln_g, ln_b, sgu_w, sgu_bias, moe)
        h1, xp, ri, rw, cnt = _mixer_out(l, h, g1, ya, yb, w_in_b, bg, w_branch_b, w_out_b, g2,
                                         w_router, r_bias)
        dest, block_e, next_e, n_active = _route_tables(cnt, ri)
        y = _experts(l, block_e, next_e, n_active, _invert(dest), xp, w_gate, w_up, w_down)
        moe = (dest, h1, rw, y)
    return _combine(*moe[:3], gf, moe[3]).reshape(BATCH, SEQ, D_MODEL)
```

```python
import functools

import jax
import jax.numpy as jnp
from jax import lax
from jax.experimental import pallas as pl
from jax.experimental.pallas import tpu as pltpu

F32 = jnp.float32
BF16 = jnp.bfloat16

D_MODEL = 2048
BATCH = 4
SEQ = 4096
DEPTH = 2
N_TOK = BATCH * SEQ
WIDTH = D_MODEL // 2
CONV_K = 3
SGU_GROUPS = 8
CHUNK = 128
GROUP_DIM = WIDTH // SGU_GROUPS
N_GROUPS = 4
PER_GROUP = 8
N_EXPERTS = N_GROUPS * PER_GROUP
TOP_K = 2
N_ASSIGN = N_TOK * TOP_K
D_EXPERT = D_MODEL // 4
EPS = 1e-6

LANES = 128
SUBLANES = 8
VMEM_LIMIT = 56 * 1024 * 1024

COL_C, COL_B, COL_XA = 0, WIDTH, 2 * WIDTH
COL_U, COL_V = 3 * WIDTH, 4 * WIDTH
COL_GA, COL_GB = 5 * WIDTH, 5 * WIDTH + D_MODEL

TM = 512
TN = 512
TILES_PER_SEQ = SEQ // TM
J1 = D_MODEL // TN
J2 = D_MODEL // TN
J_OUT = J1 + J2

PACKED = D_MODEL // 2
assert (J2 // 2) * TN == PACKED
TB = 256
P_SLOTS = N_ASSIGN + N_EXPERTS * TB
N_BLOCKS = P_SLOTS // TB
BLOCKS_PER_STEP = 4
GATHER_AHEAD = 2
assert N_BLOCKS % BLOCKS_PER_STEP == 0 and GATHER_AHEAD < BLOCKS_PER_STEP
INVERT_UNROLL = 16
TC = 512
N_CTILES = N_TOK // TC
assert N_CTILES % 2 == 0


def _dot(a, b):
    return jnp.dot(a, b, preferred_element_type=F32)


def _pack_halves(lo, hi):
    lo_bits = pltpu.bitcast(lo.astype(BF16).astype(F32), jnp.uint32)
    hi_bits = pltpu.bitcast(hi.astype(BF16).astype(F32), jnp.uint32)
    return lax.shift_right_logical(lo_bits, jnp.uint32(16)) | (hi_bits & jnp.uint32(0xFFFF0000))


def _unpack_halves(words):
    lo = pltpu.bitcast(lax.shift_left(words, jnp.uint32(16)), F32)
    hi = pltpu.bitcast(words & jnp.uint32(0xFFFF0000), F32)
    return lo, hi


def _mixer_in_body(tn, fused, *refs):
    if fused:
        (dest_ref, h1_ref, rw_ref, y_ref, g1_ref, wc_ref, wb_ref, wa_ref, wu_ref, wv_ref, cw_ref, lng_ref,
         lnb_ref, sw_ref, sb_ref, ya_ref, yb_ref, h_ref, xn_scr, xc_scr, carry_scr, u_scr, v_scr,
         fa, fb, fsem) = refs
    else:
        (h_ref, g1_ref, wc_ref, wb_ref, wa_ref, wu_ref, wv_ref, cw_ref, lng_ref, lnb_ref, sw_ref, sb_ref,
         ya_ref, yb_ref, xn_scr, xc_scr, carry_scr, u_scr, v_scr) = refs
    jc = WIDTH // tn
    jz = WIDTH // tn
    n_tiles = N_TOK // TM
    rows_per_step = TM
    i = pl.program_id(0)
    j = pl.program_id(1)

    def fetch(tile, r, k):
        src = y_ref.at[pl.ds(dest_ref[(tile * TM + r) * TOP_K + k], 1)]
        return pltpu.make_async_copy(src, (fa, fb)[k].at[pl.ds(r, 1)], fsem)

    def wait_fetch():
        for dst in (fa, fb):
            pltpu.make_async_copy(y_ref.at[pl.ds(0, TM)], dst, fsem).wait()

    def issue_fetch(step):
        if fused:
            nxt = jnp.where(i + 1 == n_tiles, 0, i + 1)
            for r in range(step * rows_per_step, (step + 1) * rows_per_step):
                fetch(nxt, r, 0).start()
                fetch(nxt, r, 1).start(priority=1)

    if fused:
        @pl.when((i == 0) & (j == 0))
        def _():
            def first_rows(r, carry):
                fetch(0, r, 0).start()
                fetch(0, r, 1).start()
                return carry
            lax.fori_loop(0, TM, first_rows, 0)

    @pl.when(j == 0)
    def _():
        if fused:
            wait_fetch()
            w = rw_ref[...]
            w1, w2 = w[:, 0:1], w[:, 1:2]
            a_lo, a_hi = _unpack_halves(fa[...])
            b_lo, b_hi = _unpack_halves(fb[...])
            halves = (h1_ref[:, 0:PACKED] + (w1 * a_lo + w2 * b_lo),
                      h1_ref[:, PACKED:D_MODEL] + (w1 * a_hi + w2 * b_hi))
            ms = sum(jnp.sum(v * v, axis=-1, keepdims=True) for v in halves) / D_MODEL
            scale = lax.rsqrt(ms + EPS)
            for c, v in enumerate(halves):
                cols = slice(c * PACKED, (c + 1) * PACKED)
                h_ref[:, cols] = v
                xn_scr[:, cols] = ((v * scale) * g1_ref[:, cols]).astype(BF16)
        else:
            x = h_ref[...]
            ms = jnp.mean(x * x, axis=-1, keepdims=True)
            xn_scr[...] = ((x * lax.rsqrt(ms + EPS)) * g1_ref[...]).astype(BF16)

    def conv_step(jv):
        if jv == 0:
            issue_fetch(jv)
        xn = xn_scr[...]
        xc = _dot(xn, wc_ref[...]) * _dot(xn, wa_ref[...])

        @pl.when(i % TILES_PER_SEQ == 0)
        def _():
            xc_scr[0:SUBLANES, :] = jnp.zeros((SUBLANES, tn), F32)

        @pl.when(i % TILES_PER_SEQ != 0)
        def _():
            xc_scr[0:SUBLANES, :] = carry_scr[jv]

        xc_scr[SUBLANES:SUBLANES + TM, :] = xc
        x1 = xc_scr[SUBLANES - 1:SUBLANES - 1 + TM, :]
        x2 = xc_scr[SUBLANES - 2:SUBLANES - 2 + TM, :]
        cw = cw_ref[...]
        conv = cw[0:1, :] * x2 + cw[1:2, :] * x1 + cw[2:3, :] * xc
        carry_scr[jv] = xc_scr[TM:TM + SUBLANES, :]
        ya_ref[...] = (_dot(xn, wb_ref[...]) * conv).astype(BF16)

    def gating_step(jv):
        xn = xn_scr[...]
        u_scr[jv] = jax.nn.gelu(_dot(xn, wu_ref[...]))
        v_scr[jv] = jax.nn.gelu(_dot(xn, wv_ref[...]))

    for jv in range(jc):
        pl.when(j == jv)(functools.partial(conv_step, jv))
    for jv in range(jz):
        pl.when(j == jc + jv)(functools.partial(gating_step, jv))

    @pl.when(j == jc + jz - 1)
    def _():
        s1 = jnp.zeros((TM, 1), F32)
        for k in range(jz):
            s1 = s1 + jnp.sum(v_scr[k], axis=-1, keepdims=True)
        mu = s1 / WIDTH
        s2 = jnp.zeros((TM, 1), F32)
        for k in range(jz):
            d = v_scr[k] - mu
            s2 = s2 + jnp.sum(d * d, axis=-1, keepdims=True)
        rstd = lax.rsqrt(s2 / WIDTH + EPS)
        row = lax.broadcasted_iota(jnp.int32, (CHUNK, CHUNK), 0)
        col = lax.broadcasted_iota(jnp.int32, (CHUNK, CHUNK), 1)
        causal = col <= row
        gpt = tn // GROUP_DIM
        for k in range(jz):
            vn = (((v_scr[k] - mu) * rstd) * lng_ref[:, k * tn:(k + 1) * tn]
                  + lnb_ref[:, k * tn:(k + 1) * tn]).astype(BF16)
            for gl in range(gpt):
                g = k * gpt + gl
                w = jnp.where(causal, sw_ref[g], 0.0).astype(BF16)
                bias = sb_ref[:, g * GROUP_DIM:(g + 1) * GROUP_DIM]
                for n in range(TM // CHUNK):
                    rows = slice(n * CHUNK, (n + 1) * CHUNK)
                    mixed = _dot(w, vn[rows, gl * GROUP_DIM:(gl + 1) * GROUP_DIM])
                    u = u_scr[k, rows, gl * GROUP_DIM:(gl + 1) * GROUP_DIM]
                    yb_ref[rows, g * GROUP_DIM:(g + 1) * GROUP_DIM] = (u * (mixed + bias)).astype(BF16)

    if fused:
        @pl.when((i == n_tiles - 1) & (j == jc + jz - 1))
        def _():
            wait_fetch()


def _mixer_in(l, tn, h, g1, w_in, conv_w, ln_g, ln_b, sgu_w, sgu_bias, moe=None):
    fused = moe is not None
    jc = jz = WIDTH // tn
    cj = lambda j: jnp.minimum(j, jc - 1)
    zj = lambda j: jnp.clip(j - jc, 0, jz - 1)
    wspec = lambda off, f: pl.BlockSpec((None, D_MODEL, tn), lambda i, j, *_: (l, 0, off // tn + f(j)))
    row_tile = pl.BlockSpec((TM, D_MODEL), lambda i, j, *_: (i, 0))
    in_specs = [
        pl.BlockSpec((None, 1, D_MODEL), lambda i, j, *_: (l, 0, 0)),
        wspec(COL_C, cj), wspec(COL_B, cj), wspec(COL_XA, cj), wspec(COL_U, zj), wspec(COL_V, zj),
        pl.BlockSpec((None, CONV_K, tn), lambda i, j, *_: (l, 0, cj(j))),
        pl.BlockSpec((None, 1, WIDTH), lambda i, j, *_: (l, 0, 0)),
        pl.BlockSpec((None, 1, WIDTH), lambda i, j, *_: (l, 0, 0)),
        pl.BlockSpec((None, SGU_GROUPS, CHUNK, CHUNK), lambda i, j, *_: (l, 0, 0, 0)),
        pl.BlockSpec((None, CHUNK, WIDTH), lambda i, j, *_: (l, 0, 0)),
    ]
    out_specs = [
        pl.BlockSpec((TM, tn), lambda i, j, *_: (i, cj(j))),
        pl.BlockSpec((TM, WIDTH), lambda i, j, *_: (i, 0)),
    ]
    out_shape = [jax.ShapeDtypeStruct((N_TOK, WIDTH), BF16), jax.ShapeDtypeStruct((N_TOK, WIDTH), BF16)]
    scratch = [
        pltpu.VMEM((TM, D_MODEL), BF16),
        pltpu.VMEM((TM + SUBLANES, tn), F32),
        pltpu.VMEM((jc, SUBLANES, tn), F32),
        pltpu.VMEM((jz, TM, tn), F32),
        pltpu.VMEM((jz, TM, tn), F32),
    ]
    weights = (g1, w_in, w_in, w_in, w_in, w_in, conv_w, ln_g, ln_b, sgu_w, sgu_bias)
    if fused:
        dest, h1, rw, y = moe
        in_specs = [row_tile, pl.BlockSpec((TM, LANES), lambda i, j, *_: (i, 0)),
                    pl.BlockSpec(memory_space=pl.ANY)] + in_specs
        out_specs.append(row_tile)
        out_shape.append(jax.ShapeDtypeStruct((N_TOK, D_MODEL), F32))
        scratch += [pltpu.VMEM((TM, PACKED), jnp.uint32), pltpu.VMEM((TM, PACKED), jnp.uint32),
                    pltpu.SemaphoreType.DMA]
        operands = (dest, h1, rw, y) + weights
    else:
        in_specs = [row_tile] + in_specs
        operands = (h,) + weights
    return pl.pallas_call(
        functools.partial(_mixer_in_body, tn, fused),
        grid_spec=pltpu.PrefetchScalarGridSpec(
            num_scalar_prefetch=1 if fused else 0,
            grid=(N_TOK // TM, jc + jz),
            in_specs=in_specs, out_specs=out_specs, scratch_shapes=scratch),
        out_shape=out_shape,
        compiler_params=pltpu.CompilerParams(
            dimension_semantics=("arbitrary", "arbitrary"), vmem_limit_bytes=VMEM_LIMIT),
        name="mixer_in",
    )(*operands)


def _mixer_out_body(h_ref, g1_ref, ya_ref, yb_ref, wga_ref, wgb_ref, bga_ref, bgb_ref, wba_ref, wbb_ref,
                    wo_ref, g2_ref, wr_ref, rb_ref,
                    h1_ref, xp_ref, ri_ref, rw_ref, cnt_ref,
                    xn_scr, mg_scr, out_scr, cnt_scr):
    i = pl.program_id(0)
    j = pl.program_id(1)

    @pl.when(j == 0)
    def _():
        x = h_ref[...]
        ms = jnp.mean(x * x, axis=-1, keepdims=True)
        xn_scr[...] = ((x * lax.rsqrt(ms + EPS)) * g1_ref[...]).astype(BF16)

    @pl.when((i == 0) & (j == 0))
    def _():
        cnt_scr[...] = jnp.zeros((1, LANES), F32)

    @pl.when(j < J1)
    def _():
        xn = xn_scr[...]
        ga = jax.nn.sigmoid(_dot(xn, wga_ref[...]) + bga_ref[...])
        gb = jax.nn.sigmoid(_dot(xn, wgb_ref[...]) + bgb_ref[...])
        mg_scr[j] = (ga * _dot(ya_ref[...], wba_ref[...]) + gb * _dot(yb_ref[...], wbb_ref[...])).astype(BF16)

    @pl.when(j >= J1)
    def _():
        acc = _dot(mg_scr[0], wo_ref[0:TN, :])
        for k in range(1, J1):
            acc = acc + _dot(mg_scr[k], wo_ref[k * TN:(k + 1) * TN, :])
        out_scr[j - J1] = acc

    @pl.when(j == J_OUT - 1)
    def _():
        ss = jnp.zeros((TM, 1), F32)
        for k in range(J2):
            cols = slice(k * TN, (k + 1) * TN)
            hk = h_ref[:, cols] + out_scr[k]
            h1_ref[:, cols] = hk
            ss = ss + jnp.sum(hk * hk, axis=-1, keepdims=True)
        rstd = lax.rsqrt(ss / D_MODEL + EPS)

        acc = jnp.zeros((TM, 2 * LANES), F32)
        for k in range(J2):
            cols = slice(k * TN, (k + 1) * TN)
            xk = (h1_ref[:, cols] * rstd) * g2_ref[:, cols]
            hi = xk.astype(BF16)
            lo = (xk - hi.astype(F32)).astype(BF16)
            acc = acc + (_dot(hi, wr_ref[cols, :]) + _dot(lo, wr_ref[cols, :]))
            bits = pltpu.bitcast(hi.astype(F32), jnp.uint32)
            if k < J2 // 2:
                xp_ref[:, cols] = lax.shift_right_logical(bits, jnp.uint32(16))
            else:
                pcols = slice(k * TN - PACKED, (k + 1) * TN - PACKED)
                xp_ref[:, pcols] = xp_ref[:, pcols] | (bits & jnp.uint32(0xFFFF0000))
        lg = (acc[:, 0:LANES] + acc[:, LANES:2 * LANES]) + rb_ref[...]

        lane_i = lax.broadcasted_iota(jnp.int32, (TM, LANES), 1)
        lane = lane_i.astype(F32)
        neg = -jnp.inf
        big = float(LANES)
        is_g = lane_i < N_GROUPS
        gl = jnp.where(is_g, lg, neg)
        gmax = jnp.max(gl, axis=-1, keepdims=True)
        g_idx = jnp.min(jnp.where(gl == gmax, lane, big), axis=-1, keepdims=True)
        g_w = 1.0 / jnp.sum(jnp.exp(gl - gmax), axis=-1, keepdims=True)

        first = N_GROUPS + g_idx * PER_GROUP
        in_grp = (lane >= first) & (lane < first + PER_GROUP)
        el = jnp.where(in_grp, lg, neg)
        m1 = jnp.max(el, axis=-1, keepdims=True)
        i1 = jnp.min(jnp.where(in_grp & (el == m1), lane, big), axis=-1, keepdims=True)
        el2 = jnp.where(lane == i1, neg, el)
        m2 = jnp.max(el2, axis=-1, keepdims=True)
        i2 = jnp.min(jnp.where(in_grp & (lane != i1) & (el2 == m2), lane, big), axis=-1, keepdims=True)
        t = jnp.exp(m2 - m1)
        w1 = g_w * (1.0 / (1.0 + t))
        w2 = g_w * (t / (1.0 + t))

        onehot = jnp.where((lane == i1) | (lane == i2), 1.0, 0.0)
        r_i = lax.broadcasted_iota(jnp.int32, (TM, TM), 0)
        c_i = lax.broadcasted_iota(jnp.int32, (TM, TM), 1)
        tri = jnp.where(c_i < r_i, 1.0, 0.0).astype(BF16)
        before = cnt_scr[...] + _dot(tri, onehot.astype(BF16))
        rank1 = jnp.sum(jnp.where(lane == i1, before, 0.0), axis=-1, keepdims=True)
        rank2 = jnp.sum(jnp.where(lane == i2, before, 0.0), axis=-1, keepdims=True)
        cnt_scr[...] = cnt_scr[...] + jnp.sum(onehot, axis=0, keepdims=True)

        ri = jnp.where(lane_i == 0, i1 - N_GROUPS,
                       jnp.where(lane_i == 1, i2 - N_GROUPS,
                                 jnp.where(lane_i == 2, rank1, jnp.where(lane_i == 3, rank2, 0.0))))
        ri_ref[...] = ri.astype(jnp.int32)
        rw_ref[...] = jnp.where(lane_i == 0, w1, jnp.where(lane_i == 1, w2, 0.0))
        cnt_ref[...] = jnp.broadcast_to(cnt_scr[...], (SUBLANES, LANES))


def _mixer_out(l, h, g1, ya, yb, w_in, b_gate, w_branch, w_out, g2, w_router, r_bias):
    mj = lambda j: jnp.minimum(j, J1 - 1)
    oj = lambda j: jnp.clip(j - J1, 0, J2 - 1)
    const = lambda *blk: pl.BlockSpec((None,) + blk, lambda i, j: (l,) + (0,) * len(blk))
    return pl.pallas_call(
        _mixer_out_body,
        grid=(N_TOK // TM, J_OUT),
        in_specs=[
            pl.BlockSpec((TM, D_MODEL), lambda i, j: (i, 0)),
            const(1, D_MODEL),
            pl.BlockSpec((TM, WIDTH), lambda i, j: (i, 0)),
            pl.BlockSpec((TM, WIDTH), lambda i, j: (i, 0)),
            pl.BlockSpec((None, D_MODEL, TN), lambda i, j: (l, 0, COL_GA // TN + mj(j))),
            pl.BlockSpec((None, D_MODEL, TN), lambda i, j: (l, 0, COL_GB // TN + mj(j))),
            pl.BlockSpec((None, 1, TN), lambda i, j: (l, 0, mj(j))),
            pl.BlockSpec((None, 1, TN), lambda i, j: (l, 0, D_MODEL // TN + mj(j))),
            pl.BlockSpec((None, None, WIDTH, TN), lambda i, j: (l, 0, 0, mj(j))),
            pl.BlockSpec((None, None, WIDTH, TN), lambda i, j: (l, 1, 0, mj(j))),
            pl.BlockSpec((None, D_MODEL, TN), lambda i, j: (l, 0, oj(j))),
            const(1, D_MODEL),
            const(D_MODEL, 2 * LANES), const(1, LANES),
        ],
        out_specs=[
            pl.BlockSpec((TM, D_MODEL), lambda i, j: (i, 0)),
            pl.BlockSpec((TM, PACKED), lambda i, j: (i, 0)),
            pl.BlockSpec((TM, LANES), lambda i, j: (i, 0)),
            pl.BlockSpec((TM, LANES), lambda i, j: (i, 0)),
            pl.BlockSpec((SUBLANES, LANES), lambda i, j: (0, 0)),
        ],
        out_shape=[jax.ShapeDtypeStruct((N_TOK, D_MODEL), F32),
                   jax.ShapeDtypeStruct((N_TOK, PACKED), jnp.uint32),
                   jax.ShapeDtypeStruct((N_TOK, LANES), jnp.int32),
                   jax.ShapeDtypeStruct((N_TOK, LANES), F32),
                   jax.ShapeDtypeStruct((SUBLANES, LANES), F32)],
        scratch_shapes=[
            pltpu.VMEM((TM, D_MODEL), BF16),
            pltpu.VMEM((J1, TM, TN), BF16),
            pltpu.VMEM((J2, TM, TN), F32),
            pltpu.VMEM((1, LANES), F32),
        ],
        compiler_params=pltpu.CompilerParams(
            dimension_semantics=("arbitrary", "arbitrary"), vmem_limit_bytes=VMEM_LIMIT),
        name="mixer_out",
    )(h, g1, ya, yb, w_in, w_in, b_gate, b_gate, w_branch, w_branch, w_out, g2, w_router, r_bias)


def _invert_body(first_ref, second_ref, spare_ref, inv_ref, sem):
    fill = pltpu.make_async_copy(spare_ref, inv_ref, sem)
    fill.start()
    fill.wait()

    def place(c, carry):
        t0 = c * INVERT_UNROLL
        for u in range(INVERT_UNROLL):
            inv_ref[first_ref[t0 + u]] = t0 + u
            inv_ref[second_ref[t0 + u]] = t0 + u
        return carry

    lax.fori_loop(0, N_TOK // INVERT_UNROLL, place, 0)


def _invert(dest):
    spare = jnp.arange(P_SLOTS, dtype=jnp.int32) % N_TOK
    smem = pl.BlockSpec(memory_space=pltpu.SMEM)
    return pl.pallas_call(
        _invert_body,
        in_specs=[smem, smem, pl.BlockSpec(memory_space=pl.ANY)],
        out_specs=pl.BlockSpec(memory_space=pltpu.SMEM),
        out_shape=jax.ShapeDtypeStruct((P_SLOTS,), jnp.int32),
        scratch_shapes=[pltpu.SemaphoreType.DMA],
        name="invert",
    )(dest[0::TOP_K], dest[1::TOP_K], spare)


def _experts_body(l, be_ref, nx_ref, na_ref, inv_ref, xp_ref, wg_ref, wu_ref, wd_ref, y_ref,
                  xbuf0, xbuf1, xbuf2, xbuf3, gsem0, gsem1, gsem2, gsem3,
                  wg_f, wu_f, wd_f, wsem, wslot, wg_s, wu_s, wd_s):
    n_act = na_ref[0]
    bufs = ((xbuf0, gsem0), (xbuf1, gsem1), (xbuf2, gsem2), (xbuf3, gsem3))

    def weight_copies(e, s):
        return [pltpu.make_async_copy(src.at[l, e], dst.at[s], wsem.at[s])
                for src, dst in ((wg_ref, wg_f), (wu_ref, wu_f), (wd_ref, wd_f))]

    def gather(blk, r, xdst, sem):
        return pltpu.make_async_copy(
            xp_ref.at[pl.ds(inv_ref[blk * TB + r], 1)], xdst.at[pl.ds(r, 1)], sem)

    def wait_gather(xdst, sem):
        pltpu.make_async_copy(xp_ref.at[pl.ds(0, TB)], xdst, sem).wait()

    def step(b, par):
        xcur, gcur = bufs[par]
        xnxt, gnxt = bufs[(par + GATHER_AHEAD) % BLOCKS_PER_STEP]
        wait_gather(xcur, gcur)

        @pl.when((b == 0) | (be_ref[b] != be_ref[jnp.maximum(b - 1, 0)]))
        def _():
            s = wslot[0]
            for c in weight_copies(be_ref[b], s):
                c.wait()
            wg_s[...] = wg_f[s].astype(BF16)
            wu_s[...] = wu_f[s].astype(BF16)
            wd_s[...] = wd_f[s].astype(BF16)

            @pl.when(nx_ref[b] >= 0)
            def _():
                for c in weight_copies(nx_ref[b], 1 - s):
                    c.start(priority=1)

            wslot[0] = 1 - s

        ahead = jnp.minimum(b + GATHER_AHEAD, N_BLOCKS - 1)
        for r in range(TB):
            gather(ahead, r, xnxt, gnxt).start()
        x_lo, x_hi = _unpack_halves(xcur[...])
        x_lo, x_hi = x_lo.astype(BF16), x_hi.astype(BF16)
        gate = _dot(x_lo, wg_s[0:PACKED, :]) + _dot(x_hi, wg_s[PACKED:D_MODEL, :])
        up = _dot(x_lo, wu_s[0:PACKED, :]) + _dot(x_hi, wu_s[PACKED:D_MODEL, :])
        hm = (jax.nn.silu(gate) * up).astype(BF16)
        y_ref[par * TB:(par + 1) * TB, :] = _pack_halves(
            _dot(hm, wd_s[:, 0:PACKED]), _dot(hm, wd_s[:, PACKED:D_MODEL]))

    @pl.when(pl.program_id(0) == 0)
    def _():
        def first_rows(r, carry):
            for k in range(GATHER_AHEAD):
                gather(k, r, *bufs[k]).start()
            return carry
        lax.fori_loop(0, TB, first_rows, 0)
        wslot[0] = 0
        for c in weight_copies(be_ref[0], 0):
            c.start(priority=1)

    for par in range(BLOCKS_PER_STEP):
        b = BLOCKS_PER_STEP * pl.program_id(0) + par
        pl.when(b < n_act)(functools.partial(step, b, par))

        @pl.when((b >= n_act) & (b < n_act + GATHER_AHEAD))
        def _():
            wait_gather(*bufs[par])

        @pl.when((b >= n_act) & (b < N_BLOCKS))
        def _():
            y_ref[par * TB:(par + 1) * TB, :] = jnp.zeros((TB, PACKED), jnp.uint32)


def _experts(l, block_e, next_e, n_active, inv, xp, w_gate, w_up, w_down):
    row_buf = pltpu.VMEM((TB, PACKED), jnp.uint32)
    in_out = (D_MODEL, D_EXPERT)
    out_in = (D_EXPERT, D_MODEL)
    n_steps = N_BLOCKS // BLOCKS_PER_STEP
    return pl.pallas_call(
        functools.partial(_experts_body, l),
        grid_spec=pltpu.PrefetchScalarGridSpec(
            num_scalar_prefetch=4,
            grid=(n_steps + 1,),
            in_specs=[
                pl.BlockSpec(memory_space=pl.ANY),
                pl.BlockSpec(memory_space=pl.ANY),
                pl.BlockSpec(memory_space=pl.ANY),
                pl.BlockSpec(memory_space=pl.ANY),
            ],
            out_specs=pl.BlockSpec((BLOCKS_PER_STEP * TB, PACKED),
                                   lambda s, be, nx, na, iv: (jnp.minimum(s, n_steps - 1), 0)),
            scratch_shapes=[
                row_buf, row_buf, row_buf, row_buf,
                pltpu.SemaphoreType.DMA, pltpu.SemaphoreType.DMA,
                pltpu.SemaphoreType.DMA, pltpu.SemaphoreType.DMA,
                pltpu.VMEM((2,) + in_out, F32), pltpu.VMEM((2,) + in_out, F32), pltpu.VMEM((2,) + out_in, F32),
                pltpu.SemaphoreType.DMA((2,)),
                pltpu.SMEM((1,), jnp.int32),
                pltpu.VMEM(in_out, BF16), pltpu.VMEM(in_out, BF16), pltpu.VMEM(out_in, BF16),
            ],
        ),
        out_shape=jax.ShapeDtypeStruct((P_SLOTS, PACKED), jnp.uint32),
        compiler_params=pltpu.CompilerParams(
            dimension_semantics=("arbitrary",), vmem_limit_bytes=VMEM_LIMIT),
        name="experts",
    )(block_e, next_e, n_active, inv, xp, w_gate, w_up, w_down)


def _combine_body(dest_ref, h1_ref, rw_ref, gf_ref, y_ref, out_ref,
                  ya0, yb0, ya1, yb1, sem0, sem1):
    i = pl.program_id(0)
    bufs = ((ya0, yb0, sem0), (ya1, yb1, sem1))

    def fetch(tile, r, k, dst, sem):
        slot = dest_ref[(tile * TC + r) * TOP_K + k]
        return pltpu.make_async_copy(y_ref.at[pl.ds(slot, 1)], dst.at[pl.ds(r, 1)], sem)

    def step(par, last):
        ya, yb, sem = bufs[par]
        for dst in (ya, yb):
            pltpu.make_async_copy(y_ref.at[pl.ds(0, TC)], dst, sem).wait()
        if not last:
            na, nb, nsem = bufs[1 - par]
            for r in range(TC):
                fetch(i + 1, r, 0, na, nsem).start()
                fetch(i + 1, r, 1, nb, nsem).start(priority=1)
        w = rw_ref[...]
        w1, w2 = w[:, 0:1], w[:, 1:2]
        a_lo, a_hi = _unpack_halves(ya[...])
        b_lo, b_hi = _unpack_halves(yb[...])
        lo = h1_ref[:, 0:PACKED] + (w1 * a_lo + w2 * b_lo)
        hi = h1_ref[:, PACKED:D_MODEL] + (w1 * a_hi + w2 * b_hi)
        ms = (jnp.sum(lo * lo, axis=-1, keepdims=True)
              + jnp.sum(hi * hi, axis=-1, keepdims=True)) / D_MODEL
        scale = lax.rsqrt(ms + EPS)
        out_ref[:, 0:PACKED] = (lo * scale) * gf_ref[:, 0:PACKED]
        out_ref[:, PACKED:D_MODEL] = (hi * scale) * gf_ref[:, PACKED:D_MODEL]

    @pl.when(i == 0)
    def _():
        def first_rows(r, carry):
            fetch(0, r, 0, ya0, sem0).start()
            fetch(0, r, 1, yb0, sem0).start()
            return carry
        lax.fori_loop(0, TC, first_rows, 0)

    @pl.when(i % 2 == 0)
    def _():
        step(0, last=False)

    @pl.when((i % 2 == 1) & (i < N_CTILES - 1))
    def _():
        step(1, last=False)

    @pl.when(i == N_CTILES - 1)
    def _():
        step(1, last=True)


def _combine(dest, h1, rw, gf, y):
    row_buf = pltpu.VMEM((TC, PACKED), jnp.uint32)
    return pl.pallas_call(
        _combine_body,
        grid_spec=pltpu.PrefetchScalarGridSpec(
            num_scalar_prefetch=1,
            grid=(N_CTILES,),
            in_specs=[
                pl.BlockSpec((TC, D_MODEL), lambda i, d: (i, 0)),
                pl.BlockSpec((TC, LANES), lambda i, d: (i, 0)),
                pl.BlockSpec((1, D_MODEL), lambda i, d: (0, 0)),
                pl.BlockSpec(memory_space=pl.ANY),
            ],
            out_specs=pl.BlockSpec((TC, D_MODEL), lambda i, d: (i, 0)),
            scratch_shapes=[row_buf, row_buf, row_buf, row_buf,
                            pltpu.SemaphoreType.DMA, pltpu.SemaphoreType.DMA],
        ),
        out_shape=jax.ShapeDtypeStruct((N_TOK, D_MODEL), F32),
        compiler_params=pltpu.CompilerParams(
            dimension_semantics=("arbitrary",), vmem_limit_bytes=VMEM_LIMIT),
        name="combine",
    )(dest, h1, rw, gf, y)


def _route_tables(cnt, ri):
    counts = cnt[0, N_GROUPS:N_GROUPS + N_EXPERTS].astype(jnp.int32)
    padded = (counts + TB - 1) // TB * TB
    pad_end = jnp.cumsum(padded)
    pad_start = pad_end - padded
    experts = jnp.arange(N_EXPERTS, dtype=jnp.int32)
    start_of = jnp.sum(jnp.where(ri[:, 0:TOP_K, None] == experts, pad_start, 0), axis=-1)
    dest = (start_of + ri[:, TOP_K:2 * TOP_K]).reshape(N_ASSIGN)
    n_active = (pad_end[-1:] // TB).astype(jnp.int32)
    block_row = jnp.arange(N_BLOCKS, dtype=jnp.int32)[:, None] * TB
    block_e = jnp.minimum(jnp.sum((pad_end[None, :] <= block_row).astype(jnp.int32), axis=1), N_EXPERTS - 1)
    run_end = jnp.sum(jnp.where(block_e[:, None] == experts, pad_end, 0), axis=-1) // TB
    next_e = jnp.where(run_end < n_active[0], block_e[jnp.minimum(run_end, N_BLOCKS - 1)], -1)
    return dest, block_e, next_e.astype(jnp.int32), n_active


def kernel(x, norm1_g, w_in, b_gate, conv_w, ln_v_g, ln_v_b, sgu_w, sgu_b, w_branch, w_out, norm2_g,
           router_g, router_g_b, router_e, router_e_b, w_gate, w_up, w_down, final_g):
    h = x.reshape(N_TOK, D_MODEL)

    w_in_b = w_in.astype(BF16)
    w_branch_b = w_branch.astype(BF16)
    w_out_b = w_out.astype(BF16)
    g1 = norm1_g.reshape(DEPTH, 1, D_MODEL)
    g2 = norm2_g.reshape(DEPTH, 1, D_MODEL)
    gf = final_g.reshape(1, D_MODEL)
    ln_g = ln_v_g.reshape(DEPTH, 1, WIDTH)
    ln_b = ln_v_b.reshape(DEPTH, 1, WIDTH)
    bg = b_gate.reshape(DEPTH, 1, 2 * D_MODEL)
    sgu_bias = jnp.repeat(jnp.swapaxes(sgu_b, 1, 2), GROUP_DIM, axis=2)
    pad = LANES - N_GROUPS - N_EXPERTS
    w_r = jnp.concatenate([router_g, router_e, jnp.zeros((DEPTH, D_MODEL, pad), F32)], axis=2)
    wr_hi = w_r.astype(BF16)
    w_router = jnp.concatenate([wr_hi, (w_r - wr_hi.astype(F32)).astype(BF16)], axis=2)
    r_bias = jnp.concatenate([router_g_b, router_e_b, jnp.zeros((DEPTH, pad), F32)], axis=1)
    r_bias = r_bias.reshape(DEPTH, 1, LANES)

    moe = None
    for l in range(DEPTH):
        if moe is None:
            ya, yb = _mixer_in(l, TN, h, g1, w_in_b, conv_w, ln_g, ln_b, sgu_w, sgu_bias)
        else:
            ya, yb, h = _mixer_in(l, TN, None, g1, w_in_b, conv_w, ln_g, ln_b, sgu_w, sgu_bias, moe)
        h1, xp, ri, rw, cnt = _mixer_out(l, h, g1, ya, yb, w_in_b, bg, w_branch_b, w_out_b, g2,
                                         w_router, r_bias)
        dest, block_e, next_e, n_active = _route_tables(cnt, ri)
        y = _experts(l, block_e, next_e, n_active, _invert(dest), xp, w_gate, w_up, w_down)
        moe = (dest, h1, rw, y)
    return _combine(*moe[:3], gf, moe[3]).reshape(BATCH, SEQ, D_MODEL)
```

```python
import functools

import jax
import jax.numpy as jnp
from jax import lax
from jax.experimental import pallas as pl
from jax.experimental.pallas import tpu as pltpu

F32 = jnp.float32
BF16 = jnp.bfloat16

D_MODEL = 2048
BATCH = 4
SEQ = 4096
DEPTH = 2
N_TOK = BATCH * SEQ
WIDTH = D_MODEL // 2
CONV_K = 3
SGU_GROUPS = 8
CHUNK = 128
GROUP_DIM = WIDTH // SGU_GROUPS
N_GROUPS = 4
PER_GROUP = 8
N_EXPERTS = N_GROUPS * PER_GROUP
TOP_K = 2
N_ASSIGN = N_TOK * TOP_K
D_EXPERT = D_MODEL // 4
EPS = 1e-6

LANES = 128
SUBLANES = 8
VMEM_LIMIT = 56 * 1024 * 1024

COL_C, COL_B, COL_XA = 0, WIDTH, 2 * WIDTH
COL_U, COL_V = 3 * WIDTH, 4 * WIDTH
COL_GA, COL_GB = 5 * WIDTH, 5 * WIDTH + D_MODEL

TM = 512
TN = 512
TILES_PER_SEQ = SEQ // TM
J1 = D_MODEL // TN
J2 = D_MODEL // TN
J_OUT = J1 + J2

PACKED = D_MODEL // 2
assert (J2 // 2) * TN == PACKED
TB = 256
P_SLOTS = N_ASSIGN + N_EXPERTS * TB
N_BLOCKS = P_SLOTS // TB
BLOCKS_PER_STEP = 4
GATHER_AHEAD = 2
assert N_BLOCKS % BLOCKS_PER_STEP == 0 and GATHER_AHEAD < BLOCKS_PER_STEP
INVERT_UNROLL = 16
TC = 512
N_CTILES = N_TOK // TC
assert N_CTILES % 2 == 0


def _dot(a, b):
    return jnp.dot(a, b, preferred_element_type=F32)


def _pack_halves(lo, hi):
    lo_bits = pltpu.bitcast(lo.astype(BF16).astype(F32), jnp.uint32)
    hi_bits = pltpu.bitcast(hi.astype(BF16).astype(F32), jnp.uint32)
    return lax.shift_right_logical(lo_bits, jnp.uint32(16)) | (hi_bits & jnp.uint32(0xFFFF0000))


def _unpack_halves(words):
    lo = pltpu.bitcast(lax.shift_left(words, jnp.uint32(16)), F32)
    hi = pltpu.bitcast(words & jnp.uint32(0xFFFF0000), F32)
    return lo, hi


def _mixer_in_body(tn, fused, *refs):
    if fused:
        (dest_ref, h1_ref, rw_ref, y_ref, g1_ref, wc_ref, wb_ref, wa_ref, wu_ref, wv_ref, cw_ref, lng_ref,
         lnb_ref, sw_ref, sb_ref, ya_ref, yb_ref, h_ref, xn_scr, xc_scr, carry_scr, u_scr, v_scr,
         fa, fb, fsem) = refs
    else:
        (h_ref, g1_ref, wc_ref, wb_ref, wa_ref, wu_ref, wv_ref, cw_ref, lng_ref, lnb_ref, sw_ref, sb_ref,
         ya_ref, yb_ref, xn_scr, xc_scr, carry_scr, u_scr, v_scr) = refs
    jc = WIDTH // tn
    jz = WIDTH // tn
    n_tiles = N_TOK // TM
    rows_per_step = TM
    i = pl.program_id(0)
    j = pl.program_id(1)

    def fetch(tile, r, k):
        src = y_ref.at[pl.ds(dest_ref[(tile * TM + r) * TOP_K + k], 1)]
        return pltpu.make_async_copy(src, (fa, fb)[k].at[pl.ds(r, 1)], fsem)

    def wait_fetch():
        for dst in (fa, fb):
            pltpu.make_async_copy(y_ref.at[pl.ds(0, TM)], dst, fsem).wait()

    def issue_fetch(step):
        if fused:
            nxt = jnp.where(i + 1 == n_tiles, 0, i + 1)
            for r in range(step * rows_per_step, (step + 1) * rows_per_step):
                fetch(nxt, r, 0).start()
                fetch(nxt, r, 1).start(priority=1)

    if fused:
        @pl.when((i == 0) & (j == 0))
        def _():
            def first_rows(r, carry):
                fetch(0, r, 0).start()
                fetch(0, r, 1).start()
                return carry
            lax.fori_loop(0, TM, first_rows, 0)

    @pl.when((i == 0) & (j == 0))
    def _():
        carry_scr[...] = jnp.zeros((jc, SUBLANES, tn), F32)

    @pl.when(j == 0)
    def _():
        if fused:
            wait_fetch()
            w = rw_ref[...]
            w1, w2 = w[:, 0:1], w[:, 1:2]
            a_lo, a_hi = _unpack_halves(fa[...])
            b_lo, b_hi = _unpack_halves(fb[...])
            halves = (h1_ref[:, 0:PACKED] + (w1 * a_lo + w2 * b_lo),
                      h1_ref[:, PACKED:D_MODEL] + (w1 * a_hi + w2 * b_hi))
            ms = sum(jnp.sum(v * v, axis=-1, keepdims=True) for v in halves) / D_MODEL
            scale = lax.rsqrt(ms + EPS)
            for c, v in enumerate(halves):
                cols = slice(c * PACKED, (c + 1) * PACKED)
                h_ref[:, cols] = v
                xn_scr[:, cols] = ((v * scale) * g1_ref[:, cols]).astype(BF16)
        else:
            x = h_ref[...]
            ms = jnp.mean(x * x, axis=-1, keepdims=True)
            xn_scr[...] = ((x * lax.rsqrt(ms + EPS)) * g1_ref[...]).astype(BF16)

    def conv_step(jv):
        if jv == 0:
            issue_fetch(jv)
        xn = xn_scr[...]
        xc = _dot(xn, wc_ref[...]) * _dot(xn, wa_ref[...])

        xc_scr[0:SUBLANES, :] = jnp.where(i % TILES_PER_SEQ == 0, 0.0, carry_scr[jv])
        xc_scr[SUBLANES:SUBLANES + TM, :] = xc
        x1 = xc_scr[SUBLANES - 1:SUBLANES - 1 + TM, :]
        x2 = xc_scr[SUBLANES - 2:SUBLANES - 2 + TM, :]
        cw = cw_ref[...]
        conv = cw[0:1, :] * x2 + cw[1:2, :] * x1 + cw[2:3, :] * xc
        carry_scr[jv] = xc_scr[TM:TM + SUBLANES, :]
        ya_ref[...] = (_dot(xn, wb_ref[...]) * conv).astype(BF16)

    def gating_step(jv):
        xn = xn_scr[...]
        u_scr[jv] = jax.nn.gelu(_dot(xn, wu_ref[...]))
        v_scr[jv] = jax.nn.gelu(_dot(xn, wv_ref[...]))

    for jv in range(jc):
        pl.when(j == jv)(functools.partial(conv_step, jv))
    for jv in range(jz):
        pl.when(j == jc + jv)(functools.partial(gating_step, jv))

    @pl.when(j == jc + jz - 1)
    def _():
        s1 = jnp.zeros((TM, 1), F32)
        for k in range(jz):
            s1 = s1 + jnp.sum(v_scr[k], axis=-1, keepdims=True)
        mu = s1 / WIDTH
        s2 = jnp.zeros((TM, 1), F32)
        for k in range(jz):
            d = v_scr[k] - mu
            s2 = s2 + jnp.sum(d * d, axis=-1, keepdims=True)
        rstd = lax.rsqrt(s2 / WIDTH + EPS)
        row = lax.broadcasted_iota(jnp.int32, (CHUNK, CHUNK), 0)
        col = lax.broadcasted_iota(jnp.int32, (CHUNK, CHUNK), 1)
        causal = col <= row
        gpt = tn // GROUP_DIM
        for k in range(jz):
            vn = (((v_scr[k] - mu) * rstd) * lng_ref[:, k * tn:(k + 1) * tn]
                  + lnb_ref[:, k * tn:(k + 1) * tn]).astype(BF16)
            for gl in range(gpt):
                g = k * gpt + gl
                w = jnp.where(causal, sw_ref[g], 0.0).astype(BF16)
                bias = sb_ref[:, g * GROUP_DIM:(g + 1) * GROUP_DIM]
                for n in range(TM // CHUNK):
                    rows = slice(n * CHUNK, (n + 1) * CHUNK)
                    mixed = _dot(w, vn[rows, gl * GROUP_DIM:(gl + 1) * GROUP_DIM])
                    u = u_scr[k, rows, gl * GROUP_DIM:(gl + 1) * GROUP_DIM]
                    yb_ref[rows, g * GROUP_DIM:(g + 1) * GROUP_DIM] = (u * (mixed + bias)).astype(BF16)

    if fused:
        @pl.when((i == n_tiles - 1) & (j == jc + jz - 1))
        def _():
            wait_fetch()


def _mixer_in(l, tn, h, g1, w_in, conv_w, ln_g, ln_b, sgu_w, sgu_bias, moe=None):
    fused = moe is not None
    jc = jz = WIDTH // tn
    cj = lambda j: jnp.minimum(j, jc - 1)
    zj = lambda j: jnp.clip(j - jc, 0, jz - 1)
    wspec = lambda off, f: pl.BlockSpec((None, D_MODEL, tn), lambda i, j, *_: (l, 0, off // tn + f(j)))
    row_tile = pl.BlockSpec((TM, D_MODEL), lambda i, j, *_: (i, 0))
    in_specs = [
        pl.BlockSpec((None, 1, D_MODEL), lambda i, j, *_: (l, 0, 0)),
        wspec(COL_C, cj), wspec(COL_B, cj), wspec(COL_XA, cj), wspec(COL_U, zj), wspec(COL_V, zj),
        pl.BlockSpec((None, CONV_K, tn), lambda i, j, *_: (l, 0, cj(j))),
        pl.BlockSpec((None, 1, WIDTH), lambda i, j, *_: (l, 0, 0)),
        pl.BlockSpec((None, 1, WIDTH), lambda i, j, *_: (l, 0, 0)),
        pl.BlockSpec((None, SGU_GROUPS, CHUNK, CHUNK), lambda i, j, *_: (l, 0, 0, 0)),
        pl.BlockSpec((None, CHUNK, WIDTH), lambda i, j, *_: (l, 0, 0)),
    ]
    out_specs = [
        pl.BlockSpec((TM, tn), lambda i, j, *_: (i, cj(j))),
        pl.BlockSpec((TM, WIDTH), lambda i, j, *_: (i, 0)),
    ]
    out_shape = [jax.ShapeDtypeStruct((N_TOK, WIDTH), BF16), jax.ShapeDtypeStruct((N_TOK, WIDTH), BF16)]
    scratch = [
        pltpu.VMEM((TM, D_MODEL), BF16),
        pltpu.VMEM((TM + SUBLANES, tn), F32),
        pltpu.VMEM((jc, SUBLANES, tn), F32),
        pltpu.VMEM((jz, TM, tn), F32),
        pltpu.VMEM((jz, TM, tn), F32),
    ]
    weights = (g1, w_in, w_in, w_in, w_in, w_in, conv_w, ln_g, ln_b, sgu_w, sgu_bias)
    if fused:
        dest, h1, rw, y = moe
        in_specs = [row_tile, pl.BlockSpec((TM, LANES), lambda i, j, *_: (i, 0)),
                    pl.BlockSpec(memory_space=pl.ANY)] + in_specs
        out_specs.append(row_tile)
        out_shape.append(jax.ShapeDtypeStruct((N_TOK, D_MODEL), F32))
        scratch += [pltpu.VMEM((TM, PACKED), jnp.uint32), pltpu.VMEM((TM, PACKED), jnp.uint32),
                    pltpu.SemaphoreType.DMA]
        operands = (dest, h1, rw, y) + weights
    else:
        in_specs = [row_tile] + in_specs
        operands = (h,) + weights
    return pl.pallas_call(
        functools.partial(_mixer_in_body, tn, fused),
        grid_spec=pltpu.PrefetchScalarGridSpec(
            num_scalar_prefetch=1 if fused else 0,
            grid=(N_TOK // TM, jc + jz),
            in_specs=in_specs, out_specs=out_specs, scratch_shapes=scratch),
        out_shape=out_shape,
        compiler_params=pltpu.CompilerParams(
            dimension_semantics=("arbitrary", "arbitrary"), vmem_limit_bytes=VMEM_LIMIT),
        name="mixer_in",
    )(*operands)


def _mixer_out_body(h_ref, g1_ref, ya_ref, yb_ref, wga_ref, wgb_ref, bga_ref, bgb_ref, wba_ref, wbb_ref,
                    wo_ref, g2_ref, wr_ref, rb_ref,
                    h1_ref, xp_ref, ri_ref, rw_ref, cnt_ref,
                    xn_scr, mg_scr, out_scr, cnt_scr):
    i = pl.program_id(0)
    j = pl.program_id(1)

    @pl.when(j == 0)
    def _():
        x = h_ref[...]
        ms = jnp.mean(x * x, axis=-1, keepdims=True)
        xn_scr[...] = ((x * lax.rsqrt(ms + EPS)) * g1_ref[...]).astype(BF16)

    @pl.when((i == 0) & (j == 0))
    def _():
        cnt_scr[...] = jnp.zeros((1, LANES), F32)

    @pl.when(j < J1)
    def _():
        xn = xn_scr[...]
        ga = jax.nn.sigmoid(_dot(xn, wga_ref[...]) + bga_ref[...])
        gb = jax.nn.sigmoid(_dot(xn, wgb_ref[...]) + bgb_ref[...])
        mg_scr[j] = (ga * _dot(ya_ref[...], wba_ref[...]) + gb * _dot(yb_ref[...], wbb_ref[...])).astype(BF16)

    @pl.when(j >= J1)
    def _():
        acc = _dot(mg_scr[0], wo_ref[0:TN, :])
        for k in range(1, J1):
            acc = acc + _dot(mg_scr[k], wo_ref[k * TN:(k + 1) * TN, :])
        out_scr[j - J1] = acc

    @pl.when(j == J_OUT - 1)
    def _():
        ss = jnp.zeros((TM, 1), F32)
        for k in range(J2):
            cols = slice(k * TN, (k + 1) * TN)
            hk = h_ref[:, cols] + out_scr[k]
            h1_ref[:, cols] = hk
            ss = ss + jnp.sum(hk * hk, axis=-1, keepdims=True)
        rstd = lax.rsqrt(ss / D_MODEL + EPS)

        acc = jnp.zeros((TM, 2 * LANES), F32)
        for k in range(J2):
            cols = slice(k * TN, (k + 1) * TN)
            xk = (h1_ref[:, cols] * rstd) * g2_ref[:, cols]
            hi = xk.astype(BF16)
            acc = acc + _dot(hi, wr_ref[cols, :])
            bits = pltpu.bitcast(hi.astype(F32), jnp.uint32)
            if k < J2 // 2:
                xp_ref[:, cols] = lax.shift_right_logical(bits, jnp.uint32(16))
            else:
                pcols = slice(k * TN - PACKED, (k + 1) * TN - PACKED)
                xp_ref[:, pcols] = xp_ref[:, pcols] | (bits & jnp.uint32(0xFFFF0000))
        lg = (acc[:, 0:LANES] + acc[:, LANES:2 * LANES]) + rb_ref[...]

        lane_i = lax.broadcasted_iota(jnp.int32, (TM, LANES), 1)
        lane = lane_i.astype(F32)
        neg = -jnp.inf
        big = float(LANES)
        is_g = lane_i < N_GROUPS
        gl = jnp.where(is_g, lg, neg)
        gmax = jnp.max(gl, axis=-1, keepdims=True)
        g_idx = jnp.min(jnp.where(gl == gmax, lane, big), axis=-1, keepdims=True)
        g_w = 1.0 / jnp.sum(jnp.exp(gl - gmax), axis=-1, keepdims=True)

        first = N_GROUPS + g_idx * PER_GROUP
        in_grp = (lane >= first) & (lane < first + PER_GROUP)
        el = jnp.where(in_grp, lg, neg)
        m1 = jnp.max(el, axis=-1, keepdims=True)
        i1 = jnp.min(jnp.where(in_grp & (el == m1), lane, big), axis=-1, keepdims=True)
        el2 = jnp.where(lane == i1, neg, el)
        m2 = jnp.max(el2, axis=-1, keepdims=True)
        i2 = jnp.min(jnp.where(in_grp & (lane != i1) & (el2 == m2), lane, big), axis=-1, keepdims=True)
        t = jnp.exp(m2 - m1)
        w1 = g_w * (1.0 / (1.0 + t))
        w2 = g_w * (t / (1.0 + t))

        onehot = jnp.where((lane == i1) | (lane == i2), 1.0, 0.0)
        r_i = lax.broadcasted_iota(jnp.int32, (TM, TM), 0)
        c_i = lax.broadcasted_iota(jnp.int32, (TM, TM), 1)
        tri = jnp.where(c_i < r_i, 1.0, 0.0).astype(BF16)
        before = cnt_scr[...] + _dot(tri, onehot.astype(BF16))
        rank1 = jnp.sum(jnp.where(lane == i1, before, 0.0), axis=-1, keepdims=True)
        rank2 = jnp.sum(jnp.where(lane == i2, before, 0.0), axis=-1, keepdims=True)
        cnt_scr[...] = cnt_scr[...] + jnp.sum(onehot, axis=0, keepdims=True)

        ri = jnp.where(lane_i == 0, i1 - N_GROUPS,
                       jnp.where(lane_i == 1, i2 - N_GROUPS,
                                 jnp.where(lane_i == 2, rank1, jnp.where(lane_i == 3, rank2, 0.0))))
        ri_ref[...] = ri.astype(jnp.int32)
        rw_ref[...] = jnp.where(lane_i == 0, w1, jnp.where(lane_i == 1, w2, 0.0))
        cnt_ref[...] = jnp.broadcast_to(cnt_scr[...], (SUBLANES, LANES))


def _mixer_out(l, h, g1, ya, yb, w_in, b_gate, w_branch, w_out, g2, w_router, r_bias):
    mj = lambda j: jnp.minimum(j, J1 - 1)
    oj = lambda j: jnp.clip(j - J1, 0, J2 - 1)
    const = lambda *blk: pl.BlockSpec((None,) + blk, lambda i, j: (l,) + (0,) * len(blk))
    return pl.pallas_call(
        _mixer_out_body,
        grid=(N_TOK // TM, J_OUT),
        in_specs=[
            pl.BlockSpec((TM, D_MODEL), lambda i, j: (i, 0)),
            const(1, D_MODEL),
            pl.BlockSpec((TM, WIDTH), lambda i, j: (i, 0)),
            pl.BlockSpec((TM, WIDTH), lambda i, j: (i, 0)),
            pl.BlockSpec((None, D_MODEL, TN), lambda i, j: (l, 0, COL_GA // TN + mj(j))),
            pl.BlockSpec((None, D_MODEL, TN), lambda i, j: (l, 0, COL_GB // TN + mj(j))),
            pl.BlockSpec((None, 1, TN), lambda i, j: (l, 0, mj(j))),
            pl.BlockSpec((None, 1, TN), lambda i, j: (l, 0, D_MODEL // TN + mj(j))),
            pl.BlockSpec((None, None, WIDTH, TN), lambda i, j: (l, 0, 0, mj(j))),
            pl.BlockSpec((None, None, WIDTH, TN), lambda i, j: (l, 1, 0, mj(j))),
            pl.BlockSpec((None, D_MODEL, TN), lambda i, j: (l, 0, oj(j))),
            const(1, D_MODEL),
            const(D_MODEL, 2 * LANES), const(1, LANES),
        ],
        out_specs=[
            pl.BlockSpec((TM, D_MODEL), lambda i, j: (i, 0)),
            pl.BlockSpec((TM, PACKED), lambda i, j: (i, 0)),
            pl.BlockSpec((TM, LANES), lambda i, j: (i, 0)),
            pl.BlockSpec((TM, LANES), lambda i, j: (i, 0)),
            pl.BlockSpec((SUBLANES, LANES), lambda i, j: (0, 0)),
        ],
        out_shape=[jax.ShapeDtypeStruct((N_TOK, D_MODEL), F32),
                   jax.ShapeDtypeStruct((N_TOK, PACKED), jnp.uint32),
                   jax.ShapeDtypeStruct((N_TOK, LANES), jnp.int32),
                   jax.ShapeDtypeStruct((N_TOK, LANES), F32),
                   jax.ShapeDtypeStruct((SUBLANES, LANES), F32)],
        scratch_shapes=[
            pltpu.VMEM((TM, D_MODEL), BF16),
            pltpu.VMEM((J1, TM, TN), BF16),
            pltpu.VMEM((J2, TM, TN), F32),
            pltpu.VMEM((1, LANES), F32),
        ],
        compiler_params=pltpu.CompilerParams(
            dimension_semantics=("arbitrary", "arbitrary"), vmem_limit_bytes=VMEM_LIMIT),
        name="mixer_out",
    )(h, g1, ya, yb, w_in, w_in, b_gate, b_gate, w_branch, w_branch, w_out, g2, w_router, r_bias)


def _invert_body(first_ref, second_ref, spare_ref, inv_ref, sem):
    fill = pltpu.make_async_copy(spare_ref, inv_ref, sem)
    fill.start()
    fill.wait()

    def place(c, carry):
        t0 = c * INVERT_UNROLL
        for u in range(INVERT_UNROLL):
            inv_ref[first_ref[t0 + u]] = t0 + u
            inv_ref[second_ref[t0 + u]] = t0 + u
        return carry

    lax.fori_loop(0, N_TOK // INVERT_UNROLL, place, 0)


def _invert(dest):
    spare = jnp.arange(P_SLOTS, dtype=jnp.int32) % N_TOK
    smem = pl.BlockSpec(memory_space=pltpu.SMEM)
    return pl.pallas_call(
        _invert_body,
        in_specs=[smem, smem, pl.BlockSpec(memory_space=pl.ANY)],
        out_specs=pl.BlockSpec(memory_space=pltpu.SMEM),
        out_shape=jax.ShapeDtypeStruct((P_SLOTS,), jnp.int32),
        scratch_shapes=[pltpu.SemaphoreType.DMA],
        name="invert",
    )(dest[0::TOP_K], dest[1::TOP_K], spare)


def _experts_body(l, be_ref, nx_ref, na_ref, inv_ref, xp_ref, wg_ref, wu_ref, wd_ref, y_ref,
                  xbuf0, xbuf1, xbuf2, xbuf3, gsem0, gsem1, gsem2, gsem3,
                  wg_f, wu_f, wd_f, wsem, wslot, wg_s, wu_s, wd_s):
    n_act = na_ref[0]
    bufs = ((xbuf0, gsem0), (xbuf1, gsem1), (xbuf2, gsem2), (xbuf3, gsem3))

    def weight_copies(e, s):
        return [pltpu.make_async_copy(src.at[l, e], dst.at[s], wsem.at[s])
                for src, dst in ((wg_ref, wg_f), (wu_ref, wu_f), (wd_ref, wd_f))]

    def gather(blk, r, xdst, sem):
        return pltpu.make_async_copy(
            xp_ref.at[pl.ds(inv_ref[blk * TB + r], 1)], xdst.at[pl.ds(r, 1)], sem)

    def wait_gather(xdst, sem):
        pltpu.make_async_copy(xp_ref.at[pl.ds(0, TB)], xdst, sem).wait()

    def step(b, par):
        xcur, gcur = bufs[par]
        xnxt, gnxt = bufs[(par + GATHER_AHEAD) % BLOCKS_PER_STEP]
        wait_gather(xcur, gcur)

        @pl.when((b == 0) | (be_ref[b] != be_ref[jnp.maximum(b - 1, 0)]))
        def _():
            s = wslot[0]
            for c in weight_copies(be_ref[b], s):
                c.wait()
            wg_s[...] = wg_f[s].astype(BF16)
            wu_s[...] = wu_f[s].astype(BF16)
            wd_s[...] = wd_f[s].astype(BF16)

            @pl.when(nx_ref[b] >= 0)
            def _():
                for c in weight_copies(nx_ref[b], 1 - s):
                    c.start(priority=1)

            wslot[0] = 1 - s

        ahead = jnp.minimum(b + GATHER_AHEAD, N_BLOCKS - 1)
        for r in range(TB):
            gather(ahead, r, xnxt, gnxt).start()
        x_lo, x_hi = _unpack_halves(xcur[...])
        x_lo, x_hi = x_lo.astype(BF16), x_hi.astype(BF16)
        gate = _dot(x_lo, wg_s[0:PACKED, :]) + _dot(x_hi, wg_s[PACKED:D_MODEL, :])
        up = _dot(x_lo, wu_s[0:PACKED, :]) + _dot(x_hi, wu_s[PACKED:D_MODEL, :])
        hm = (jax.nn.silu(gate) * up).astype(BF16)
        y_ref[par * TB:(par + 1) * TB, :] = _pack_halves(
            _dot(hm, wd_s[:, 0:PACKED]), _dot(hm, wd_s[:, PACKED:D_MODEL]))

    @pl.when(pl.program_id(0) == 0)
    def _():
        def first_rows(r, carry):
            for k in range(GATHER_AHEAD):
                gather(k, r, *bufs[k]).start()
            return carry
        lax.fori_loop(0, TB, first_rows, 0)
        wslot[0] = 0
        for c in weight_copies(be_ref[0], 0):
            c.start(priority=1)

    for par in range(BLOCKS_PER_STEP):
        b = BLOCKS_PER_STEP * pl.program_id(0) + par
        pl.when(b < n_act)(functools.partial(step, b, par))

        @pl.when((b >= n_act) & (b < n_act + GATHER_AHEAD))
        def _():
            wait_gather(*bufs[par])

        @pl.when((b >= n_act) & (b < N_BLOCKS))
        def _():
            y_ref[par * TB:(par + 1) * TB, :] = jnp.zeros((TB, PACKED), jnp.uint32)


def _experts(l, block_e, next_e, n_active, inv, xp, w_gate, w_up, w_down):
    row_buf = pltpu.VMEM((TB, PACKED), jnp.uint32)
    in_out = (D_MODEL, D_EXPERT)
    out_in = (D_EXPERT, D_MODEL)
    n_steps = N_BLOCKS // BLOCKS_PER_STEP
    return pl.pallas_call(
        functools.partial(_experts_body, l),
        grid_spec=pltpu.PrefetchScalarGridSpec(
            num_scalar_prefetch=4,
            grid=(n_steps + 1,),
            in_specs=[
                pl.BlockSpec(memory_space=pl.ANY),
                pl.BlockSpec(memory_space=pl.ANY),
                pl.BlockSpec(memory_space=pl.ANY),
                pl.BlockSpec(memory_space=pl.ANY),
            ],
            out_specs=pl.BlockSpec((BLOCKS_PER_STEP * TB, PACKED),
                                   lambda s, be, nx, na, iv: (jnp.minimum(s, n_steps - 1), 0)),
            scratch_shapes=[
                row_buf, row_buf, row_buf, row_buf,
                pltpu.SemaphoreType.DMA, pltpu.SemaphoreType.DMA,
                pltpu.SemaphoreType.DMA, pltpu.SemaphoreType.DMA,
                pltpu.VMEM((2,) + in_out, F32), pltpu.VMEM((2,) + in_out, F32), pltpu.VMEM((2,) + out_in, F32),
                pltpu.SemaphoreType.DMA((2,)),
                pltpu.SMEM((1,), jnp.int32),
                pltpu.VMEM(in_out, BF16), pltpu.VMEM(in_out, BF16), pltpu.VMEM(out_in, BF16),
            ],
        ),
        out_shape=jax.ShapeDtypeStruct((P_SLOTS, PACKED), jnp.uint32),
        compiler_params=pltpu.CompilerParams(
            dimension_semantics=("arbitrary",), vmem_limit_bytes=VMEM_LIMIT),
        name="experts",
    )(block_e, next_e, n_active, inv, xp, w_gate, w_up, w_down)


def _combine_body(dest_ref, h1_ref, rw_ref, gf_ref, y_ref, out_ref,
                  ya0, yb0, ya1, yb1, sem0, sem1):
    i = pl.program_id(0)
    bufs = ((ya0, yb0, sem0), (ya1, yb1, sem1))

    def fetch(tile, r, k, dst, sem):
        slot = dest_ref[(tile * TC + r) * TOP_K + k]
        return pltpu.make_async_copy(y_ref.at[pl.ds(slot, 1)], dst.at[pl.ds(r, 1)], sem)

    def step(par, last):
        ya, yb, sem = bufs[par]
        for dst in (ya, yb):
            pltpu.make_async_copy(y_ref.at[pl.ds(0, TC)], dst, sem).wait()
        if not last:
            na, nb, nsem = bufs[1 - par]
            for r in range(TC):
                fetch(i + 1, r, 0, na, nsem).start()
                fetch(i + 1, r, 1, nb, nsem).start(priority=1)
        w = rw_ref[...]
        w1, w2 = w[:, 0:1], w[:, 1:2]
        a_lo, a_hi = _unpack_halves(ya[...])
        b_lo, b_hi = _unpack_halves(yb[...])
        lo = h1_ref[:, 0:PACKED] + (w1 * a_lo + w2 * b_lo)
        hi = h1_ref[:, PACKED:D_MODEL] + (w1 * a_hi + w2 * b_hi)
        ms = (jnp.sum(lo * lo, axis=-1, keepdims=True)
              + jnp.sum(hi * hi, axis=-1, keepdims=True)) / D_MODEL
        scale = lax.rsqrt(ms + EPS)
        out_ref[:, 0:PACKED] = (lo * scale) * gf_ref[:, 0:PACKED]
        out_ref[:, PACKED:D_MODEL] = (hi * scale) * gf_ref[:, PACKED:D_MODEL]

    @pl.when(i == 0)
    def _():
        def first_rows(r, carry):
            fetch(0, r, 0, ya0, sem0).start()
            fetch(0, r, 1, yb0, sem0).start()
            return carry
        lax.fori_loop(0, TC, first_rows, 0)

    @pl.when(i % 2 == 0)
    def _():
        step(0, last=False)

    @pl.when((i % 2 == 1) & (i < N_CTILES - 1))
    def _():
        step(1, last=False)

    @pl.when(i == N_CTILES - 1)
    def _():
        step(1, last=True)


def _combine(dest, h1, rw, gf, y):
    row_buf = pltpu.VMEM((TC, PACKED), jnp.uint32)
    return pl.pallas_call(
        _combine_body,
        grid_spec=pltpu.PrefetchScalarGridSpec(
            num_scalar_prefetch=1,
            grid=(N_CTILES,),
            in_specs=[
                pl.BlockSpec((TC, D_MODEL), lambda i, d: (i, 0)),
                pl.BlockSpec((TC, LANES), lambda i, d: (i, 0)),
                pl.BlockSpec((1, D_MODEL), lambda i, d: (0, 0)),
                pl.BlockSpec(memory_space=pl.ANY),
            ],
            out_specs=pl.BlockSpec((TC, D_MODEL), lambda i, d: (i, 0)),
            scratch_shapes=[row_buf, row_buf, row_buf, row_buf,
                            pltpu.SemaphoreType.DMA, pltpu.SemaphoreType.DMA],
        ),
        out_shape=jax.ShapeDtypeStruct((N_TOK, D_MODEL), F32),
        compiler_params=pltpu.CompilerParams(
            dimension_semantics=("arbitrary",), vmem_limit_bytes=VMEM_LIMIT),
        name="combine",
    )(dest, h1, rw, gf, y)


def _route_tables(cnt, ri):
    counts = cnt[0, N_GROUPS:N_GROUPS + N_EXPERTS].astype(jnp.int32)
    padded = (counts + TB - 1) // TB * TB
    pad_end = jnp.cumsum(padded)
    pad_start = pad_end - padded
    experts = jnp.arange(N_EXPERTS, dtype=jnp.int32)
    start_of = jnp.sum(jnp.where(ri[:, 0:TOP_K, None] == experts, pad_start, 0), axis=-1)
    dest = (start_of + ri[:, TOP_K:2 * TOP_K]).reshape(N_ASSIGN)
    n_active = (pad_end[-1:] // TB).astype(jnp.int32)
    block_row = jnp.arange(N_BLOCKS, dtype=jnp.int32)[:, None] * TB
    block_e = jnp.minimum(jnp.sum((pad_end[None, :] <= block_row).astype(jnp.int32), axis=1), N_EXPERTS - 1)
    run_end = jnp.sum(jnp.where(block_e[:, None] == experts, pad_end, 0), axis=-1) // TB
    next_e = jnp.where(run_end < n_active[0], block_e[jnp.minimum(run_end, N_BLOCKS - 1)], -1)
    return dest, block_e, next_e.astype(jnp.int32), n_active


def kernel(x, norm1_g, w_in, b_gate, conv_w, ln_v_g, ln_v_b, sgu_w, sgu_b, w_branch, w_out, norm2_g,
           router_g, router_g_b, router_e, router_e_b, w_gate, w_up, w_down, final_g):
    h = x.reshape(N_TOK, D_MODEL)

    w_in_b = w_in.astype(BF16)
    w_branch_b = w_branch.astype(BF16)
    w_out_b = w_out.astype(BF16)
    g1 = norm1_g.reshape(DEPTH, 1, D_MODEL)
    g2 = norm2_g.reshape(DEPTH, 1, D_MODEL)
    gf = final_g.reshape(1, D_MODEL)
    ln_g = ln_v_g.reshape(DEPTH, 1, WIDTH)
    ln_b = ln_v_b.reshape(DEPTH, 1, WIDTH)
    bg = b_gate.reshape(DEPTH, 1, 2 * D_MODEL)
    sgu_bias = jnp.repeat(jnp.swapaxes(sgu_b, 1, 2), GROUP_DIM, axis=2)
    pad = LANES - N_GROUPS - N_EXPERTS
    w_r = jnp.concatenate([router_g, router_e, jnp.zeros((DEPTH, D_MODEL, pad), F32)], axis=2)
    wr_hi = w_r.astype(BF16)
    w_router = jnp.concatenate([wr_hi, (w_r - wr_hi.astype(F32)).astype(BF16)], axis=2)
    r_bias = jnp.concatenate([router_g_b, router_e_b, jnp.zeros((DEPTH, pad), F32)], axis=1)
    r_bias = r_bias.reshape(DEPTH, 1, LANES)

    moe = None
    for l in range(DEPTH):
        if moe is None:
            ya, yb = _mixer_in(l, TN, h, g1, w_in_b, conv_w, ln_g, ln_b, sgu_w, sgu_bias)
        else:
            ya, yb, h = _mixer_in(l, TN, None, g1, w_in_b, conv_w, ln_g, ln_b, sgu_w, sgu_bias, moe)
        h1, xp, ri, rw, cnt = _mixer_out(l, h, g1, ya, yb, w_in_b, bg, w_branch_b, w_out_b, g2,
                                         w_router, r_bias)
        dest, block_e, next_e, n_active = _route_tables(cnt, ri)
        y = _experts(l, block_e, next_e, n_active, _invert(dest), xp, w_gate, w_up, w_down)
        moe = (dest, h1, rw, y)
    return _combine(*moe[:3], gf, moe[3]).reshape(BATCH, SEQ, D_MODEL)
```

```python
import functools

import jax
import jax.numpy as jnp
from jax import lax
from jax.experimental import pallas as pl
from jax.experimental.pallas import tpu as pltpu

F32 = jnp.float32
BF16 = jnp.bfloat16

D_MODEL = 2048
BATCH = 4
SEQ = 4096
DEPTH = 2
N_TOK = BATCH * SEQ
WIDTH = D_MODEL // 2
CONV_K = 3
SGU_GROUPS = 8
CHUNK = 128
GROUP_DIM = WIDTH // SGU_GROUPS
N_GROUPS = 4
PER_GROUP = 8
N_EXPERTS = N_GROUPS * PER_GROUP
TOP_K = 2
N_ASSIGN = N_TOK * TOP_K
D_EXPERT = D_MODEL // 4
EPS = 1e-6

LANES = 128
SUBLANES = 8
VMEM_LIMIT = 56 * 1024 * 1024

COL_C, COL_B, COL_XA = 0, WIDTH, 2 * WIDTH
COL_U, COL_V = 3 * WIDTH, 4 * WIDTH
COL_GA, COL_GB = 5 * WIDTH, 5 * WIDTH + D_MODEL

TM = 512
TN = 512
TILES_PER_SEQ = SEQ // TM
J1 = D_MODEL // TN
J2 = D_MODEL // TN
J_OUT = J1 + J2

PACKED = D_MODEL // 2
assert (J2 // 2) * TN == PACKED
TB = 256
P_SLOTS = N_ASSIGN + N_EXPERTS * TB
N_BLOCKS = P_SLOTS // TB
BLOCKS_PER_STEP = 4
GATHER_AHEAD = 2
assert N_BLOCKS % BLOCKS_PER_STEP == 0 and GATHER_AHEAD < BLOCKS_PER_STEP
INVERT_UNROLL = 16
TC = 512
N_CTILES = N_TOK // TC
assert N_CTILES % 2 == 0


def _dot(a, b):
    return jnp.dot(a, b, preferred_element_type=F32)


def _pack_halves(lo, hi):
    lo_bits = pltpu.bitcast(lo.astype(BF16).astype(F32), jnp.uint32)
    hi_bits = pltpu.bitcast(hi.astype(BF16).astype(F32), jnp.uint32)
    return lax.shift_right_logical(lo_bits, jnp.uint32(16)) | (hi_bits & jnp.uint32(0xFFFF0000))


def _unpack_halves(words):
    lo = pltpu.bitcast(lax.shift_left(words, jnp.uint32(16)), F32)
    hi = pltpu.bitcast(words & jnp.uint32(0xFFFF0000), F32)
    return lo, hi


def _mixer_in_body(tn, fused, *refs):
    if fused:
        (dest_ref, h1_ref, rw_ref, y_ref, g1_ref, wc_ref, wb_ref, wa_ref, wu_ref, wv_ref, cw_ref, lng_ref,
         lnb_ref, sw_ref, sb_ref, ya_ref, yb_ref, h_ref, xn_scr, xc_scr, carry_scr, u_scr, v_scr,
         fa, fb, fsem) = refs
    else:
        (h_ref, g1_ref, wc_ref, wb_ref, wa_ref, wu_ref, wv_ref, cw_ref, lng_ref, lnb_ref, sw_ref, sb_ref,
         ya_ref, yb_ref, xn_scr, xc_scr, carry_scr, u_scr, v_scr) = refs
    jc = WIDTH // tn
    jz = WIDTH // tn
    n_tiles = N_TOK // TM
    rows_per_step = TM
    i = pl.program_id(0)
    j = pl.program_id(1)

    def fetch(tile, r, k):
        src = y_ref.at[pl.ds(dest_ref[(tile * TM + r) * TOP_K + k], 1)]
        return pltpu.make_async_copy(src, (fa, fb)[k].at[pl.ds(r, 1)], fsem)

    def wait_fetch():
        for dst in (fa, fb):
            pltpu.make_async_copy(y_ref.at[pl.ds(0, TM)], dst, fsem).wait()

    def issue_fetch(step):
        if fused:
            nxt = jnp.where(i + 1 == n_tiles, 0, i + 1)
            for r in range(step * rows_per_step, (step + 1) * rows_per_step):
                fetch(nxt, r, 0).start()
                fetch(nxt, r, 1).start(priority=1)

    if fused:
        @pl.when((i == 0) & (j == 0))
        def _():
            def first_rows(r, carry):
                fetch(0, r, 0).start()
                fetch(0, r, 1).start()
                return carry
            lax.fori_loop(0, TM, first_rows, 0)

    @pl.when((i == 0) & (j == 0))
    def _():
        carry_scr[...] = jnp.zeros((jc, SUBLANES, tn), F32)

    @pl.when(j == 0)
    def _():
        if fused:
            wait_fetch()
            w = rw_ref[...]
            w1, w2 = w[:, 0:1], w[:, 1:2]
            a_lo, a_hi = _unpack_halves(fa[...])
            b_lo, b_hi = _unpack_halves(fb[...])
            halves = (h1_ref[:, 0:PACKED] + (w1 * a_lo + w2 * b_lo),
                      h1_ref[:, PACKED:D_MODEL] + (w1 * a_hi + w2 * b_hi))
            ms = sum(jnp.sum(v * v, axis=-1, keepdims=True) for v in halves) / D_MODEL
            scale = lax.rsqrt(ms + EPS)
            for c, v in enumerate(halves):
                cols = slice(c * PACKED, (c + 1) * PACKED)
                h_ref[:, cols] = v
                xn_scr[:, cols] = ((v * scale) * g1_ref[:, cols]).astype(BF16)
        else:
            x = h_ref[...]
            ms = jnp.mean(x * x, axis=-1, keepdims=True)
            xn_scr[...] = ((x * lax.rsqrt(ms + EPS)) * g1_ref[...]).astype(BF16)

    def conv_step(jv):
        if jv == 0:
            issue_fetch(jv)
        xn = xn_scr[...]
        xc = _dot(xn, wc_ref[...]) * _dot(xn, wa_ref[...])

        xc_scr[0:SUBLANES, :] = jnp.where(i % TILES_PER_SEQ == 0, 0.0, carry_scr[jv])
        xc_scr[SUBLANES:SUBLANES + TM, :] = xc
        x1 = xc_scr[SUBLANES - 1:SUBLANES - 1 + TM, :]
        x2 = xc_scr[SUBLANES - 2:SUBLANES - 2 + TM, :]
        cw = cw_ref[...]
        conv = cw[0:1, :] * x2 + cw[1:2, :] * x1 + cw[2:3, :] * xc
        carry_scr[jv] = xc_scr[TM:TM + SUBLANES, :]
        ya_ref[...] = (_dot(xn, wb_ref[...]) * conv).astype(BF16)

    def gating_step(jv):
        xn = xn_scr[...]
        u_scr[jv] = jax.nn.gelu(_dot(xn, wu_ref[...]))
        v_scr[jv] = jax.nn.gelu(_dot(xn, wv_ref[...]))

    for jv in range(jc):
        pl.when(j == jv)(functools.partial(conv_step, jv))
    for jv in range(jz):
        pl.when(j == jc + jv)(functools.partial(gating_step, jv))

    @pl.when(j == jc + jz - 1)
    def _():
        s1 = jnp.zeros((TM, 1), F32)
        for k in range(jz):
            s1 = s1 + jnp.sum(v_scr[k], axis=-1, keepdims=True)
        mu = s1 / WIDTH
        s2 = jnp.zeros((TM, 1), F32)
        for k in range(jz):
            d = v_scr[k] - mu
            s2 = s2 + jnp.sum(d * d, axis=-1, keepdims=True)
        rstd = lax.rsqrt(s2 / WIDTH + EPS)
        row = lax.broadcasted_iota(jnp.int32, (CHUNK, CHUNK), 0)
        col = lax.broadcasted_iota(jnp.int32, (CHUNK, CHUNK), 1)
        causal = col <= row
        gpt = tn // GROUP_DIM
        for k in range(jz):
            vn = (((v_scr[k] - mu) * rstd) * lng_ref[:, k * tn:(k + 1) * tn]
                  + lnb_ref[:, k * tn:(k + 1) * tn]).astype(BF16)
            for gl in range(gpt):
                g = k * gpt + gl
                w = jnp.where(causal, sw_ref[g], 0.0).astype(BF16)
                bias = sb_ref[:, g * GROUP_DIM:(g + 1) * GROUP_DIM]
                for n in range(TM // CHUNK):
                    rows = slice(n * CHUNK, (n + 1) * CHUNK)
                    mixed = _dot(w, vn[rows, gl * GROUP_DIM:(gl + 1) * GROUP_DIM])
                    u = u_scr[k, rows, gl * GROUP_DIM:(gl + 1) * GROUP_DIM]
                    yb_ref[rows, g * GROUP_DIM:(g + 1) * GROUP_DIM] = (u * (mixed + bias)).astype(BF16)

    if fused:
        @pl.when((i == n_tiles - 1) & (j == jc + jz - 1))
        def _():
            wait_fetch()


def _mixer_in(l, tn, h, g1, w_in, conv_w, ln_g, ln_b, sgu_w, sgu_bias, moe=None):
    fused = moe is not None
    jc = jz = WIDTH // tn
    cj = lambda j: jnp.minimum(j, jc - 1)
    zj = lambda j: jnp.clip(j - jc, 0, jz - 1)
    wspec = lambda off, f: pl.BlockSpec((None, D_MODEL, tn), lambda i, j, *_: (l, 0, off // tn + f(j)))
    row_tile = pl.BlockSpec((TM, D_MODEL), lambda i, j, *_: (i, 0))
    in_specs = [
        pl.BlockSpec((None, 1, D_MODEL), lambda i, j, *_: (l, 0, 0)),
        wspec(COL_C, cj), wspec(COL_B, cj), wspec(COL_XA, cj), wspec(COL_U, zj), wspec(COL_V, zj),
        pl.BlockSpec((None, CONV_K, tn), lambda i, j, *_: (l, 0, cj(j))),
        pl.BlockSpec((None, 1, WIDTH), lambda i, j, *_: (l, 0, 0)),
        pl.BlockSpec((None, 1, WIDTH), lambda i, j, *_: (l, 0, 0)),
        pl.BlockSpec((None, SGU_GROUPS, CHUNK, CHUNK), lambda i, j, *_: (l, 0, 0, 0)),
        pl.BlockSpec((None, CHUNK, WIDTH), lambda i, j, *_: (l, 0, 0)),
    ]
    out_specs = [
        pl.BlockSpec((TM, tn), lambda i, j, *_: (i, cj(j))),
        pl.BlockSpec((TM, WIDTH), lambda i, j, *_: (i, 0)),
    ]
    out_shape = [jax.ShapeDtypeStruct((N_TOK, WIDTH), BF16), jax.ShapeDtypeStruct((N_TOK, WIDTH), BF16)]
    scratch = [
        pltpu.VMEM((TM, D_MODEL), BF16),
        pltpu.VMEM((TM + SUBLANES, tn), F32),
        pltpu.VMEM((jc, SUBLANES, tn), F32),
        pltpu.VMEM((jz, TM, tn), F32),
        pltpu.VMEM((jz, TM, tn), F32),
    ]
    weights = (g1, w_in, w_in, w_in, w_in, w_in, conv_w, ln_g, ln_b, sgu_w, sgu_bias)
    if fused:
        dest, h1, rw, y = moe
        in_specs = [row_tile, pl.BlockSpec((TM, LANES), lambda i, j, *_: (i, 0)),
                    pl.BlockSpec(memory_space=pl.ANY)] + in_specs
        out_specs.append(row_tile)
        out_shape.append(jax.ShapeDtypeStruct((N_TOK, D_MODEL), F32))
        scratch += [pltpu.VMEM((TM, PACKED), jnp.uint32), pltpu.VMEM((TM, PACKED), jnp.uint32),
                    pltpu.SemaphoreType.DMA]
        operands = (dest, h1, rw, y) + weights
    else:
        in_specs = [row_tile] + in_specs
        operands = (h,) + weights
    return pl.pallas_call(
        functools.partial(_mixer_in_body, tn, fused),
        grid_spec=pltpu.PrefetchScalarGridSpec(
            num_scalar_prefetch=1 if fused else 0,
            grid=(N_TOK // TM, jc + jz),
            in_specs=in_specs, out_specs=out_specs, scratch_shapes=scratch),
        out_shape=out_shape,
        compiler_params=pltpu.CompilerParams(
            dimension_semantics=("arbitrary", "arbitrary"), vmem_limit_bytes=VMEM_LIMIT),
        name="mixer_in",
    )(*operands)


def _mixer_out_body(h_ref, g1_ref, ya_ref, yb_ref, wga_ref, wgb_ref, bga_ref, bgb_ref, wba_ref, wbb_ref,
                    wo_ref, g2_ref, wr_ref, rb_ref,
                    h1_ref, xp_ref, ri_ref, rw_ref, cnt_ref,
                    xn_scr, mg_scr, out_scr, cnt_scr):
    i = pl.program_id(0)
    j = pl.program_id(1)

    @pl.when(j == 0)
    def _():
        x = h_ref[...]
        ms = jnp.mean(x * x, axis=-1, keepdims=True)
        xn_scr[...] = ((x * lax.rsqrt(ms + EPS)) * g1_ref[...]).astype(BF16)

    @pl.when((i == 0) & (j == 0))
    def _():
        cnt_scr[...] = jnp.zeros((1, LANES), F32)

    @pl.when(j < J1)
    def _():
        xn = xn_scr[...]
        ga = jax.nn.sigmoid(_dot(xn, wga_ref[...]) + bga_ref[...])
        gb = jax.nn.sigmoid(_dot(xn, wgb_ref[...]) + bgb_ref[...])
        mg_scr[j] = (ga * _dot(ya_ref[...], wba_ref[...]) + gb * _dot(yb_ref[...], wbb_ref[...])).astype(BF16)

    @pl.when(j >= J1)
    def _():
        acc = _dot(mg_scr[0], wo_ref[0:TN, :])
        for k in range(1, J1):
            acc = acc + _dot(mg_scr[k], wo_ref[k * TN:(k + 1) * TN, :])
        out_scr[j - J1] = acc

    @pl.when(j == J_OUT - 1)
    def _():
        ss = jnp.zeros((TM, 1), F32)
        for k in range(J2):
            cols = slice(k * TN, (k + 1) * TN)
            hk = h_ref[:, cols] + out_scr[k]
            h1_ref[:, cols] = hk
            ss = ss + jnp.sum(hk * hk, axis=-1, keepdims=True)
        rstd = lax.rsqrt(ss / D_MODEL + EPS)

        acc = jnp.zeros((TM, 2 * LANES), F32)
        for k in range(J2):
            cols = slice(k * TN, (k + 1) * TN)
            xk = (h1_ref[:, cols] * rstd) * g2_ref[:, cols]
            hi = xk.astype(BF16)
            lo = (xk - hi.astype(F32)).astype(BF16)
            acc = acc + (_dot(hi, wr_ref[cols, :]) + _dot(lo, wr_ref[cols, :]))
            bits = pltpu.bitcast(hi.astype(F32), jnp.uint32)
            if k < J2 // 2:
                xp_ref[:, cols] = lax.shift_right_logical(bits, jnp.uint32(16))
            else:
                pcols = slice(k * TN - PACKED, (k + 1) * TN - PACKED)
                xp_ref[:, pcols] = xp_ref[:, pcols] | (bits & jnp.uint32(0xFFFF0000))
        lg = (acc[:, 0:LANES] + acc[:, LANES:2 * LANES]) + rb_ref[...]

        lane_i = lax.broadcasted_iota(jnp.int32, (TM, LANES), 1)
        lane = lane_i.astype(F32)
        neg = -jnp.inf
        big = float(LANES)
        is_g = lane_i < N_GROUPS
        gl = jnp.where(is_g, lg, neg)
        gmax = jnp.max(gl, axis=-1, keepdims=True)
        g_idx = jnp.min(jnp.where(gl == gmax, lane, big), axis=-1, keepdims=True)
        g_w = 1.0 / jnp.sum(jnp.exp(gl - gmax), axis=-1, keepdims=True)

        first = N_GROUPS + g_idx * PER_GROUP
        in_grp = (lane >= first) & (lane < first + PER_GROUP)
        el = jnp.where(in_grp, lg, neg)
        m1 = jnp.max(el, axis=-1, keepdims=True)
        i1 = jnp.min(jnp.where(in_grp & (el == m1), lane, big), axis=-1, keepdims=True)
        el2 = jnp.where(lane == i1, neg, el)
        m2 = jnp.max(el2, axis=-1, keepdims=True)
        i2 = jnp.min(jnp.where(in_grp & (lane != i1) & (el2 == m2), lane, big), axis=-1, keepdims=True)
        t = jnp.exp(m2 - m1)
        w1 = g_w * (1.0 / (1.0 + t))
        w2 = g_w * (t / (1.0 + t))

        onehot = jnp.where((lane == i1) | (lane == i2), 1.0, 0.0)
        r_i = lax.broadcasted_iota(jnp.int32, (TM, TM), 0)
        c_i = lax.broadcasted_iota(jnp.int32, (TM, TM), 1)
        tri = jnp.where(c_i < r_i, 1.0, 0.0).astype(BF16)
        before = cnt_scr[...] + _dot(tri, onehot.astype(BF16))
        rank1 = jnp.sum(jnp.where(lane == i1, before, 0.0), axis=-1, keepdims=True)
        rank2 = jnp.sum(jnp.where(lane == i2, before, 0.0), axis=-1, keepdims=True)
        cnt_scr[...] = cnt_scr[...] + jnp.sum(onehot, axis=0, keepdims=True)

        ri = jnp.where(lane_i == 0, i1 - N_GROUPS,
                       jnp.where(lane_i == 1, i2 - N_GROUPS,
                                 jnp.where(lane_i == 2, rank1, jnp.where(lane_i == 3, rank2, 0.0))))
        ri_ref[...] = ri.astype(jnp.int32)
        rw_ref[...] = jnp.where(lane_i == 0, w1, jnp.where(lane_i == 1, w2, 0.0))
        cnt_ref[...] = jnp.broadcast_to(cnt_scr[...], (SUBLANES, LANES))


def _mixer_out(l, h, g1, ya, yb, w_in, b_gate, w_branch, w_out, g2, w_router, r_bias):
    mj = lambda j: jnp.minimum(j, J1 - 1)
    oj = lambda j: jnp.clip(j - J1, 0, J2 - 1)
    const = lambda *blk: pl.BlockSpec((None,) + blk, lambda i, j: (l,) + (0,) * len(blk))
    return pl.pallas_call(
        _mixer_out_body,
        grid=(N_TOK // TM, J_OUT),
        in_specs=[
            pl.BlockSpec((TM, D_MODEL), lambda i, j: (i, 0)),
            const(1, D_MODEL),
            pl.BlockSpec((TM, WIDTH), lambda i, j: (i, 0)),
            pl.BlockSpec((TM, WIDTH), lambda i, j: (i, 0)),
            pl.BlockSpec((None, D_MODEL, TN), lambda i, j: (l, 0, COL_GA // TN + mj(j))),
            pl.BlockSpec((None, D_MODEL, TN), lambda i, j: (l, 0, COL_GB // TN + mj(j))),
            pl.BlockSpec((None, 1, TN), lambda i, j: (l, 0, mj(j))),
            pl.BlockSpec((None, 1, TN), lambda i, j: (l, 0, D_MODEL // TN + mj(j))),
            pl.BlockSpec((None, None, WIDTH, TN), lambda i, j: (l, 0, 0, mj(j))),
            pl.BlockSpec((None, None, WIDTH, TN), lambda i, j: (l, 1, 0, mj(j))),
            pl.BlockSpec((None, D_MODEL, TN), lambda i, j: (l, 0, oj(j))),
            const(1, D_MODEL),
            const(D_MODEL, 2 * LANES), const(1, LANES),
        ],
        out_specs=[
            pl.BlockSpec((TM, D_MODEL), lambda i, j: (i, 0)),
            pl.BlockSpec((TM, PACKED), lambda i, j: (i, 0)),
            pl.BlockSpec((TM, LANES), lambda i, j: (i, 0)),
            pl.BlockSpec((TM, LANES), lambda i, j: (i, 0)),
            pl.BlockSpec((SUBLANES, LANES), lambda i, j: (0, 0)),
        ],
        out_shape=[jax.ShapeDtypeStruct((N_TOK, D_MODEL), F32),
                   jax.ShapeDtypeStruct((N_TOK, PACKED), jnp.uint32),
                   jax.ShapeDtypeStruct((N_TOK, LANES), jnp.int32),
                   jax.ShapeDtypeStruct((N_TOK, LANES), F32),
                   jax.ShapeDtypeStruct((SUBLANES, LANES), F32)],
        scratch_shapes=[
            pltpu.VMEM((TM, D_MODEL), BF16),
            pltpu.VMEM((J1, TM, TN), BF16),
            pltpu.VMEM((J2, TM, TN), F32),
            pltpu.VMEM((1, LANES), F32),
        ],
        compiler_params=pltpu.CompilerParams(
            dimension_semantics=("arbitrary", "arbitrary"), vmem_limit_bytes=VMEM_LIMIT),
        name="mixer_out",
    )(h, g1, ya, yb, w_in, w_in, b_gate, b_gate, w_branch, w_branch, w_out, g2, w_router, r_bias)


def _invert_body(first_ref, second_ref, spare_ref, inv_ref, sem):
    fill = pltpu.make_async_copy(spare_ref, inv_ref, sem)
    fill.start()
    fill.wait()

    def place(c, carry):
        t0 = c * INVERT_UNROLL
        for u in range(INVERT_UNROLL):
            inv_ref[first_ref[t0 + u]] = t0 + u
            inv_ref[second_ref[t0 + u]] = t0 + u
        return carry

    lax.fori_loop(0, N_TOK // INVERT_UNROLL, place, 0)


def _invert(dest):
    spare = jnp.arange(P_SLOTS, dtype=jnp.int32) % N_TOK
    smem = pl.BlockSpec(memory_space=pltpu.SMEM)
    return pl.pallas_call(
        _invert_body,
        in_specs=[smem, smem, pl.BlockSpec(memory_space=pl.ANY)],
        out_specs=pl.BlockSpec(memory_space=pltpu.SMEM),
        out_shape=jax.ShapeDtypeStruct((P_SLOTS,), jnp.int32),
        scratch_shapes=[pltpu.SemaphoreType.DMA],
        name="invert",
    )(dest[0::TOP_K], dest[1::TOP_K], spare)


def _experts_body(l, be_ref, nx_ref, na_ref, inv_ref, xp_ref, wg_ref, wu_ref, wd_ref, y_ref,
                  xbuf0, xbuf1, xbuf2, xbuf3, gsem0, gsem1, gsem2, gsem3,
                  wg_f, wu_f, wd_f, wsem, wslot, wg_s, wu_s, wd_s):
    n_act = na_ref[0]
    bufs = ((xbuf0, gsem0), (xbuf1, gsem1), (xbuf2, gsem2), (xbuf3, gsem3))

    def weight_copies(e, s):
        return [pltpu.make_async_copy(src.at[l, e], dst.at[s], wsem.at[s])
                for src, dst in ((wg_ref, wg_f), (wu_ref, wu_f), (wd_ref, wd_f))]

    def gather(blk, r, xdst, sem):
        return pltpu.make_async_copy(
            xp_ref.at[pl.ds(inv_ref[blk * TB + r], 1)], xdst.at[pl.ds(r, 1)], sem)

    def wait_gather(xdst, sem):
        pltpu.make_async_copy(xp_ref.at[pl.ds(0, TB)], xdst, sem).wait()

    def step(b, par):
        xcur, gcur = bufs[par]
        xnxt, gnxt = bufs[(par + GATHER_AHEAD) % BLOCKS_PER_STEP]
        wait_gather(xcur, gcur)

        @pl.when((b == 0) | (be_ref[b] != be_ref[jnp.maximum(b - 1, 0)]))
        def _():
            s = wslot[0]
            for c in weight_copies(be_ref[b], s):
                c.wait()
            wg_s[...] = wg_f[s].astype(BF16)
            wu_s[...] = wu_f[s].astype(BF16)
            wd_s[...] = wd_f[s].astype(BF16)

            @pl.when(nx_ref[b] >= 0)
            def _():
                for c in weight_copies(nx_ref[b], 1 - s):
                    c.start(priority=1)

            wslot[0] = 1 - s

        ahead = jnp.minimum(b + GATHER_AHEAD, N_BLOCKS - 1)
        for r in range(TB):
            gather(ahead, r, xnxt, gnxt).start()
        x_lo, x_hi = _unpack_halves(xcur[...])
        x_lo, x_hi = x_lo.astype(BF16), x_hi.astype(BF16)
        gate = _dot(x_lo, wg_s[0:PACKED, :]) + _dot(x_hi, wg_s[PACKED:D_MODEL, :])
        up = _dot(x_lo, wu_s[0:PACKED, :]) + _dot(x_hi, wu_s[PACKED:D_MODEL, :])
        hm = (jax.nn.silu(gate) * up).astype(BF16)
        y_ref[par * TB:(par + 1) * TB, :] = _pack_halves(
            _dot(hm, wd_s[:, 0:PACKED]), _dot(hm, wd_s[:, PACKED:D_MODEL]))

    @pl.when(pl.program_id(0) == 0)
    def _():
        def first_rows(r, carry):
            for k in range(GATHER_AHEAD):
                gather(k, r, *bufs[k]).start()
            return carry
        lax.fori_loop(0, TB, first_rows, 0)
        wslot[0] = 0
        for c in weight_copies(be_ref[0], 0):
            c.start(priority=1)

    for par in range(BLOCKS_PER_STEP):
        b = BLOCKS_PER_STEP * pl.program_id(0) + par
        pl.when(b < n_act)(functools.partial(step, b, par))

        @pl.when((b >= n_act) & (b < n_act + GATHER_AHEAD))
        def _():
            wait_gather(*bufs[par])

        @pl.when((b >= n_act) & (b < N_BLOCKS))
        def _():
            y_ref[par * TB:(par + 1) * TB, :] = jnp.zeros((TB, PACKED), jnp.uint32)


def _experts(l, block_e, next_e, n_active, inv, xp, w_gate, w_up, w_down):
    row_buf = pltpu.VMEM((TB, PACKED), jnp.uint32)
    in_out = (D_MODEL, D_EXPERT)
    out_in = (D_EXPERT, D_MODEL)
    n_steps = N_BLOCKS // BLOCKS_PER_STEP
    return pl.pallas_call(
        functools.partial(_experts_body, l),
        grid_spec=pltpu.PrefetchScalarGridSpec(
            num_scalar_prefetch=4,
            grid=(n_steps + 1,),
            in_specs=[
                pl.BlockSpec(memory_space=pl.ANY),
                pl.BlockSpec(memory_space=pl.ANY),
                pl.BlockSpec(memory_space=pl.ANY),
                pl.BlockSpec(memory_space=pl.ANY),
            ],
            out_specs=pl.BlockSpec((BLOCKS_PER_STEP * TB, PACKED),
                                   lambda s, be, nx, na, iv: (jnp.minimum(s, n_steps - 1), 0)),
            scratch_shapes=[
                row_buf, row_buf, row_buf, row_buf,
                pltpu.SemaphoreType.DMA, pltpu.SemaphoreType.DMA,
                pltpu.SemaphoreType.DMA, pltpu.SemaphoreType.DMA,
                pltpu.VMEM((2,) + in_out, F32), pltpu.VMEM((2,) + in_out, F32), pltpu.VMEM((2,) + out_in, F32),
                pltpu.SemaphoreType.DMA((2,)),
                pltpu.SMEM((1,), jnp.int32),
                pltpu.VMEM(in_out, BF16), pltpu.VMEM(in_out, BF16), pltpu.VMEM(out_in, BF16),
            ],
        ),
        out_shape=jax.ShapeDtypeStruct((P_SLOTS, PACKED), jnp.uint32),
        compiler_params=pltpu.CompilerParams(
            dimension_semantics=("arbitrary",), vmem_limit_bytes=VMEM_LIMIT),
        name="experts",
    )(block_e, next_e, n_active, inv, xp, w_gate, w_up, w_down)


def _combine_body(dest_ref, h1_ref, rw_ref, gf_ref, y_ref, out_ref,
                  ya0, yb0, ya1, yb1, sem0, sem1):
    i = pl.program_id(0)
    bufs = ((ya0, yb0, sem0), (ya1, yb1, sem1))

    def fetch(tile, r, k, dst, sem):
        slot = dest_ref[(tile * TC + r) * TOP_K + k]
        return pltpu.make_async_copy(y_ref.at[pl.ds(slot, 1)], dst.at[pl.ds(r, 1)], sem)

    def step(par, last):
        ya, yb, sem = bufs[par]
        for dst in (ya, yb):
            pltpu.make_async_copy(y_ref.at[pl.ds(0, TC)], dst, sem).wait()
        if not last:
            na, nb, nsem = bufs[1 - par]
            for r in range(TC):
                fetch(i + 1, r, 0, na, nsem).start()
                fetch(i + 1, r, 1, nb, nsem).start(priority=1)
        w = rw_ref[...]
        w1, w2 = w[:, 0:1], w[:, 1:2]
        a_lo, a_hi = _unpack_halves(ya[...])
        b_lo, b_hi = _unpack_halves(yb[...])
        lo = h1_ref[:, 0:PACKED] + (w1 * a_lo + w2 * b_lo)
        hi = h1_ref[:, PACKED:D_MODEL] + (w1 * a_hi + w2 * b_hi)
        ms = (jnp.sum(lo * lo, axis=-1, keepdims=True)
              + jnp.sum(hi * hi, axis=-1, keepdims=True)) / D_MODEL
        scale = lax.rsqrt(ms + EPS)
        out_ref[:, 0:PACKED] = (lo * scale) * gf_ref[:, 0:PACKED]
        out_ref[:, PACKED:D_MODEL] = (hi * scale) * gf_ref[:, PACKED:D_MODEL]

    @pl.when(i == 0)
    def _():
        def first_rows(r, carry):
            fetch(0, r, 0, ya0, sem0).start()
            fetch(0, r, 1, yb0, sem0).start()
            return carry
        lax.fori_loop(0, TC, first_rows, 0)

    @pl.when(i % 2 == 0)
    def _():
        step(0, last=False)

    @pl.when((i % 2 == 1) & (i < N_CTILES - 1))
    def _():
        step(1, last=False)

    @pl.when(i == N_CTILES - 1)
    def _():
        step(1, last=True)


def _combine(dest, h1, rw, gf, y):
    row_buf = pltpu.VMEM((TC, PACKED), jnp.uint32)
    return pl.pallas_call(
        _combine_body,
        grid_spec=pltpu.PrefetchScalarGridSpec(
            num_scalar_prefetch=1,
            grid=(N_CTILES,),
            in_specs=[
                pl.BlockSpec((TC, D_MODEL), lambda i, d: (i, 0)),
                pl.BlockSpec((TC, LANES), lambda i, d: (i, 0)),
                pl.BlockSpec((1, D_MODEL), lambda i, d: (0, 0)),
                pl.BlockSpec(memory_space=pl.ANY),
            ],
            out_specs=pl.BlockSpec((TC, D_MODEL), lambda i, d: (i, 0)),
            scratch_shapes=[row_buf, row_buf, row_buf, row_buf,
                            pltpu.SemaphoreType.DMA, pltpu.SemaphoreType.DMA],
        ),
        out_shape=jax.ShapeDtypeStruct((N_TOK, D_MODEL), F32),
        compiler_params=pltpu.CompilerParams(
            dimension_semantics=("arbitrary",), vmem_limit_bytes=VMEM_LIMIT),
        name="combine",
    )(dest, h1, rw, gf, y)


def _route_tables(cnt, ri):
    counts = cnt[0, N_GROUPS:N_GROUPS + N_EXPERTS].astype(jnp.int32)
    padded = (counts + TB - 1) // TB * TB
    pad_end = jnp.cumsum(padded)
    pad_start = pad_end - padded
    experts = jnp.arange(N_EXPERTS, dtype=jnp.int32)
    start_of = jnp.sum(jnp.where(ri[:, 0:TOP_K, None] == experts, pad_start, 0), axis=-1)
    dest = (start_of + ri[:, TOP_K:2 * TOP_K]).reshape(N_ASSIGN)
    n_active = (pad_end[-1:] // TB).astype(jnp.int32)
    block_row = jnp.arange(N_BLOCKS, dtype=jnp.int32)[:, None] * TB
    block_e = jnp.minimum(jnp.sum((pad_end[None, :] <= block_row).astype(jnp.int32), axis=1), N_EXPERTS - 1)
    run_end = jnp.sum(jnp.where(block_e[:, None] == experts, pad_end, 0), axis=-1) // TB
    next_e = jnp.where(run_end < n_active[0], block_e[jnp.minimum(run_end, N_BLOCKS - 1)], -1)
    return dest, block_e, next_e.astype(jnp.int32), n_active


def kernel(x, norm1_g, w_in, b_gate, conv_w, ln_v_g, ln_v_b, sgu_w, sgu_b, w_branch, w_out, norm2_g,
           router_g, router_g_b, router_e, router_e_b, w_gate, w_up, w_down, final_g):
    h = x.reshape(N_TOK, D_MODEL)

    w_in_b = w_in.astype(BF16)
    w_branch_b = w_branch.astype(BF16)
    w_out_b = w_out.astype(BF16)
    g1 = norm1_g.reshape(DEPTH, 1, D_MODEL)
    g2 = norm2_g.reshape(DEPTH, 1, D_MODEL)
    gf = final_g.reshape(1, D_MODEL)
    ln_g = ln_v_g.reshape(DEPTH, 1, WIDTH)
    ln_b = ln_v_b.reshape(DEPTH, 1, WIDTH)
    bg = b_gate.reshape(DEPTH, 1, 2 * D_MODEL)
    sgu_bias = jnp.repeat(jnp.swapaxes(sgu_b, 1, 2), GROUP_DIM, axis=2)
    pad = LANES - N_GROUPS - N_EXPERTS
    w_r = jnp.concatenate([router_g, router_e, jnp.zeros((DEPTH, D_MODEL, pad), F32)], axis=2)
    wr_hi = w_r.astype(BF16)
    w_router = jnp.concatenate([wr_hi, (w_r - wr_hi.astype(F32)).astype(BF16)], axis=2)
    r_bias = jnp.concatenate([router_g_b, router_e_b, jnp.zeros((DEPTH, pad), F32)], axis=1)
    r_bias = r_bias.reshape(DEPTH, 1, LANES)

    moe = None
    for l in range(DEPTH):
        if moe is None:
            ya, yb = _mixer_in(l, TN, h, g1, w_in_b, conv_w, ln_g, ln_b, sgu_w, sgu_bias)
        else:
            ya, yb, h = _mixer_in(l, TN, None, g1, w_in_b, conv_w, ln_g, ln_b, sgu_w, sgu_bias, moe)
        h1, xp, ri, rw, cnt = _mixer_out(l, h, g1, ya, yb, w_in_b, bg, w_branch_b, w_out_b, g2,
                                         w_router, r_bias)
        dest, block_e, next_e, n_active = _route_tables(cnt, ri)
        y = _experts(l, block_e, next_e, n_active, _invert(dest), xp, w_gate, w_up, w_down)
        moe = (dest, h1, rw, y)
    return _combine(*moe[:3], gf, moe[3]).reshape(BATCH, SEQ, D_MODEL)
```

```python
import functools

import jax
import jax.numpy as jnp
from jax import lax
from jax.experimental import pallas as pl
from jax.experimental.pallas import tpu as pltpu

F32 = jnp.float32
BF16 = jnp.bfloat16

D_MODEL = 2048
BATCH = 4
SEQ = 4096
DEPTH = 2
N_TOK = BATCH * SEQ
WIDTH = D_MODEL // 2
CONV_K = 3
SGU_GROUPS = 8
CHUNK = 128
GROUP_DIM = WIDTH // SGU_GROUPS
N_GROUPS = 4
PER_GROUP = 8
N_EXPERTS = N_GROUPS * PER_GROUP
TOP_K = 2
N_ASSIGN = N_TOK * TOP_K
D_EXPERT = D_MODEL // 4
EPS = 1e-6

LANES = 128
SUBLANES = 8
VMEM_LIMIT = 56 * 1024 * 1024

COL_C, COL_B, COL_XA = 0, WIDTH, 2 * WIDTH
COL_U, COL_V = 3 * WIDTH, 4 * WIDTH
COL_GA, COL_GB = 5 * WIDTH, 5 * WIDTH + D_MODEL

TM = 512
TN = 512
TILES_PER_SEQ = SEQ // TM
J1 = D_MODEL // TN
J2 = D_MODEL // TN
J_OUT = J1 + J2

PACKED = D_MODEL // 2
assert (J2 // 2) * TN == PACKED
TB = 256
P_SLOTS = N_ASSIGN + N_EXPERTS * TB
N_BLOCKS = P_SLOTS // TB
BLOCKS_PER_STEP = 4
GATHER_AHEAD = 2
assert N_BLOCKS % BLOCKS_PER_STEP == 0 and GATHER_AHEAD < BLOCKS_PER_STEP
INVERT_UNROLL = 16
TC = 512
N_CTILES = N_TOK // TC
assert N_CTILES % 2 == 0


def _dot(a, b):
    return jnp.dot(a, b, preferred_element_type=F32)


def _pack_halves(lo, hi):
    lo_bits = pltpu.bitcast(lo.astype(BF16).astype(F32), jnp.uint32)
    hi_bits = pltpu.bitcast(hi.astype(BF16).astype(F32), jnp.uint32)
    return lax.shift_right_logical(lo_bits, jnp.uint32(16)) | (hi_bits & jnp.uint32(0xFFFF0000))


def _unpack_halves(words):
    lo = pltpu.bitcast(lax.shift_left(words, jnp.uint32(16)), F32)
    hi = pltpu.bitcast(words & jnp.uint32(0xFFFF0000), F32)
    return lo, hi


def _mixer_in_body(tn, fused, *refs):
    if fused:
        (dest_ref, h1_ref, rw_ref, y_ref, g1_ref, wc_ref, wb_ref, wa_ref, wu_ref, wv_ref, cw_ref, lng_ref,
         lnb_ref, sw_ref, sb_ref, ya_ref, yb_ref, h_ref, xn_scr, xc_scr, carry_scr, u_scr, v_scr,
         fa, fb, fsem) = refs
    else:
        (h_ref, g1_ref, wc_ref, wb_ref, wa_ref, wu_ref, wv_ref, cw_ref, lng_ref, lnb_ref, sw_ref, sb_ref,
         ya_ref, yb_ref, xn_scr, xc_scr, carry_scr, u_scr, v_scr) = refs
    jc = WIDTH // tn
    jz = WIDTH // tn
    n_tiles = N_TOK // TM
    rows_per_step = TM
    i = pl.program_id(0)
    j = pl.program_id(1)

    def fetch(tile, r, k):
        src = y_ref.at[pl.ds(dest_ref[(tile * TM + r) * TOP_K + k], 1)]
        return pltpu.make_async_copy(src, (fa, fb)[k].at[pl.ds(r, 1)], fsem)

    def wait_fetch():
        for dst in (fa, fb):
            pltpu.make_async_copy(y_ref.at[pl.ds(0, TM)], dst, fsem).wait()

    def issue_fetch(step):
        if fused:
            nxt = jnp.where(i + 1 == n_tiles, 0, i + 1)
            for r in range(step * rows_per_step, (step + 1) * rows_per_step):
                fetch(nxt, r, 0).start(priority=1)
                fetch(nxt, r, 1).start(priority=1)

    if fused:
        @pl.when((i == 0) & (j == 0))
        def _():
            def first_rows(r, carry):
                fetch(0, r, 0).start()
                fetch(0, r, 1).start()
                return carry
            lax.fori_loop(0, TM, first_rows, 0)

    @pl.when((i == 0) & (j == 0))
    def _():
        carry_scr[...] = jnp.zeros((jc, SUBLANES, tn), F32)

    @pl.when(j == 0)
    def _():
        if fused:
            wait_fetch()
            w = rw_ref[...]
            w1, w2 = w[:, 0:1], w[:, 1:2]
            a_lo, a_hi = _unpack_halves(fa[...])
            b_lo, b_hi = _unpack_halves(fb[...])
            halves = (h1_ref[:, 0:PACKED] + (w1 * a_lo + w2 * b_lo),
                      h1_ref[:, PACKED:D_MODEL] + (w1 * a_hi + w2 * b_hi))
            ms = sum(jnp.sum(v * v, axis=-1, keepdims=True) for v in halves) / D_MODEL
            scale = lax.rsqrt(ms + EPS)
            for c, v in enumerate(halves):
                cols = slice(c * PACKED, (c + 1) * PACKED)
                h_ref[:, cols] = v
                xn_scr[:, cols] = ((v * scale) * g1_ref[:, cols]).astype(BF16)
        else:
            x = h_ref[...]
            ms = jnp.mean(x * x, axis=-1, keepdims=True)
            xn_scr[...] = ((x * lax.rsqrt(ms + EPS)) * g1_ref[...]).astype(BF16)

    def conv_step(jv):
        if jv == 0:
            issue_fetch(jv)
        xn = xn_scr[...]
        xc = _dot(xn, wc_ref[...]) * _dot(xn, wa_ref[...])

        xc_scr[0:SUBLANES, :] = jnp.where(i % TILES_PER_SEQ == 0, 0.0, carry_scr[jv])
        xc_scr[SUBLANES:SUBLANES + TM, :] = xc
        x1 = xc_scr[SUBLANES - 1:SUBLANES - 1 + TM, :]
        x2 = xc_scr[SUBLANES - 2:SUBLANES - 2 + TM, :]
        cw = cw_ref[...]
        conv = cw[0:1, :] * x2 + cw[1:2, :] * x1 + cw[2:3, :] * xc
        carry_scr[jv] = xc_scr[TM:TM + SUBLANES, :]
        ya_ref[...] = (_dot(xn, wb_ref[...]) * conv).astype(BF16)

    def gating_step(jv):
        xn = xn_scr[...]
        u_scr[jv] = jax.nn.gelu(_dot(xn, wu_ref[...]))
        v_scr[jv] = jax.nn.gelu(_dot(xn, wv_ref[...]))

    for jv in range(jc):
        pl.when(j == jv)(functools.partial(conv_step, jv))
    for jv in range(jz):
        pl.when(j == jc + jv)(functools.partial(gating_step, jv))

    @pl.when(j == jc + jz - 1)
    def _():
        s1 = jnp.zeros((TM, 1), F32)
        for k in range(jz):
            s1 = s1 + jnp.sum(v_scr[k], axis=-1, keepdims=True)
        mu = s1 / WIDTH
        s2 = jnp.zeros((TM, 1), F32)
        for k in range(jz):
            d = v_scr[k] - mu
            s2 = s2 + jnp.sum(d * d, axis=-1, keepdims=True)
        rstd = lax.rsqrt(s2 / WIDTH + EPS)
        row = lax.broadcasted_iota(jnp.int32, (CHUNK, CHUNK), 0)
        col = lax.broadcasted_iota(jnp.int32, (CHUNK, CHUNK), 1)
        causal = col <= row
        gpt = tn // GROUP_DIM
        for k in range(jz):
            vn = (((v_scr[k] - mu) * rstd) * lng_ref[:, k * tn:(k + 1) * tn]
                  + lnb_ref[:, k * tn:(k + 1) * tn]).astype(BF16)
            for gl in range(gpt):
                g = k * gpt + gl
                w = jnp.where(causal, sw_ref[g], 0.0).astype(BF16)
                bias = sb_ref[:, g * GROUP_DIM:(g + 1) * GROUP_DIM]
                for n in range(TM // CHUNK):
                    rows = slice(n * CHUNK, (n + 1) * CHUNK)
                    mixed = _dot(w, vn[rows, gl * GROUP_DIM:(gl + 1) * GROUP_DIM])
                    u = u_scr[k, rows, gl * GROUP_DIM:(gl + 1) * GROUP_DIM]
                    yb_ref[rows, g * GROUP_DIM:(g + 1) * GROUP_DIM] = (u * (mixed + bias)).astype(BF16)

    if fused:
        @pl.when((i == n_tiles - 1) & (j == jc + jz - 1))
        def _():
            wait_fetch()


def _mixer_in(l, tn, h, g1, w_in, conv_w, ln_g, ln_b, sgu_w, sgu_bias, moe=None):
    fused = moe is not None
    jc = jz = WIDTH // tn
    cj = lambda j: jnp.minimum(j, jc - 1)
    zj = lambda j: jnp.clip(j - jc, 0, jz - 1)
    wspec = lambda off, f: pl.BlockSpec((None, D_MODEL, tn), lambda i, j, *_: (l, 0, off // tn + f(j)))
    row_tile = pl.BlockSpec((TM, D_MODEL), lambda i, j, *_: (i, 0))
    in_specs = [
        pl.BlockSpec((None, 1, D_MODEL), lambda i, j, *_: (l, 0, 0)),
        wspec(COL_C, cj), wspec(COL_B, cj), wspec(COL_XA, cj), wspec(COL_U, zj), wspec(COL_V, zj),
        pl.BlockSpec((None, CONV_K, tn), lambda i, j, *_: (l, 0, cj(j))),
        pl.BlockSpec((None, 1, WIDTH), lambda i, j, *_: (l, 0, 0)),
        pl.BlockSpec((None, 1, WIDTH), lambda i, j, *_: (l, 0, 0)),
        pl.BlockSpec((None, SGU_GROUPS, CHUNK, CHUNK), lambda i, j, *_: (l, 0, 0, 0)),
        pl.BlockSpec((None, CHUNK, WIDTH), lambda i, j, *_: (l, 0, 0)),
    ]
    out_specs = [
        pl.BlockSpec((TM, tn), lambda i, j, *_: (i, cj(j))),
        pl.BlockSpec((TM, WIDTH), lambda i, j, *_: (i, 0)),
    ]
    out_shape = [jax.ShapeDtypeStruct((N_TOK, WIDTH), BF16), jax.ShapeDtypeStruct((N_TOK, WIDTH), BF16)]
    scratch = [
        pltpu.VMEM((TM, D_MODEL), BF16),
        pltpu.VMEM((TM + SUBLANES, tn), F32),
        pltpu.VMEM((jc, SUBLANES, tn), F32),
        pltpu.VMEM((jz, TM, tn), F32),
        pltpu.VMEM((jz, TM, tn), F32),
    ]
    weights = (g1, w_in, w_in, w_in, w_in, w_in, conv_w, ln_g, ln_b, sgu_w, sgu_bias)
    if fused:
        dest, h1, rw, y = moe
        in_specs = [row_tile, pl.BlockSpec((TM, LANES), lambda i, j, *_: (i, 0)),
                    pl.BlockSpec(memory_space=pl.ANY)] + in_specs
        out_specs.append(row_tile)
        out_shape.append(jax.ShapeDtypeStruct((N_TOK, D_MODEL), F32))
        scratch += [pltpu.VMEM((TM, PACKED), jnp.uint32), pltpu.VMEM((TM, PACKED), jnp.uint32),
                    pltpu.SemaphoreType.DMA]
        operands = (dest, h1, rw, y) + weights
    else:
        in_specs = [row_tile] + in_specs
        operands = (h,) + weights
    return pl.pallas_call(
        functools.partial(_mixer_in_body, tn, fused),
        grid_spec=pltpu.PrefetchScalarGridSpec(
            num_scalar_prefetch=1 if fused else 0,
            grid=(N_TOK // TM, jc + jz),
            in_specs=in_specs, out_specs=out_specs, scratch_shapes=scratch),
        out_shape=out_shape,
        compiler_params=pltpu.CompilerParams(
            dimension_semantics=("arbitrary", "arbitrary"), vmem_limit_bytes=VMEM_LIMIT),
        name="mixer_in",
    )(*operands)


def _mixer_out_body(h_ref, g1_ref, ya_ref, yb_ref, wga_ref, wgb_ref, bga_ref, bgb_ref, wba_ref, wbb_ref,
                    wo_ref, g2_ref, wr_ref, rb_ref,
                    h1_ref, xp_ref, ri_ref, rw_ref, cnt_ref,
                    xn_scr, mg_scr, out_scr, cnt_scr):
    i = pl.program_id(0)
    j = pl.program_id(1)

    @pl.when(j == 0)
    def _():
        x = h_ref[...]
        ms = jnp.mean(x * x, axis=-1, keepdims=True)
        xn_scr[...] = ((x * lax.rsqrt(ms + EPS)) * g1_ref[...]).astype(BF16)

    @pl.when((i == 0) & (j == 0))
    def _():
        cnt_scr[...] = jnp.zeros((1, LANES), F32)

    @pl.when(j < J1)
    def _():
        xn = xn_scr[...]
        ga = jax.nn.sigmoid(_dot(xn, wga_ref[...]) + bga_ref[...])
        gb = jax.nn.sigmoid(_dot(xn, wgb_ref[...]) + bgb_ref[...])
        mg_scr[j] = (ga * _dot(ya_ref[...], wba_ref[...]) + gb * _dot(yb_ref[...], wbb_ref[...])).astype(BF16)

    @pl.when(j >= J1)
    def _():
        acc = _dot(mg_scr[0], wo_ref[0:TN, :])
        for k in range(1, J1):
            acc = acc + _dot(mg_scr[k], wo_ref[k * TN:(k + 1) * TN, :])
        out_scr[j - J1] = acc

    @pl.when(j == J_OUT - 1)
    def _():
        ss = jnp.zeros((TM, 1), F32)
        for k in range(J2):
            cols = slice(k * TN, (k + 1) * TN)
            hk = h_ref[:, cols] + out_scr[k]
            h1_ref[:, cols] = hk
            ss = ss + jnp.sum(hk * hk, axis=-1, keepdims=True)
        rstd = lax.rsqrt(ss / D_MODEL + EPS)

        acc = jnp.zeros((TM, 2 * LANES), F32)
        for k in range(J2):
            cols = slice(k * TN, (k + 1) * TN)
            xk = (h1_ref[:, cols] * rstd) * g2_ref[:, cols]
            hi = xk.astype(BF16)
            lo = (xk - hi.astype(F32)).astype(BF16)
            acc = acc + (_dot(hi, wr_ref[cols, :]) + _dot(lo, wr_ref[cols, :]))
            bits = pltpu.bitcast(hi.astype(F32), jnp.uint32)
            if k < J2 // 2:
                xp_ref[:, cols] = lax.shift_right_logical(bits, jnp.uint32(16))
            else:
                pcols = slice(k * TN - PACKED, (k + 1) * TN - PACKED)
                xp_ref[:, pcols] = xp_ref[:, pcols] | (bits & jnp.uint32(0xFFFF0000))
        lg = (acc[:, 0:LANES] + acc[:, LANES:2 * LANES]) + rb_ref[...]

        lane_i = lax.broadcasted_iota(jnp.int32, (TM, LANES), 1)
        lane = lane_i.astype(F32)
        neg = -jnp.inf
        big = float(LANES)
        is_g = lane_i < N_GROUPS
        gl = jnp.where(is_g, lg, neg)
        gmax = jnp.max(gl, axis=-1, keepdims=True)
        g_idx = jnp.min(jnp.where(gl == gmax, lane, big), axis=-1, keepdims=True)
        g_w = 1.0 / jnp.sum(jnp.exp(gl - gmax), axis=-1, keepdims=True)

        first = N_GROUPS + g_idx * PER_GROUP
        in_grp = (lane >= first) & (lane < first + PER_GROUP)
        el = jnp.where(in_grp, lg, neg)
        m1 = jnp.max(el, axis=-1, keepdims=True)
        i1 = jnp.min(jnp.where(in_grp & (el == m1), lane, big), axis=-1, keepdims=True)
        el2 = jnp.where(lane == i1, neg, el)
        m2 = jnp.max(el2, axis=-1, keepdims=True)
        i2 = jnp.min(jnp.where(in_grp & (lane != i1) & (el2 == m2), lane, big), axis=-1, keepdims=True)
        t = jnp.exp(m2 - m1)
        w1 = g_w * (1.0 / (1.0 + t))
        w2 = g_w * (t / (1.0 + t))

        onehot = jnp.where((lane == i1) | (lane == i2), 1.0, 0.0)
        r_i = lax.broadcasted_iota(jnp.int32, (TM, TM), 0)
        c_i = lax.broadcasted_iota(jnp.int32, (TM, TM), 1)
        tri = jnp.where(c_i < r_i, 1.0, 0.0).astype(BF16)
        before = cnt_scr[...] + _dot(tri, onehot.astype(BF16))
        rank1 = jnp.sum(jnp.where(lane == i1, before, 0.0), axis=-1, keepdims=True)
        rank2 = jnp.sum(jnp.where(lane == i2, before, 0.0), axis=-1, keepdims=True)
        cnt_scr[...] = cnt_scr[...] + jnp.sum(onehot, axis=0, keepdims=True)

        ri = jnp.where(lane_i == 0, i1 - N_GROUPS,
                       jnp.where(lane_i == 1, i2 - N_GROUPS,
                                 jnp.where(lane_i == 2, rank1, jnp.where(lane_i == 3, rank2, 0.0))))
        ri_ref[...] = ri.astype(jnp.int32)
        rw_ref[...] = jnp.where(lane_i == 0, w1, jnp.where(lane_i == 1, w2, 0.0))
        cnt_ref[...] = jnp.broadcast_to(cnt_scr[...], (SUBLANES, LANES))


def _mixer_out(l, h, g1, ya, yb, w_in, b_gate, w_branch, w_out, g2, w_router, r_bias):
    mj = lambda j: jnp.minimum(j, J1 - 1)
    oj = lambda j: jnp.clip(j - J1, 0, J2 - 1)
    const = lambda *blk: pl.BlockSpec((None,) + blk, lambda i, j: (l,) + (0,) * len(blk))
    return pl.pallas_call(
        _mixer_out_body,
        grid=(N_TOK // TM, J_OUT),
        in_specs=[
            pl.BlockSpec((TM, D_MODEL), lambda i, j: (i, 0)),
            const(1, D_MODEL),
            pl.BlockSpec((TM, WIDTH), lambda i, j: (i, 0)),
            pl.BlockSpec((TM, WIDTH), lambda i, j: (i, 0)),
            pl.BlockSpec((None, D_MODEL, TN), lambda i, j: (l, 0, COL_GA // TN + mj(j))),
            pl.BlockSpec((None, D_MODEL, TN), lambda i, j: (l, 0, COL_GB // TN + mj(j))),
            pl.BlockSpec((None, 1, TN), lambda i, j: (l, 0, mj(j))),
            pl.BlockSpec((None, 1, TN), lambda i, j: (l, 0, D_MODEL // TN + mj(j))),
            pl.BlockSpec((None, None, WIDTH, TN), lambda i, j: (l, 0, 0, mj(j))),
            pl.BlockSpec((None, None, WIDTH, TN), lambda i, j: (l, 1, 0, mj(j))),
            pl.BlockSpec((None, D_MODEL, TN), lambda i, j: (l, 0, oj(j))),
            const(1, D_MODEL),
            const(D_MODEL, 2 * LANES), const(1, LANES),
        ],
        out_specs=[
            pl.BlockSpec((TM, D_MODEL), lambda i, j: (i, 0)),
            pl.BlockSpec((TM, PACKED), lambda i, j: (i, 0)),
            pl.BlockSpec((TM, LANES), lambda i, j: (i, 0)),
            pl.BlockSpec((TM, LANES), lambda i, j: (i, 0)),
            pl.BlockSpec((SUBLANES, LANES), lambda i, j: (0, 0)),
        ],
        out_shape=[jax.ShapeDtypeStruct((N_TOK, D_MODEL), F32),
                   jax.ShapeDtypeStruct((N_TOK, PACKED), jnp.uint32),
                   jax.ShapeDtypeStruct((N_TOK, LANES), jnp.int32),
                   jax.ShapeDtypeStruct((N_TOK, LANES), F32),
                   jax.ShapeDtypeStruct((SUBLANES, LANES), F32)],
        scratch_shapes=[
            pltpu.VMEM((TM, D_MODEL), BF16),
            pltpu.VMEM((J1, TM, TN), BF16),
            pltpu.VMEM((J2, TM, TN), F32),
            pltpu.VMEM((1, LANES), F32),
        ],
        compiler_params=pltpu.CompilerParams(
            dimension_semantics=("arbitrary", "arbitrary"), vmem_limit_bytes=VMEM_LIMIT),
        name="mixer_out",
    )(h, g1, ya, yb, w_in, w_in, b_gate, b_gate, w_branch, w_branch, w_out, g2, w_router, r_bias)


def _invert_body(first_ref, second_ref, spare_ref, inv_ref, sem):
    fill = pltpu.make_async_copy(spare_ref, inv_ref, sem)
    fill.start()
    fill.wait()

    def place(c, carry):
        t0 = c * INVERT_UNROLL
        for u in range(INVERT_UNROLL):
            inv_ref[first_ref[t0 + u]] = t0 + u
            inv_ref[second_ref[t0 + u]] = t0 + u
        return carry

    lax.fori_loop(0, N_TOK // INVERT_UNROLL, place, 0)


def _invert(dest):
    spare = jnp.arange(P_SLOTS, dtype=jnp.int32) % N_TOK
    smem = pl.BlockSpec(memory_space=pltpu.SMEM)
    return pl.pallas_call(
        _invert_body,
        in_specs=[smem, smem, pl.BlockSpec(memory_space=pl.ANY)],
        out_specs=pl.BlockSpec(memory_space=pltpu.SMEM),
        out_shape=jax.ShapeDtypeStruct((P_SLOTS,), jnp.int32),
        scratch_shapes=[pltpu.SemaphoreType.DMA],
        name="invert",
    )(dest[0::TOP_K], dest[1::TOP_K], spare)


def _experts_body(l, be_ref, nx_ref, na_ref, inv_ref, xp_ref, wg_ref, wu_ref, wd_ref, y_ref,
                  xbuf0, xbuf1, xbuf2, xbuf3, gsem0, gsem1, gsem2, gsem3,
                  wg_f, wu_f, wd_f, wsem, wslot, wg_s, wu_s, wd_s):
    n_act = na_ref[0]
    bufs = ((xbuf0, gsem0), (xbuf1, gsem1), (xbuf2, gsem2), (xbuf3, gsem3))

    def weight_copies(e, s):
        return [pltpu.make_async_copy(src.at[l, e], dst.at[s], wsem.at[s])
                for src, dst in ((wg_ref, wg_f), (wu_ref, wu_f), (wd_ref, wd_f))]

    def gather(blk, r, xdst, sem):
        return pltpu.make_async_copy(
            xp_ref.at[pl.ds(inv_ref[blk * TB + r], 1)], xdst.at[pl.ds(r, 1)], sem)

    def wait_gather(xdst, sem):
        pltpu.make_async_copy(xp_ref.at[pl.ds(0, TB)], xdst, sem).wait()

    def step(b, par):
        xcur, gcur = bufs[par]
        xnxt, gnxt = bufs[(par + GATHER_AHEAD) % BLOCKS_PER_STEP]
        wait_gather(xcur, gcur)

        @pl.when((b == 0) | (be_ref[b] != be_ref[jnp.maximum(b - 1, 0)]))
        def _():
            s = wslot[0]
            for c in weight_copies(be_ref[b], s):
                c.wait()
            wg_s[...] = wg_f[s].astype(BF16)
            wu_s[...] = wu_f[s].astype(BF16)
            wd_s[...] = wd_f[s].astype(BF16)

            @pl.when(nx_ref[b] >= 0)
            def _():
                for c in weight_copies(nx_ref[b], 1 - s):
                    c.start(priority=1)

            wslot[0] = 1 - s

        ahead = jnp.minimum(b + GATHER_AHEAD, N_BLOCKS - 1)
        for r in range(TB):
            gather(ahead, r, xnxt, gnxt).start()
        x_lo, x_hi = _unpack_halves(xcur[...])
        x_lo, x_hi = x_lo.astype(BF16), x_hi.astype(BF16)
        gate = _dot(x_lo, wg_s[0:PACKED, :]) + _dot(x_hi, wg_s[PACKED:D_MODEL, :])
        up = _dot(x_lo, wu_s[0:PACKED, :]) + _dot(x_hi, wu_s[PACKED:D_MODEL, :])
        hm = (jax.nn.silu(gate) * up).astype(BF16)
        y_ref[par * TB:(par + 1) * TB, :] = _pack_halves(
            _dot(hm, wd_s[:, 0:PACKED]), _dot(hm, wd_s[:, PACKED:D_MODEL]))

    @pl.when(pl.program_id(0) == 0)
    def _():
        def first_rows(r, carry):
            for k in range(GATHER_AHEAD):
                gather(k, r, *bufs[k]).start()
            return carry
        lax.fori_loop(0, TB, first_rows, 0)
        wslot[0] = 0
        for c in weight_copies(be_ref[0], 0):
            c.start(priority=1)

    for par in range(BLOCKS_PER_STEP):
        b = BLOCKS_PER_STEP * pl.program_id(0) + par
        pl.when(b < n_act)(functools.partial(step, b, par))

        @pl.when((b >= n_act) & (b < n_act + GATHER_AHEAD))
        def _():
            wait_gather(*bufs[par])

        @pl.when((b >= n_act) & (b < N_BLOCKS))
        def _():
            y_ref[par * TB:(par + 1) * TB, :] = jnp.zeros((TB, PACKED), jnp.uint32)


def _experts(l, block_e, next_e, n_active, inv, xp, w_gate, w_up, w_down):
    row_buf = pltpu.VMEM((TB, PACKED), jnp.uint32)
    in_out = (D_MODEL, D_EXPERT)
    out_in = (D_EXPERT, D_MODEL)
    n_steps = N_BLOCKS // BLOCKS_PER_STEP
    return pl.pallas_call(
        functools.partial(_experts_body, l),
        grid_spec=pltpu.PrefetchScalarGridSpec(
            num_scalar_prefetch=4,
            grid=(n_steps + 1,),
            in_specs=[
                pl.BlockSpec(memory_space=pl.ANY),
                pl.BlockSpec(memory_space=pl.ANY),
                pl.BlockSpec(memory_space=pl.ANY),
                pl.BlockSpec(memory_space=pl.ANY),
            ],
            out_specs=pl.BlockSpec((BLOCKS_PER_STEP * TB, PACKED),
                                   lambda s, be, nx, na, iv: (jnp.minimum(s, n_steps - 1), 0)),
            scratch_shapes=[
                row_buf, row_buf, row_buf, row_buf,
                pltpu.SemaphoreType.DMA, pltpu.SemaphoreType.DMA,
                pltpu.SemaphoreType.DMA, pltpu.SemaphoreType.DMA,
                pltpu.VMEM((2,) + in_out, F32), pltpu.VMEM((2,) + in_out, F32), pltpu.VMEM((2,) + out_in, F32),
                pltpu.SemaphoreType.DMA((2,)),
                pltpu.SMEM((1,), jnp.int32),
                pltpu.VMEM(in_out, BF16), pltpu.VMEM(in_out, BF16), pltpu.VMEM(out_in, BF16),
            ],
        ),
        out_shape=jax.ShapeDtypeStruct((P_SLOTS, PACKED), jnp.uint32),
        compiler_params=pltpu.CompilerParams(
            dimension_semantics=("arbitrary",), vmem_limit_bytes=VMEM_LIMIT),
        name="experts",
    )(block_e, next_e, n_active, inv, xp, w_gate, w_up, w_down)


def _combine_body(dest_ref, h1_ref, rw_ref, gf_ref, y_ref, out_ref,
                  ya0, yb0, ya1, yb1, sem0, sem1):
    i = pl.program_id(0)
    bufs = ((ya0, yb0, sem0), (ya1, yb1, sem1))

    def fetch(tile, r, k, dst, sem):
        slot = dest_ref[(tile * TC + r) * TOP_K + k]
        return pltpu.make_async_copy(y_ref.at[pl.ds(slot, 1)], dst.at[pl.ds(r, 1)], sem)

    def step(par, last):
        ya, yb, sem = bufs[par]
        for dst in (ya, yb):
            pltpu.make_async_copy(y_ref.at[pl.ds(0, TC)], dst, sem).wait()
        if not last:
            na, nb, nsem = bufs[1 - par]
            for r in range(TC):
                fetch(i + 1, r, 0, na, nsem).start()
                fetch(i + 1, r, 1, nb, nsem).start(priority=1)
        w = rw_ref[...]
        w1, w2 = w[:, 0:1], w[:, 1:2]
        a_lo, a_hi = _unpack_halves(ya[...])
        b_lo, b_hi = _unpack_halves(yb[...])
        lo = h1_ref[:, 0:PACKED] + (w1 * a_lo + w2 * b_lo)
        hi = h1_ref[:, PACKED:D_MODEL] + (w1 * a_hi + w2 * b_hi)
        ms = (jnp.sum(lo * lo, axis=-1, keepdims=True)
              + jnp.sum(hi * hi, axis=-1, keepdims=True)) / D_MODEL
        scale = lax.rsqrt(ms + EPS)
        out_ref[:, 0:PACKED] = (lo * scale) * gf_ref[:, 0:PACKED]
        out_ref[:, PACKED:D_MODEL] = (hi * scale) * gf_ref[:, PACKED:D_MODEL]

    @pl.when(i == 0)
    def _():
        def first_rows(r, carry):
            fetch(0, r, 0, ya0, sem0).start()
            fetch(0, r, 1, yb0, sem0).start()
            return carry
        lax.fori_loop(0, TC, first_rows, 0)

    @pl.when(i % 2 == 0)
    def _():
        step(0, last=False)

    @pl.when((i % 2 == 1) & (i < N_CTILES - 1))
    def _():
        step(1, last=False)

    @pl.when(i == N_CTILES - 1)
    def _():
        step(1, last=True)


def _combine(dest, h1, rw, gf, y):
    row_buf = pltpu.VMEM((TC, PACKED), jnp.uint32)
    return pl.pallas_call(
        _combine_body,
        grid_spec=pltpu.PrefetchScalarGridSpec(
            num_scalar_prefetch=1,
            grid=(N_CTILES,),
            in_specs=[
                pl.BlockSpec((TC, D_MODEL), lambda i, d: (i, 0)),
                pl.BlockSpec((TC, LANES), lambda i, d: (i, 0)),
                pl.BlockSpec((1, D_MODEL), lambda i, d: (0, 0)),
                pl.BlockSpec(memory_space=pl.ANY),
            ],
            out_specs=pl.BlockSpec((TC, D_MODEL), lambda i, d: (i, 0)),
            scratch_shapes=[row_buf, row_buf, row_buf, row_buf,
                            pltpu.SemaphoreType.DMA, pltpu.SemaphoreType.DMA],
        ),
        out_shape=jax.ShapeDtypeStruct((N_TOK, D_MODEL), F32),
        compiler_params=pltpu.CompilerParams(
            dimension_semantics=("arbitrary",), vmem_limit_bytes=VMEM_LIMIT),
        name="combine",
    )(dest, h1, rw, gf, y)


def _route_tables(cnt, ri):
    counts = cnt[0, N_GROUPS:N_GROUPS + N_EXPERTS].astype(jnp.int32)
    padded = (counts + TB - 1) // TB * TB
    pad_end = jnp.cumsum(padded)
    pad_start = pad_end - padded
    experts = jnp.arange(N_EXPERTS, dtype=jnp.int32)
    start_of = jnp.sum(jnp.where(ri[:, 0:TOP_K, None] == experts, pad_start, 0), axis=-1)
    dest = (start_of + ri[:, TOP_K:2 * TOP_K]).reshape(N_ASSIGN)
    n_active = (pad_end[-1:] // TB).astype(jnp.int32)
    block_row = jnp.arange(N_BLOCKS, dtype=jnp.int32)[:, None] * TB
    block_e = jnp.minimum(jnp.sum((pad_end[None, :] <= block_row).astype(jnp.int32), axis=1), N_EXPERTS - 1)
    run_end = jnp.sum(jnp.where(block_e[:, None] == experts, pad_end, 0), axis=-1) // TB
    next_e = jnp.where(run_end < n_active[0], block_e[jnp.minimum(run_end, N_BLOCKS - 1)], -1)
    return dest, block_e, next_e.astype(jnp.int32), n_active


def kernel(x, norm1_g, w_in, b_gate, conv_w, ln_v_g, ln_v_b, sgu_w, sgu_b, w_branch, w_out, norm2_g,
           router_g, router_g_b, router_e, router_e_b, w_gate, w_up, w_down, final_g):
    h = x.reshape(N_TOK, D_MODEL)

    w_in_b = w_in.astype(BF16)
    w_branch_b = w_branch.astype(BF16)
    w_out_b = w_out.astype(BF16)
    g1 = norm1_g.reshape(DEPTH, 1, D_MODEL)
    g2 = norm2_g.reshape(DEPTH, 1, D_MODEL)
    gf = final_g.reshape(1, D_MODEL)
    ln_g = ln_v_g.reshape(DEPTH, 1, WIDTH)
    ln_b = ln_v_b.reshape(DEPTH, 1, WIDTH)
    bg = b_gate.reshape(DEPTH, 1, 2 * D_MODEL)
    sgu_bias = jnp.repeat(jnp.swapaxes(sgu_b, 1, 2), GROUP_DIM, axis=2)
    pad = LANES - N_GROUPS - N_EXPERTS
    w_r = jnp.concatenate([router_g, router_e, jnp.zeros((DEPTH, D_MODEL, pad), F32)], axis=2)
    wr_hi = w_r.astype(BF16)
    w_router = jnp.concatenate([wr_hi, (w_r - wr_hi.astype(F32)).astype(BF16)], axis=2)
    r_bias = jnp.concatenate([router_g_b, router_e_b, jnp.zeros((DEPTH, pad), F32)], axis=1)
    r_bias = r_bias.reshape(DEPTH, 1, LANES)

    moe = None
    for l in range(DEPTH):
        if moe is None:
            ya, yb = _mixer_in(l, TN, h, g1, w_in_b, conv_w, ln_g, ln_b, sgu_w, sgu_bias)
        else:
            ya, yb, h = _mixer_in(l, TN, None, g1, w_in_b, conv_w, ln_g, ln_b, sgu_w, sgu_bias, moe)
        h1, xp, ri, rw, cnt = _mixer_out(l, h, g1, ya, yb, w_in_b, bg, w_branch_b, w_out_b, g2,
                                         w_router, r_bias)
        dest, block_e, next_e, n_active = _route_tables(cnt, ri)
        y = _experts(l, block_e, next_e, n_active, _invert(dest), xp, w_gate, w_up, w_down)
        moe = (dest, h1, rw, y)
    return _combine(*moe[:3], gf, moe[3]).reshape(BATCH, SEQ, D_MODEL)
```

```python
import functools

import jax
import jax.numpy as jnp
from jax import lax
from jax.experimental import pallas as pl
from jax.experimental.pallas import tpu as pltpu

F32 = jnp.float32
BF16 = jnp.bfloat16

D_MODEL = 2048
BATCH = 4
SEQ = 4096
DEPTH = 2
N_TOK = BATCH * SEQ
WIDTH = D_MODEL // 2
CONV_K = 3
SGU_GROUPS = 8
CHUNK = 128
GROUP_DIM = WIDTH // SGU_GROUPS
N_GROUPS = 4
PER_GROUP = 8
N_EXPERTS = N_GROUPS * PER_GROUP
TOP_K = 2
N_ASSIGN = N_TOK * TOP_K
D_EXPERT = D_MODEL // 4
EPS = 1e-6

LANES = 128
SUBLANES = 8
VMEM_LIMIT = 56 * 1024 * 1024

COL_C, COL_B, COL_XA = 0, WIDTH, 2 * WIDTH
COL_U, COL_V = 3 * WIDTH, 4 * WIDTH
COL_GA, COL_GB = 5 * WIDTH, 5 * WIDTH + D_MODEL

TM = 512
TN = 512
TILES_PER_SEQ = SEQ // TM
J1 = D_MODEL // TN
J2 = D_MODEL // TN
J_OUT = J1 + J2

PACKED = D_MODEL // 2
assert (J2 // 2) * TN == PACKED
TB = 256
P_SLOTS = N_ASSIGN + N_EXPERTS * TB
N_BLOCKS = P_SLOTS // TB
BLOCKS_PER_STEP = 4
GATHER_AHEAD = 2
assert N_BLOCKS % BLOCKS_PER_STEP == 0 and GATHER_AHEAD < BLOCKS_PER_STEP
INVERT_UNROLL = 16
TC = 512
N_CTILES = N_TOK // TC
assert N_CTILES % 2 == 0


def _dot(a, b):
    return jnp.dot(a, b, preferred_element_type=F32)


def _pack_halves(lo, hi):
    lo_bits = pltpu.bitcast(lo.astype(BF16).astype(F32), jnp.uint32)
    hi_bits = pltpu.bitcast(hi.astype(BF16).astype(F32), jnp.uint32)
    return lax.shift_right_logical(lo_bits, jnp.uint32(16)) | (hi_bits & jnp.uint32(0xFFFF0000))


def _unpack_halves(words):
    lo = pltpu.bitcast(lax.shift_left(words, jnp.uint32(16)), F32)
    hi = pltpu.bitcast(words & jnp.uint32(0xFFFF0000), F32)
    return lo, hi


def _mixer_in_body(tn, fused, *refs):
    if fused:
        (dest_ref, h1_ref, rw_ref, y_ref, g1_ref, wc_ref, wb_ref, wa_ref, wu_ref, wv_ref, cw_ref, lng_ref,
         lnb_ref, sw_ref, sb_ref, ya_ref, yb_ref, h_ref, xn_scr, xc_scr, carry_scr, u_scr, v_scr,
         fa, fb, fsem) = refs
    else:
        (h_ref, g1_ref, wc_ref, wb_ref, wa_ref, wu_ref, wv_ref, cw_ref, lng_ref, lnb_ref, sw_ref, sb_ref,
         ya_ref, yb_ref, xn_scr, xc_scr, carry_scr, u_scr, v_scr) = refs
    jc = WIDTH // tn
    jz = WIDTH // tn
    n_tiles = N_TOK // TM
    rows_per_step = TM
    i = pl.program_id(0)
    j = pl.program_id(1)

    def fetch(tile, r, k):
        src = y_ref.at[pl.ds(dest_ref[(tile * TM + r) * TOP_K + k], 1)]
        return pltpu.make_async_copy(src, (fa, fb)[k].at[pl.ds(r, 1)], fsem)

    def wait_fetch():
        for dst in (fa, fb):
            pltpu.make_async_copy(y_ref.at[pl.ds(0, TM)], dst, fsem).wait()

    def issue_fetch(step):
        if fused:
            nxt = jnp.where(i + 1 == n_tiles, 0, i + 1)
            for r in range(step * rows_per_step, (step + 1) * rows_per_step):
                fetch(nxt, r, 0).start(priority=1)
                fetch(nxt, r, 1).start(priority=1)

    if fused:
        @pl.when((i == 0) & (j == 0))
        def _():
            def first_rows(r, carry):
                fetch(0, r, 0).start()
                fetch(0, r, 1).start()
                return carry
            lax.fori_loop(0, TM, first_rows, 0)

    @pl.when((i == 0) & (j == 0))
    def _():
        carry_scr[...] = jnp.zeros((jc, SUBLANES, tn), F32)

    @pl.when(j == 0)
    def _():
        if fused:
            wait_fetch()
            w = rw_ref[...]
            w1, w2 = w[:, 0:1], w[:, 1:2]
            a_lo, a_hi = _unpack_halves(fa[...])
            b_lo, b_hi = _unpack_halves(fb[...])
            halves = (h1_ref[:, 0:PACKED] + (w1 * a_lo + w2 * b_lo),
                      h1_ref[:, PACKED:D_MODEL] + (w1 * a_hi + w2 * b_hi))
            ms = sum(jnp.sum(v * v, axis=-1, keepdims=True) for v in halves) / D_MODEL
            scale = lax.rsqrt(ms + EPS)
            for c, v in enumerate(halves):
                cols = slice(c * PACKED, (c + 1) * PACKED)
                h_ref[:, cols] = v
                xn_scr[:, cols] = ((v * scale) * g1_ref[:, cols]).astype(BF16)
        else:
            x = h_ref[...]
            ms = jnp.mean(x * x, axis=-1, keepdims=True)
            xn_scr[...] = ((x * lax.rsqrt(ms + EPS)) * g1_ref[...]).astype(BF16)

    def conv_step(jv):
        if jv == 0:
            issue_fetch(jv)
        xn = xn_scr[...]
        xc = _dot(xn, wc_ref[...]) * _dot(xn, wa_ref[...])

        xc_scr[0:SUBLANES, :] = jnp.where(i % TILES_PER_SEQ == 0, 0.0, carry_scr[jv])
        xc_scr[SUBLANES:SUBLANES + TM, :] = xc
        x1 = xc_scr[SUBLANES - 1:SUBLANES - 1 + TM, :]
        x2 = xc_scr[SUBLANES - 2:SUBLANES - 2 + TM, :]
        cw = cw_ref[...]
        conv = cw[0:1, :] * x2 + cw[1:2, :] * x1 + cw[2:3, :] * xc
        carry_scr[jv] = xc_scr[TM:TM + SUBLANES, :]
        ya_ref[...] = (_dot(xn, wb_ref[...]) * conv).astype(BF16)

    def gating_step(jv):
        xn = xn_scr[...]
        u_scr[jv] = jax.nn.gelu(_dot(xn, wu_ref[...]))
        v_scr[jv] = jax.nn.gelu(_dot(xn, wv_ref[...]))

    for jv in range(jc):
        pl.when(j == jv)(functools.partial(conv_step, jv))
    for jv in range(jz):
        pl.when(j == jc + jv)(functools.partial(gating_step, jv))

    @pl.when(j == jc + jz - 1)
    def _():
        s1 = jnp.zeros((TM, 1), F32)
        for k in range(jz):
            s1 = s1 + jnp.sum(v_scr[k], axis=-1, keepdims=True)
        mu = s1 / WIDTH
        s2 = jnp.zeros((TM, 1), F32)
        for k in range(jz):
            d = v_scr[k] - mu
            s2 = s2 + jnp.sum(d * d, axis=-1, keepdims=True)
        rstd = lax.rsqrt(s2 / WIDTH + EPS)
        row = lax.broadcasted_iota(jnp.int32, (CHUNK, CHUNK), 0)
        col = lax.broadcasted_iota(jnp.int32, (CHUNK, CHUNK), 1)
        causal = col <= row
        gpt = tn // GROUP_DIM
        for k in range(jz):
            vn = (((v_scr[k] - mu) * rstd) * lng_ref[:, k * tn:(k + 1) * tn]
                  + lnb_ref[:, k * tn:(k + 1) * tn]).astype(BF16)
            for gl in range(gpt):
                g = k * gpt + gl
                w = jnp.where(causal, sw_ref[g], 0.0).astype(BF16)
                bias = sb_ref[:, g * GROUP_DIM:(g + 1) * GROUP_DIM]
                for n in range(TM // CHUNK):
                    rows = slice(n * CHUNK, (n + 1) * CHUNK)
                    mixed = _dot(w, vn[rows, gl * GROUP_DIM:(gl + 1) * GROUP_DIM])
                    u = u_scr[k, rows, gl * GROUP_DIM:(gl + 1) * GROUP_DIM]
                    yb_ref[rows, g * GROUP_DIM:(g + 1) * GROUP_DIM] = (u * (mixed + bias)).astype(BF16)

    if fused:
        @pl.when((i == n_tiles - 1) & (j == jc + jz - 1))
        def _():
            wait_fetch()


def _mixer_in(l, tn, h, g1, w_in, conv_w, ln_g, ln_b, sgu_w, sgu_bias, moe=None):
    fused = moe is not None
    jc = jz = WIDTH // tn
    cj = lambda j: jnp.minimum(j, jc - 1)
    zj = lambda j: jnp.clip(j - jc, 0, jz - 1)
    wspec = lambda off, f: pl.BlockSpec((None, D_MODEL, tn), lambda i, j, *_: (l, 0, off // tn + f(j)))
    row_tile = pl.BlockSpec((TM, D_MODEL), lambda i, j, *_: (i, 0))
    in_specs = [
        pl.BlockSpec((None, 1, D_MODEL), lambda i, j, *_: (l, 0, 0)),
        wspec(COL_C, cj), wspec(COL_B, cj), wspec(COL_XA, cj), wspec(COL_U, zj), wspec(COL_V, zj),
        pl.BlockSpec((None, CONV_K, tn), lambda i, j, *_: (l, 0, cj(j))),
        pl.BlockSpec((None, 1, WIDTH), lambda i, j, *_: (l, 0, 0)),
        pl.BlockSpec((None, 1, WIDTH), lambda i, j, *_: (l, 0, 0)),
        pl.BlockSpec((None, SGU_GROUPS, CHUNK, CHUNK), lambda i, j, *_: (l, 0, 0, 0)),
        pl.BlockSpec((None, CHUNK, WIDTH), lambda i, j, *_: (l, 0, 0)),
    ]
    out_specs = [
        pl.BlockSpec((TM, tn), lambda i, j, *_: (i, cj(j))),
        pl.BlockSpec((TM, WIDTH), lambda i, j, *_: (i, 0)),
    ]
    out_shape = [jax.ShapeDtypeStruct((N_TOK, WIDTH), BF16), jax.ShapeDtypeStruct((N_TOK, WIDTH), BF16)]
    scratch = [
        pltpu.VMEM((TM, D_MODEL), BF16),
        pltpu.VMEM((TM + SUBLANES, tn), F32),
        pltpu.VMEM((jc, SUBLANES, tn), F32),
        pltpu.VMEM((jz, TM, tn), F32),
        pltpu.VMEM((jz, TM, tn), F32),
    ]
    weights = (g1, w_in, w_in, w_in, w_in, w_in, conv_w, ln_g, ln_b, sgu_w, sgu_bias)
    if fused:
        dest, h1, rw, y = moe
        in_specs = [row_tile, pl.BlockSpec((TM, LANES), lambda i, j, *_: (i, 0)),
                    pl.BlockSpec(memory_space=pl.ANY)] + in_specs
        out_specs.append(row_tile)
        out_shape.append(jax.ShapeDtypeStruct((N_TOK, D_MODEL), F32))
        scratch += [pltpu.VMEM((TM, PACKED), jnp.uint32), pltpu.VMEM((TM, PACKED), jnp.uint32),
                    pltpu.SemaphoreType.DMA]
        operands = (dest, h1, rw, y) + weights
    else:
        in_specs = [row_tile] + in_specs
        operands = (h,) + weights
    return pl.pallas_call(
        functools.partial(_mixer_in_body, tn, fused),
        grid_spec=pltpu.PrefetchScalarGridSpec(
            num_scalar_prefetch=1 if fused else 0,
            grid=(N_TOK // TM, jc + jz),
            in_specs=in_specs, out_specs=out_specs, scratch_shapes=scratch),
        out_shape=out_shape,
        compiler_params=pltpu.CompilerParams(
            dimension_semantics=("arbitrary", "arbitrary"), vmem_limit_bytes=VMEM_LIMIT),
        name="mixer_in",
    )(*operands)


def _mixer_out_body(h_ref, g1_ref, ya_ref, yb_ref, wga_ref, wgb_ref, bga_ref, bgb_ref, wba_ref, wbb_ref,
                    wo_ref, g2_ref, wr_ref, rb_ref,
                    h1_ref, xp_ref, ri_ref, rw_ref, cnt_ref,
                    xn_scr, mg_scr, out_scr, cnt_scr):
    i = pl.program_id(0)
    j = pl.program_id(1)

    @pl.when((i == 0) & (j == 0))
    def _():
        cnt_scr[...] = jnp.zeros((1, LANES), F32)

    def merge():
        xn = xn_scr[...]
        ga = jax.nn.sigmoid(_dot(xn, wga_ref[...]) + bga_ref[...])
        gb = jax.nn.sigmoid(_dot(xn, wgb_ref[...]) + bgb_ref[...])
        mg_scr[j] = (ga * _dot(ya_ref[...], wba_ref[...]) + gb * _dot(yb_ref[...], wbb_ref[...])).astype(BF16)

    @pl.when(j == 0)
    def _():
        x = h_ref[...]
        ms = jnp.mean(x * x, axis=-1, keepdims=True)
        xn_scr[...] = ((x * lax.rsqrt(ms + EPS)) * g1_ref[...]).astype(BF16)
        merge()

    pl.when((j > 0) & (j < J1))(merge)

    @pl.when(j >= J1)
    def _():
        acc = _dot(mg_scr[0], wo_ref[0:TN, :])
        for k in range(1, J1):
            acc = acc + _dot(mg_scr[k], wo_ref[k * TN:(k + 1) * TN, :])
        out_scr[j - J1] = acc

    @pl.when(j == J_OUT - 1)
    def _():
        ss = jnp.zeros((TM, 1), F32)
        for k in range(J2):
            cols = slice(k * TN, (k + 1) * TN)
            hk = h_ref[:, cols] + out_scr[k]
            h1_ref[:, cols] = hk
            ss = ss + jnp.sum(hk * hk, axis=-1, keepdims=True)
        rstd = lax.rsqrt(ss / D_MODEL + EPS)

        acc = jnp.zeros((TM, 2 * LANES), F32)
        for k in range(J2):
            cols = slice(k * TN, (k + 1) * TN)
            xk = (h1_ref[:, cols] * rstd) * g2_ref[:, cols]
            hi = xk.astype(BF16)
            lo = (xk - hi.astype(F32)).astype(BF16)
            acc = acc + (_dot(hi, wr_ref[cols, :]) + _dot(lo, wr_ref[cols, :]))
            bits = pltpu.bitcast(hi.astype(F32), jnp.uint32)
            if k < J2 // 2:
                xp_ref[:, cols] = lax.shift_right_logical(bits, jnp.uint32(16))
            else:
                pcols = slice(k * TN - PACKED, (k + 1) * TN - PACKED)
                xp_ref[:, pcols] = xp_ref[:, pcols] | (bits & jnp.uint32(0xFFFF0000))
        lg = (acc[:, 0:LANES] + acc[:, LANES:2 * LANES]) + rb_ref[...]

        lane_i = lax.broadcasted_iota(jnp.int32, (TM, LANES), 1)
        lane = lane_i.astype(F32)
        neg = -jnp.inf
        big = float(LANES)
        is_g = lane_i < N_GROUPS
        gl = jnp.where(is_g, lg, neg)
        gmax = jnp.max(gl, axis=-1, keepdims=True)
        g_idx = jnp.min(jnp.where(gl == gmax, lane, big), axis=-1, keepdims=True)
        g_w = 1.0 / jnp.sum(jnp.exp(gl - gmax), axis=-1, keepdims=True)

        first = N_GROUPS + g_idx * PER_GROUP
        in_grp = (lane >= first) & (lane < first + PER_GROUP)
        el = jnp.where(in_grp, lg, neg)
        m1 = jnp.max(el, axis=-1, keepdims=True)
        i1 = jnp.min(jnp.where(in_grp & (el == m1), lane, big), axis=-1, keepdims=True)
        el2 = jnp.where(lane == i1, neg, el)
        m2 = jnp.max(el2, axis=-1, keepdims=True)
        i2 = jnp.min(jnp.where(in_grp & (lane != i1) & (el2 == m2), lane, big), axis=-1, keepdims=True)
        t = jnp.exp(m2 - m1)
        w1 = g_w * (1.0 / (1.0 + t))
        w2 = g_w * (t / (1.0 + t))

        onehot = jnp.where((lane == i1) | (lane == i2), 1.0, 0.0)
        r_i = lax.broadcasted_iota(jnp.int32, (TM, TM), 0)
        c_i = lax.broadcasted_iota(jnp.int32, (TM, TM), 1)
        tri = jnp.where(c_i < r_i, 1.0, 0.0).astype(BF16)
        before = cnt_scr[...] + _dot(tri, onehot.astype(BF16))
        rank1 = jnp.sum(jnp.where(lane == i1, before, 0.0), axis=-1, keepdims=True)
        rank2 = jnp.sum(jnp.where(lane == i2, before, 0.0), axis=-1, keepdims=True)
        cnt_scr[...] = cnt_scr[...] + jnp.sum(onehot, axis=0, keepdims=True)

        ri = jnp.where(lane_i == 0, i1 - N_GROUPS,
                       jnp.where(lane_i == 1, i2 - N_GROUPS,
                                 jnp.where(lane_i == 2, rank1, jnp.where(lane_i == 3, rank2, 0.0))))
        ri_ref[...] = ri.astype(jnp.int32)
        rw_ref[...] = jnp.where(lane_i == 0, w1, jnp.where(lane_i == 1, w2, 0.0))
        cnt_ref[...] = jnp.broadcast_to(cnt_scr[...], (SUBLANES, LANES))


def _mixer_out(l, h, g1, ya, yb, w_in, b_gate, w_branch, w_out, g2, w_router, r_bias):
    mj = lambda j: jnp.minimum(j, J1 - 1)
    oj = lambda j: jnp.clip(j - J1, 0, J2 - 1)
    const = lambda *blk: pl.BlockSpec((None,) + blk, lambda i, j: (l,) + (0,) * len(blk))
    return pl.pallas_call(
        _mixer_out_body,
        grid=(N_TOK // TM, J_OUT),
        in_specs=[
            pl.BlockSpec((TM, D_MODEL), lambda i, j: (i, 0)),
            const(1, D_MODEL),
            pl.BlockSpec((TM, WIDTH), lambda i, j: (i, 0)),
            pl.BlockSpec((TM, WIDTH), lambda i, j: (i, 0)),
            pl.BlockSpec((None, D_MODEL, TN), lambda i, j: (l, 0, COL_GA // TN + mj(j))),
            pl.BlockSpec((None, D_MODEL, TN), lambda i, j: (l, 0, COL_GB // TN + mj(j))),
            pl.BlockSpec((None, 1, TN), lambda i, j: (l, 0, mj(j))),
            pl.BlockSpec((None, 1, TN), lambda i, j: (l, 0, D_MODEL // TN + mj(j))),
            pl.BlockSpec((None, None, WIDTH, TN), lambda i, j: (l, 0, 0, mj(j))),
            pl.BlockSpec((None, None, WIDTH, TN), lambda i, j: (l, 1, 0, mj(j))),
            pl.BlockSpec((None, D_MODEL, TN), lambda i, j: (l, 0, oj(j))),
            const(1, D_MODEL),
            const(D_MODEL, 2 * LANES), const(1, LANES),
        ],
        out_specs=[
            pl.BlockSpec((TM, D_MODEL), lambda i, j: (i, 0)),
            pl.BlockSpec((TM, PACKED), lambda i, j: (i, 0)),
            pl.BlockSpec((TM, LANES), lambda i, j: (i, 0)),
            pl.BlockSpec((TM, LANES), lambda i, j: (i, 0)),
            pl.BlockSpec((SUBLANES, LANES), lambda i, j: (0, 0)),
        ],
        out_shape=[jax.ShapeDtypeStruct((N_TOK, D_MODEL), F32),
                   jax.ShapeDtypeStruct((N_TOK, PACKED), jnp.uint32),
                   jax.ShapeDtypeStruct((N_TOK, LANES), jnp.int32),
                   jax.ShapeDtypeStruct((N_TOK, LANES), F32),
                   jax.ShapeDtypeStruct((SUBLANES, LANES), F32)],
        scratch_shapes=[
            pltpu.VMEM((TM, D_MODEL), BF16),
            pltpu.VMEM((J1, TM, TN), BF16),
            pltpu.VMEM((J2, TM, TN), F32),
            pltpu.VMEM((1, LANES), F32),
        ],
        compiler_params=pltpu.CompilerParams(
            dimension_semantics=("arbitrary", "arbitrary"), vmem_limit_bytes=VMEM_LIMIT),
        name="mixer_out",
    )(h, g1, ya, yb, w_in, w_in, b_gate, b_gate, w_branch, w_branch, w_out, g2, w_router, r_bias)


def _invert_body(first_ref, second_ref, spare_ref, inv_ref, sem):
    fill = pltpu.make_async_copy(spare_ref, inv_ref, sem)
    fill.start()
    fill.wait()

    def place(c, carry):
        t0 = c * INVERT_UNROLL
        for u in range(INVERT_UNROLL):
            inv_ref[first_ref[t0 + u]] = t0 + u
            inv_ref[second_ref[t0 + u]] = t0 + u
        return carry

    lax.fori_loop(0, N_TOK // INVERT_UNROLL, place, 0)


def _invert(dest):
    spare = jnp.arange(P_SLOTS, dtype=jnp.int32) % N_TOK
    smem = pl.BlockSpec(memory_space=pltpu.SMEM)
    return pl.pallas_call(
        _invert_body,
        in_specs=[smem, smem, pl.BlockSpec(memory_space=pl.ANY)],
        out_specs=pl.BlockSpec(memory_space=pltpu.SMEM),
        out_shape=jax.ShapeDtypeStruct((P_SLOTS,), jnp.int32),
        scratch_shapes=[pltpu.SemaphoreType.DMA],
        name="invert",
    )(dest[0::TOP_K], dest[1::TOP_K], spare)


def _experts_body(l, be_ref, nx_ref, na_ref, inv_ref, xp_ref, wg_ref, wu_ref, wd_ref, y_ref,
                  xbuf0, xbuf1, xbuf2, xbuf3, gsem0, gsem1, gsem2, gsem3,
                  wg_f, wu_f, wd_f, wsem, wslot, wg_s, wu_s, wd_s):
    n_act = na_ref[0]
    bufs = ((xbuf0, gsem0), (xbuf1, gsem1), (xbuf2, gsem2), (xbuf3, gsem3))

    def weight_copies(e, s):
        return [pltpu.make_async_copy(src.at[l, e], dst.at[s], wsem.at[s])
                for src, dst in ((wg_ref, wg_f), (wu_ref, wu_f), (wd_ref, wd_f))]

    def gather(blk, r, xdst, sem):
        return pltpu.make_async_copy(
            xp_ref.at[pl.ds(inv_ref[blk * TB + r], 1)], xdst.at[pl.ds(r, 1)], sem)

    def wait_gather(xdst, sem):
        pltpu.make_async_copy(xp_ref.at[pl.ds(0, TB)], xdst, sem).wait()

    def step(b, par):
        xcur, gcur = bufs[par]
        xnxt, gnxt = bufs[(par + GATHER_AHEAD) % BLOCKS_PER_STEP]
        wait_gather(xcur, gcur)

        @pl.when((b == 0) | (be_ref[b] != be_ref[jnp.maximum(b - 1, 0)]))
        def _():
            s = wslot[0]
            for c in weight_copies(be_ref[b], s):
                c.wait()
            wg_s[...] = wg_f[s].astype(BF16)
            wu_s[...] = wu_f[s].astype(BF16)
            wd_s[...] = wd_f[s].astype(BF16)

            @pl.when(nx_ref[b] >= 0)
            def _():
                for c in weight_copies(nx_ref[b], 1 - s):
                    c.start(priority=1)

            wslot[0] = 1 - s

        ahead = jnp.minimum(b + GATHER_AHEAD, N_BLOCKS - 1)
        for r in range(TB):
            gather(ahead, r, xnxt, gnxt).start()
        x_lo, x_hi = _unpack_halves(xcur[...])
        x_lo, x_hi = x_lo.astype(BF16), x_hi.astype(BF16)
        gate = _dot(x_lo, wg_s[0:PACKED, :]) + _dot(x_hi, wg_s[PACKED:D_MODEL, :])
        up = _dot(x_lo, wu_s[0:PACKED, :]) + _dot(x_hi, wu_s[PACKED:D_MODEL, :])
        hm = (jax.nn.silu(gate) * up).astype(BF16)
        y_ref[par * TB:(par + 1) * TB, :] = _pack_halves(
            _dot(hm, wd_s[:, 0:PACKED]), _dot(hm, wd_s[:, PACKED:D_MODEL]))

    @pl.when(pl.program_id(0) == 0)
    def _():
        def first_rows(r, carry):
            for k in range(GATHER_AHEAD):
                gather(k, r, *bufs[k]).start()
            return carry
        lax.fori_loop(0, TB, first_rows, 0)
        wslot[0] = 0
        for c in weight_copies(be_ref[0], 0):
            c.start(priority=1)

    for par in range(BLOCKS_PER_STEP):
        b = BLOCKS_PER_STEP * pl.program_id(0) + par
        pl.when(b < n_act)(functools.partial(step, b, par))

        @pl.when((b >= n_act) & (b < n_act + GATHER_AHEAD))
        def _():
            wait_gather(*bufs[par])

        @pl.when((b >= n_act) & (b < N_BLOCKS))
        def _():
            y_ref[par * TB:(par + 1) * TB, :] = jnp.zeros((TB, PACKED), jnp.uint32)


def _experts(l, block_e, next_e, n_active, inv, xp, w_gate, w_up, w_down):
    row_buf = pltpu.VMEM((TB, PACKED), jnp.uint32)
    in_out = (D_MODEL, D_EXPERT)
    out_in = (D_EXPERT, D_MODEL)
    n_steps = N_BLOCKS // BLOCKS_PER_STEP
    return pl.pallas_call(
        functools.partial(_experts_body, l),
        grid_spec=pltpu.PrefetchScalarGridSpec(
            num_scalar_prefetch=4,
            grid=(n_steps + 1,),
            in_specs=[
                pl.BlockSpec(memory_space=pl.ANY),
                pl.BlockSpec(memory_space=pl.ANY),
                pl.BlockSpec(memory_space=pl.ANY),
                pl.BlockSpec(memory_space=pl.ANY),
            ],
            out_specs=pl.BlockSpec((BLOCKS_PER_STEP * TB, PACKED),
                                   lambda s, be, nx, na, iv: (jnp.minimum(s, n_steps - 1), 0)),
            scratch_shapes=[
                row_buf, row_buf, row_buf, row_buf,
                pltpu.SemaphoreType.DMA, pltpu.SemaphoreType.DMA,
                pltpu.SemaphoreType.DMA, pltpu.SemaphoreType.DMA,
                pltpu.VMEM((2,) + in_out, F32), pltpu.VMEM((2,) + in_out, F32), pltpu.VMEM((2,) + out_in, F32),
                pltpu.SemaphoreType.DMA((2,)),
                pltpu.SMEM((1,), jnp.int32),
                pltpu.VMEM(in_out, BF16), pltpu.VMEM(in_out, BF16), pltpu.VMEM(out_in, BF16),
            ],
        ),
        out_shape=jax.ShapeDtypeStruct((P_SLOTS, PACKED), jnp.uint32),
        compiler_params=pltpu.CompilerParams(
            dimension_semantics=("arbitrary",), vmem_limit_bytes=VMEM_LIMIT),
        name="experts",
    )(block_e, next_e, n_active, inv, xp, w_gate, w_up, w_down)


def _combine_body(dest_ref, h1_ref, rw_ref, gf_ref, y_ref, out_ref,
                  ya0, yb0, ya1, yb1, sem0, sem1):
    i = pl.program_id(0)
    bufs = ((ya0, yb0, sem0), (ya1, yb1, sem1))

    def fetch(tile, r, k, dst, sem):
        slot = dest_ref[(tile * TC + r) * TOP_K + k]
        return pltpu.make_async_copy(y_ref.at[pl.ds(slot, 1)], dst.at[pl.ds(r, 1)], sem)

    def step(par, last):
        ya, yb, sem = bufs[par]
        for dst in (ya, yb):
            pltpu.make_async_copy(y_ref.at[pl.ds(0, TC)], dst, sem).wait()
        if not last:
            na, nb, nsem = bufs[1 - par]
            for r in range(TC):
                fetch(i + 1, r, 0, na, nsem).start()
                fetch(i + 1, r, 1, nb, nsem).start(priority=1)
        w = rw_ref[...]
        w1, w2 = w[:, 0:1], w[:, 1:2]
        a_lo, a_hi = _unpack_halves(ya[...])
        b_lo, b_hi = _unpack_halves(yb[...])
        lo = h1_ref[:, 0:PACKED] + (w1 * a_lo + w2 * b_lo)
        hi = h1_ref[:, PACKED:D_MODEL] + (w1 * a_hi + w2 * b_hi)
        ms = (jnp.sum(lo * lo, axis=-1, keepdims=True)
              + jnp.sum(hi * hi, axis=-1, keepdims=True)) / D_MODEL
        scale = lax.rsqrt(ms + EPS)
        out_ref[:, 0:PACKED] = (lo * scale) * gf_ref[:, 0:PACKED]
        out_ref[:, PACKED:D_MODEL] = (hi * scale) * gf_ref[:, PACKED:D_MODEL]

    @pl.when(i == 0)
    def _():
        def first_rows(r, carry):
            fetch(0, r, 0, ya0, sem0).start()
            fetch(0, r, 1, yb0, sem0).start()
            return carry
        lax.fori_loop(0, TC, first_rows, 0)

    @pl.when(i % 2 == 0)
    def _():
        step(0, last=False)

    @pl.when((i % 2 == 1) & (i < N_CTILES - 1))
    def _():
        step(1, last=False)

    @pl.when(i == N_CTILES - 1)
    def _():
        step(1, last=True)


def _combine(dest, h1, rw, gf, y):
    row_buf = pltpu.VMEM((TC, PACKED), jnp.uint32)
    return pl.pallas_call(
        _combine_body,
        grid_spec=pltpu.PrefetchScalarGridSpec(
            num_scalar_prefetch=1,
            grid=(N_CTILES,),
            in_specs=[
                pl.BlockSpec((TC, D_MODEL), lambda i, d: (i, 0)),
                pl.BlockSpec((TC, LANES), lambda i, d: (i, 0)),
                pl.BlockSpec((1, D_MODEL), lambda i, d: (0, 0)),
                pl.BlockSpec(memory_space=pl.ANY),
            ],
            out_specs=pl.BlockSpec((TC, D_MODEL), lambda i, d: (i, 0)),
            scratch_shapes=[row_buf, row_buf, row_buf, row_buf,
                            pltpu.SemaphoreType.DMA, pltpu.SemaphoreType.DMA],
        ),
        out_shape=jax.ShapeDtypeStruct((N_TOK, D_MODEL), F32),
        compiler_params=pltpu.CompilerParams(
            dimension_semantics=("arbitrary",), vmem_limit_bytes=VMEM_LIMIT),
        name="combine",
    )(dest, h1, rw, gf, y)


def _route_tables(cnt, ri):
    counts = cnt[0, N_GROUPS:N_GROUPS + N_EXPERTS].astype(jnp.int32)
    padded = (counts + TB - 1) // TB * TB
    pad_end = jnp.cumsum(padded)
    pad_start = pad_end - padded
    experts = jnp.arange(N_EXPERTS, dtype=jnp.int32)
    start_of = jnp.sum(jnp.where(ri[:, 0:TOP_K, None] == experts, pad_start, 0), axis=-1)
    dest = (start_of + ri[:, TOP_K:2 * TOP_K]).reshape(N_ASSIGN)
    n_active = (pad_end[-1:] // TB).astype(jnp.int32)
    block_row = jnp.arange(N_BLOCKS, dtype=jnp.int32)[:, None] * TB
    block_e = jnp.minimum(jnp.sum((pad_end[None, :] <= block_row).astype(jnp.int32), axis=1), N_EXPERTS - 1)
    run_end = jnp.sum(jnp.where(block_e[:, None] == experts, pad_end, 0), axis=-1) // TB
    next_e = jnp.where(run_end < n_active[0], block_e[jnp.minimum(run_end, N_BLOCKS - 1)], -1)
    return dest, block_e, next_e.astype(jnp.int32), n_active


def kernel(x, norm1_g, w_in, b_gate, conv_w, ln_v_g, ln_v_b, sgu_w, sgu_b, w_branch, w_out, norm2_g,
           router_g, router_g_b, router_e, router_e_b, w_gate, w_up, w_down, final_g):
    h = x.reshape(N_TOK, D_MODEL)

    w_in_b = w_in.astype(BF16)
    w_branch_b = w_branch.astype(BF16)
    w_out_b = w_out.astype(BF16)
    g1 = norm1_g.reshape(DEPTH, 1, D_MODEL)
    g2 = norm2_g.reshape(DEPTH, 1, D_MODEL)
    gf = final_g.reshape(1, D_MODEL)
    ln_g = ln_v_g.reshape(DEPTH, 1, WIDTH)
    ln_b = ln_v_b.reshape(DEPTH, 1, WIDTH)
    bg = b_gate.reshape(DEPTH, 1, 2 * D_MODEL)
    sgu_bias = jnp.repeat(jnp.swapaxes(sgu_b, 1, 2), GROUP_DIM, axis=2)
    pad = LANES - N_GROUPS - N_EXPERTS
    w_r = jnp.concatenate([router_g, router_e, jnp.zeros((DEPTH, D_MODEL, pad), F32)], axis=2)
    wr_hi = w_r.astype(BF16)
    w_router = jnp.concatenate([wr_hi, (w_r - wr_hi.astype(F32)).astype(BF16)], axis=2)
    r_bias = jnp.concatenate([router_g_b, router_e_b, jnp.zeros((DEPTH, pad), F32)], axis=1)
    r_bias = r_bias.reshape(DEPTH, 1, LANES)

    moe = None
    for l in range(DEPTH):
        if moe is None:
            ya, yb = _mixer_in(l, TN, h, g1, w_in_b, conv_w, ln_g, ln_b, sgu_w, sgu_bias)
        else:
            ya, yb, h = _mixer_in(l, TN, None, g1, w_in_b, conv_w, ln_g, ln_b, sgu_w, sgu_bias, moe)
        h1, xp, ri, rw, cnt = _mixer_out(l, h, g1, ya, yb, w_in_b, bg, w_branch_b, w_out_b, g2,
                                         w_router, r_bias)
        dest, block_e, next_e, n_active = _route_tables(cnt, ri)
        y = _experts(l, block_e, next_e, n_active, _invert(dest), xp, w_gate, w_up, w_down)
        moe = (dest, h1, rw, y)
    return _combine(*moe[:3], gf, moe[3]).reshape(BATCH, SEQ, D_MODEL)
```

```python
import functools

import jax
import jax.numpy as jnp
from jax import lax
from jax.experimental import pallas as pl
from jax.experimental.pallas import tpu as pltpu

F32 = jnp.float32
BF16 = jnp.bfloat16

D_MODEL = 2048
BATCH = 4
SEQ = 4096
DEPTH = 2
N_TOK = BATCH * SEQ
WIDTH = D_MODEL // 2
CONV_K = 3
SGU_GROUPS = 8
CHUNK = 128
GROUP_DIM = WIDTH // SGU_GROUPS
N_GROUPS = 4
PER_GROUP = 8
N_EXPERTS = N_GROUPS * PER_GROUP
TOP_K = 2
N_ASSIGN = N_TOK * TOP_K
D_EXPERT = D_MODEL // 4
EPS = 1e-6

LANES = 128
SUBLANES = 8
VMEM_LIMIT = 56 * 1024 * 1024

COL_C, COL_B, COL_XA = 0, WIDTH, 2 * WIDTH
COL_U, COL_V = 3 * WIDTH, 4 * WIDTH
COL_GA, COL_GB = 5 * WIDTH, 5 * WIDTH + D_MODEL

TM = 512
TN = 512
TILES_PER_SEQ = SEQ // TM
J1 = D_MODEL // TN
J2 = D_MODEL // TN
J_OUT = J1 + J2

PACKED = D_MODEL // 2
assert (J2 // 2) * TN == PACKED
TB = 256
P_SLOTS = N_ASSIGN + N_EXPERTS * TB
N_BLOCKS = P_SLOTS // TB
BLOCKS_PER_STEP = 4
GATHER_AHEAD = 2
assert N_BLOCKS % BLOCKS_PER_STEP == 0 and GATHER_AHEAD < BLOCKS_PER_STEP
INVERT_UNROLL = 16
TC = 512
N_CTILES = N_TOK // TC
assert N_CTILES % 2 == 0


def _dot(a, b):
    return jnp.dot(a, b, preferred_element_type=F32)


def _pack_halves(lo, hi):
    lo_bits = pltpu.bitcast(lo.astype(BF16).astype(F32), jnp.uint32)
    hi_bits = pltpu.bitcast(hi.astype(BF16).astype(F32), jnp.uint32)
    return lax.shift_right_logical(lo_bits, jnp.uint32(16)) | (hi_bits & jnp.uint32(0xFFFF0000))


def _unpack_halves(words):
    lo = pltpu.bitcast(lax.shift_left(words, jnp.uint32(16)), F32)
    hi = pltpu.bitcast(words & jnp.uint32(0xFFFF0000), F32)
    return lo, hi


def _mixer_in_body(tn, fused, *refs):
    if fused:
        (dest_ref, h1_ref, rw_ref, y_ref, g1_ref, wc_ref, wb_ref, wa_ref, wu_ref, wv_ref, cw_ref, lng_ref,
         lnb_ref, sw_ref, sb_ref, ya_ref, yb_ref, h_ref, xn_scr, xc_scr, carry_scr, u_scr, v_scr,
         fa, fb, fsem) = refs
    else:
        (h_ref, g1_ref, wc_ref, wb_ref, wa_ref, wu_ref, wv_ref, cw_ref, lng_ref, lnb_ref, sw_ref, sb_ref,
         ya_ref, yb_ref, xn_scr, xc_scr, carry_scr, u_scr, v_scr) = refs
    jc = WIDTH // tn
    jz = WIDTH // tn
    n_tiles = N_TOK // TM
    rows_per_step = TM
    i = pl.program_id(0)
    j = pl.program_id(1)

    def fetch(tile, r, k):
        src = y_ref.at[pl.ds(dest_ref[(tile * TM + r) * TOP_K + k], 1)]
        return pltpu.make_async_copy(src, (fa, fb)[k].at[pl.ds(r, 1)], fsem)

    def wait_fetch():
        for dst in (fa, fb):
            pltpu.make_async_copy(y_ref.at[pl.ds(0, TM)], dst, fsem).wait()

    def issue_fetch(step):
        if fused:
            nxt = jnp.where(i + 1 == n_tiles, 0, i + 1)
            for r in range(step * rows_per_step, (step + 1) * rows_per_step):
                fetch(nxt, r, 0).start(priority=1)
                fetch(nxt, r, 1).start(priority=1)

    if fused:
        @pl.when((i == 0) & (j == 0))
        def _():
            def first_rows(r, carry):
                fetch(0, r, 0).start()
                fetch(0, r, 1).start()
                return carry
            lax.fori_loop(0, TM, first_rows, 0)

    @pl.when((i == 0) & (j == 0))
    def _():
        carry_scr[...] = jnp.zeros((jc, SUBLANES, tn), F32)

    @pl.when(j == 0)
    def _():
        if fused:
            wait_fetch()
            w = rw_ref[...]
            w1, w2 = w[:, 0:1], w[:, 1:2]
            a_lo, a_hi = _unpack_halves(fa[...])
            b_lo, b_hi = _unpack_halves(fb[...])
            halves = (h1_ref[:, 0:PACKED] + (w1 * a_lo + w2 * b_lo),
                      h1_ref[:, PACKED:D_MODEL] + (w1 * a_hi + w2 * b_hi))
            ms = sum(jnp.sum(v * v, axis=-1, keepdims=True) for v in halves) / D_MODEL
            scale = lax.rsqrt(ms + EPS)
            for c, v in enumerate(halves):
                cols = slice(c * PACKED, (c + 1) * PACKED)
                h_ref[:, cols] = v
                xn_scr[:, cols] = ((v * scale) * g1_ref[:, cols]).astype(BF16)
        else:
            x = h_ref[...]
            ms = jnp.mean(x * x, axis=-1, keepdims=True)
            xn_scr[...] = ((x * lax.rsqrt(ms + EPS)) * g1_ref[...]).astype(BF16)

    def conv_step(jv):
        if jv == 0:
            issue_fetch(jv)
        xn = xn_scr[...]
        xc = _dot(xn, wc_ref[...]) * _dot(xn, wa_ref[...])

        xc_scr[0:SUBLANES, :] = jnp.where(i % TILES_PER_SEQ == 0, 0.0, carry_scr[jv])
        xc_scr[SUBLANES:SUBLANES + TM, :] = xc
        x1 = xc_scr[SUBLANES - 1:SUBLANES - 1 + TM, :]
        x2 = xc_scr[SUBLANES - 2:SUBLANES - 2 + TM, :]
        cw = cw_ref[...]
        conv = cw[0:1, :] * x2 + cw[1:2, :] * x1 + cw[2:3, :] * xc
        carry_scr[jv] = xc_scr[TM:TM + SUBLANES, :]
        ya_ref[...] = (_dot(xn, wb_ref[...]) * conv).astype(BF16)

    def gating_step(jv):
        xn = xn_scr[...]
        u_scr[jv] = jax.nn.gelu(_dot(xn, wu_ref[...]))
        v_scr[jv] = jax.nn.gelu(_dot(xn, wv_ref[...]))

    for jv in range(jc):
        pl.when(j == jv)(functools.partial(conv_step, jv))
    for jv in range(jz):
        pl.when(j == jc + jv)(functools.partial(gating_step, jv))

    @pl.when(j == jc + jz - 1)
    def _():
        s1 = jnp.zeros((TM, 1), F32)
        for k in range(jz):
            s1 = s1 + jnp.sum(v_scr[k], axis=-1, keepdims=True)
        mu = s1 / WIDTH
        s2 = jnp.zeros((TM, 1), F32)
        for k in range(jz):
            d = v_scr[k] - mu
            s2 = s2 + jnp.sum(d * d, axis=-1, keepdims=True)
        rstd = lax.rsqrt(s2 / WIDTH + EPS)
        row = lax.broadcasted_iota(jnp.int32, (CHUNK, CHUNK), 0)
        col = lax.broadcasted_iota(jnp.int32, (CHUNK, CHUNK), 1)
        causal = col <= row
        gpt = tn // GROUP_DIM
        for k in range(jz):
            vn = (((v_scr[k] - mu) * rstd) * lng_ref[:, k * tn:(k + 1) * tn]
                  + lnb_ref[:, k * tn:(k + 1) * tn]).astype(BF16)
            for gl in range(gpt):
                g = k * gpt + gl
                w = jnp.where(causal, sw_ref[g], 0.0).astype(BF16)
                bias = sb_ref[:, g * GROUP_DIM:(g + 1) * GROUP_DIM]
                for n in range(TM // CHUNK):
                    rows = slice(n * CHUNK, (n + 1) * CHUNK)
                    mixed = _dot(w, vn[rows, gl * GROUP_DIM:(gl + 1) * GROUP_DIM])
                    u = u_scr[k, rows, gl * GROUP_DIM:(gl + 1) * GROUP_DIM]
                    yb_ref[rows, g * GROUP_DIM:(g + 1) * GROUP_DIM] = (u * (mixed + bias)).astype(BF16)

    if fused:
        @pl.when((i == n_tiles - 1) & (j == jc + jz - 1))
        def _():
            wait_fetch()


def _mixer_in(l, tn, h, g1, w_in, conv_w, ln_g, ln_b, sgu_w, sgu_bias, moe=None):
    fused = moe is not None
    jc = jz = WIDTH // tn
    cj = lambda j: jnp.minimum(j, jc - 1)
    zj = lambda j: jnp.clip(j - jc, 0, jz - 1)
    wspec = lambda off, f: pl.BlockSpec((None, D_MODEL, tn), lambda i, j, *_: (l, 0, off // tn + f(j)))
    row_tile = pl.BlockSpec((TM, D_MODEL), lambda i, j, *_: (i, 0))
    in_specs = [
        pl.BlockSpec((None, 1, D_MODEL), lambda i, j, *_: (l, 0, 0)),
        wspec(COL_C, cj), wspec(COL_B, cj), wspec(COL_XA, cj), wspec(COL_U, zj), wspec(COL_V, zj),
        pl.BlockSpec((None, CONV_K, tn), lambda i, j, *_: (l, 0, cj(j))),
        pl.BlockSpec((None, 1, WIDTH), lambda i, j, *_: (l, 0, 0)),
        pl.BlockSpec((None, 1, WIDTH), lambda i, j, *_: (l, 0, 0)),
        pl.BlockSpec((None, SGU_GROUPS, CHUNK, CHUNK), lambda i, j, *_: (l, 0, 0, 0)),
        pl.BlockSpec((None, CHUNK, WIDTH), lambda i, j, *_: (l, 0, 0)),
    ]
    out_specs = [
        pl.BlockSpec((TM, tn), lambda i, j, *_: (i, cj(j))),
        pl.BlockSpec((TM, WIDTH), lambda i, j, *_: (i, 0)),
    ]
    out_shape = [jax.ShapeDtypeStruct((N_TOK, WIDTH), BF16), jax.ShapeDtypeStruct((N_TOK, WIDTH), BF16)]
    scratch = [
        pltpu.VMEM((TM, D_MODEL), BF16),
        pltpu.VMEM((TM + SUBLANES, tn), F32),
        pltpu.VMEM((jc, SUBLANES, tn), F32),
        pltpu.VMEM((jz, TM, tn), F32),
        pltpu.VMEM((jz, TM, tn), F32),
    ]
    weights = (g1, w_in, w_in, w_in, w_in, w_in, conv_w, ln_g, ln_b, sgu_w, sgu_bias)
    if fused:
        dest, h1, rw, y = moe
        in_specs = [row_tile, pl.BlockSpec((TM, LANES), lambda i, j, *_: (i, 0)),
                    pl.BlockSpec(memory_space=pl.ANY)] + in_specs
        out_specs.append(row_tile)
        out_shape.append(jax.ShapeDtypeStruct((N_TOK, D_MODEL), F32))
        scratch += [pltpu.VMEM((TM, PACKED), jnp.uint32), pltpu.VMEM((TM, PACKED), jnp.uint32),
                    pltpu.SemaphoreType.DMA]
        operands = (dest, h1, rw, y) + weights
    else:
        in_specs = [row_tile] + in_specs
        operands = (h,) + weights
    return pl.pallas_call(
        functools.partial(_mixer_in_body, tn, fused),
        grid_spec=pltpu.PrefetchScalarGridSpec(
            num_scalar_prefetch=1 if fused else 0,
            grid=(N_TOK // TM, jc + jz),
            in_specs=in_specs, out_specs=out_specs, scratch_shapes=scratch),
        out_shape=out_shape,
        compiler_params=pltpu.CompilerParams(
            dimension_semantics=("arbitrary", "arbitrary"), vmem_limit_bytes=VMEM_LIMIT),
        name="mixer_in",
    )(*operands)


def _mixer_out_body(h_ref, hc_ref, g1_ref, ya_ref, yb_ref, wga_ref, wgb_ref, bga_ref, bgb_ref, wba_ref, wbb_ref,
                    wo_ref, g2_ref, wr_ref, rb_ref,
                    h1_ref, xp_ref, ri_ref, rw_ref, cnt_ref,
                    xn_scr, mg_scr, out_scr, cnt_scr):
    i = pl.program_id(0)
    j = pl.program_id(1)

    @pl.when(j == 0)
    def _():
        x = h_ref[...]
        ms = jnp.mean(x * x, axis=-1, keepdims=True)
        xn_scr[...] = ((x * lax.rsqrt(ms + EPS)) * g1_ref[...]).astype(BF16)

    @pl.when((i == 0) & (j == 0))
    def _():
        cnt_scr[...] = jnp.zeros((1, LANES), F32)

    @pl.when(j < J1)
    def _():
        xn = xn_scr[...]
        ga = jax.nn.sigmoid(_dot(xn, wga_ref[...]) + bga_ref[...])
        gb = jax.nn.sigmoid(_dot(xn, wgb_ref[...]) + bgb_ref[...])
        mg_scr[j] = (ga * _dot(ya_ref[...], wba_ref[...]) + gb * _dot(yb_ref[...], wbb_ref[...])).astype(BF16)

    @pl.when(j >= J1)
    def _():
        acc = _dot(mg_scr[0], wo_ref[0:TN, :])
        for k in range(1, J1):
            acc = acc + _dot(mg_scr[k], wo_ref[k * TN:(k + 1) * TN, :])
        h1c = hc_ref[...] + acc
        h1_ref[...] = h1c
        out_scr[j - J1] = h1c

    @pl.when(j == J_OUT - 1)
    def _():
        ss = jnp.zeros((TM, 1), F32)
        for k in range(J2):
            hk = out_scr[k]
            ss = ss + jnp.sum(hk * hk, axis=-1, keepdims=True)
        rstd = lax.rsqrt(ss / D_MODEL + EPS)

        acc = jnp.zeros((TM, 2 * LANES), F32)
        for k in range(J2):
            cols = slice(k * TN, (k + 1) * TN)
            xk = (out_scr[k] * rstd) * g2_ref[:, cols]
            hi = xk.astype(BF16)
            lo = (xk - hi.astype(F32)).astype(BF16)
            acc = acc + (_dot(hi, wr_ref[cols, :]) + _dot(lo, wr_ref[cols, :]))
            bits = pltpu.bitcast(hi.astype(F32), jnp.uint32)
            if k < J2 // 2:
                xp_ref[:, cols] = lax.shift_right_logical(bits, jnp.uint32(16))
            else:
                pcols = slice(k * TN - PACKED, (k + 1) * TN - PACKED)
                xp_ref[:, pcols] = xp_ref[:, pcols] | (bits & jnp.uint32(0xFFFF0000))
        lg = (acc[:, 0:LANES] + acc[:, LANES:2 * LANES]) + rb_ref[...]

        lane_i = lax.broadcasted_iota(jnp.int32, (TM, LANES), 1)
        lane = lane_i.astype(F32)
        neg = -jnp.inf
        big = float(LANES)
        is_g = lane_i < N_GROUPS
        gl = jnp.where(is_g, lg, neg)
        gmax = jnp.max(gl, axis=-1, keepdims=True)
        g_idx = jnp.min(jnp.where(gl == gmax, lane, big), axis=-1, keepdims=True)
        g_w = 1.0 / jnp.sum(jnp.exp(gl - gmax), axis=-1, keepdims=True)

        first = N_GROUPS + g_idx * PER_GROUP
        in_grp = (lane >= first) & (lane < first + PER_GROUP)
        el = jnp.where(in_grp, lg, neg)
        m1 = jnp.max(el, axis=-1, keepdims=True)
        i1 = jnp.min(jnp.where(in_grp & (el == m1), lane, big), axis=-1, keepdims=True)
        el2 = jnp.where(lane == i1, neg, el)
        m2 = jnp.max(el2, axis=-1, keepdims=True)
        i2 = jnp.min(jnp.where(in_grp & (lane != i1) & (el2 == m2), lane, big), axis=-1, keepdims=True)
        t = jnp.exp(m2 - m1)
        w1 = g_w * (1.0 / (1.0 + t))
        w2 = g_w * (t / (1.0 + t))

        onehot = jnp.where((lane == i1) | (lane == i2), 1.0, 0.0)
        r_i = lax.broadcasted_iota(jnp.int32, (TM, TM), 0)
        c_i = lax.broadcasted_iota(jnp.int32, (TM, TM), 1)
        tri = jnp.where(c_i < r_i, 1.0, 0.0).astype(BF16)
        before = cnt_scr[...] + _dot(tri, onehot.astype(BF16))
        rank1 = jnp.sum(jnp.where(lane == i1, before, 0.0), axis=-1, keepdims=True)
        rank2 = jnp.sum(jnp.where(lane == i2, before, 0.0), axis=-1, keepdims=True)
        cnt_scr[...] = cnt_scr[...] + jnp.sum(onehot, axis=0, keepdims=True)

        ri = jnp.where(lane_i == 0, i1 - N_GROUPS,
                       jnp.where(lane_i == 1, i2 - N_GROUPS,
                                 jnp.where(lane_i == 2, rank1, jnp.where(lane_i == 3, rank2, 0.0))))
        ri_ref[...] = ri.astype(jnp.int32)
        rw_ref[...] = jnp.where(lane_i == 0, w1, jnp.where(lane_i == 1, w2, 0.0))
        cnt_ref[...] = jnp.broadcast_to(cnt_scr[...], (SUBLANES, LANES))


def _mixer_out(l, h, g1, ya, yb, w_in, b_gate, w_branch, w_out, g2, w_router, r_bias):
    mj = lambda j: jnp.minimum(j, J1 - 1)
    oj = lambda j: jnp.clip(j - J1, 0, J2 - 1)
    const = lambda *blk: pl.BlockSpec((None,) + blk, lambda i, j: (l,) + (0,) * len(blk))
    return pl.pallas_call(
        _mixer_out_body,
        grid=(N_TOK // TM, J_OUT),
        in_specs=[
            pl.BlockSpec((TM, D_MODEL), lambda i, j: (i, 0)),
            pl.BlockSpec((TM, TN), lambda i, j: (i, oj(j))),
            const(1, D_MODEL),
            pl.BlockSpec((TM, WIDTH), lambda i, j: (i, 0)),
            pl.BlockSpec((TM, WIDTH), lambda i, j: (i, 0)),
            pl.BlockSpec((None, D_MODEL, TN), lambda i, j: (l, 0, COL_GA // TN + mj(j))),
            pl.BlockSpec((None, D_MODEL, TN), lambda i, j: (l, 0, COL_GB // TN + mj(j))),
            pl.BlockSpec((None, 1, TN), lambda i, j: (l, 0, mj(j))),
            pl.BlockSpec((None, 1, TN), lambda i, j: (l, 0, D_MODEL // TN + mj(j))),
            pl.BlockSpec((None, None, WIDTH, TN), lambda i, j: (l, 0, 0, mj(j))),
            pl.BlockSpec((None, None, WIDTH, TN), lambda i, j: (l, 1, 0, mj(j))),
            pl.BlockSpec((None, D_MODEL, TN), lambda i, j: (l, 0, oj(j))),
            const(1, D_MODEL),
            const(D_MODEL, 2 * LANES), const(1, LANES),
        ],
        out_specs=[
            pl.BlockSpec((TM, TN), lambda i, j: (i, oj(j))),
            pl.BlockSpec((TM, PACKED), lambda i, j: (i, 0)),
            pl.BlockSpec((TM, LANES), lambda i, j: (i, 0)),
            pl.BlockSpec((TM, LANES), lambda i, j: (i, 0)),
            pl.BlockSpec((SUBLANES, LANES), lambda i, j: (0, 0)),
        ],
        out_shape=[jax.ShapeDtypeStruct((N_TOK, D_MODEL), F32),
                   jax.ShapeDtypeStruct((N_TOK, PACKED), jnp.uint32),
                   jax.ShapeDtypeStruct((N_TOK, LANES), jnp.int32),
                   jax.ShapeDtypeStruct((N_TOK, LANES), F32),
                   jax.ShapeDtypeStruct((SUBLANES, LANES), F32)],
        scratch_shapes=[
            pltpu.VMEM((TM, D_MODEL), BF16),
            pltpu.VMEM((J1, TM, TN), BF16),
            pltpu.VMEM((J2, TM, TN), F32),
            pltpu.VMEM((1, LANES), F32),
        ],
        compiler_params=pltpu.CompilerParams(
            dimension_semantics=("arbitrary", "arbitrary"), vmem_limit_bytes=VMEM_LIMIT),
        name="mixer_out",
    )(h, h, g1, ya, yb, w_in, w_in, b_gate, b_gate, w_branch, w_branch, w_out, g2, w_router, r_bias)


def _invert_body(first_ref, second_ref, spare_ref, inv_ref, sem):
    fill = pltpu.make_async_copy(spare_ref, inv_ref, sem)
    fill.start()
    fill.wait()

    def place(c, carry):
        t0 = c * INVERT_UNROLL
        for u in range(INVERT_UNROLL):
            inv_ref[first_ref[t0 + u]] = t0 + u
            inv_ref[second_ref[t0 + u]] = t0 + u
        return carry

    lax.fori_loop(0, N_TOK // INVERT_UNROLL, place, 0)


def _invert(dest):
    spare = jnp.arange(P_SLOTS, dtype=jnp.int32) % N_TOK
    smem = pl.BlockSpec(memory_space=pltpu.SMEM)
    return pl.pallas_call(
        _invert_body,
        in_specs=[smem, smem, pl.BlockSpec(memory_space=pl.ANY)],
        out_specs=pl.BlockSpec(memory_space=pltpu.SMEM),
        out_shape=jax.ShapeDtypeStruct((P_SLOTS,), jnp.int32),
        scratch_shapes=[pltpu.SemaphoreType.DMA],
        name="invert",
    )(dest[0::TOP_K], dest[1::TOP_K], spare)


def _experts_body(l, be_ref, nx_ref, na_ref, inv_ref, xp_ref, wg_ref, wu_ref, wd_ref, y_ref,
                  xbuf0, xbuf1, xbuf2, xbuf3, gsem0, gsem1, gsem2, gsem3,
                  wg_f, wu_f, wd_f, wsem, wslot, wg_s, wu_s, wd_s):
    n_act = na_ref[0]
    bufs = ((xbuf0, gsem0), (xbuf1, gsem1), (xbuf2, gsem2), (xbuf3, gsem3))

    def weight_copies(e, s):
        return [pltpu.make_async_copy(src.at[l, e], dst.at[s], wsem.at[s])
                for src, dst in ((wg_ref, wg_f), (wu_ref, wu_f), (wd_ref, wd_f))]

    def gather(blk, r, xdst, sem):
        return pltpu.make_async_copy(
            xp_ref.at[pl.ds(inv_ref[blk * TB + r], 1)], xdst.at[pl.ds(r, 1)], sem)

    def wait_gather(xdst, sem):
        pltpu.make_async_copy(xp_ref.at[pl.ds(0, TB)], xdst, sem).wait()

    def step(b, par):
        xcur, gcur = bufs[par]
        xnxt, gnxt = bufs[(par + GATHER_AHEAD) % BLOCKS_PER_STEP]
        wait_gather(xcur, gcur)

        @pl.when((b == 0) | (be_ref[b] != be_ref[jnp.maximum(b - 1, 0)]))
        def _():
            s = wslot[0]
            for c in weight_copies(be_ref[b], s):
                c.wait()
            wg_s[...] = wg_f[s].astype(BF16)
            wu_s[...] = wu_f[s].astype(BF16)
            wd_s[...] = wd_f[s].astype(BF16)

            @pl.when(nx_ref[b] >= 0)
            def _():
                for c in weight_copies(nx_ref[b], 1 - s):
                    c.start(priority=1)

            wslot[0] = 1 - s

        ahead = jnp.minimum(b + GATHER_AHEAD, N_BLOCKS - 1)
        for r in range(TB):
            gather(ahead, r, xnxt, gnxt).start()
        x_lo, x_hi = _unpack_halves(xcur[...])
        x_lo, x_hi = x_lo.astype(BF16), x_hi.astype(BF16)
        gate = _dot(x_lo, wg_s[0:PACKED, :]) + _dot(x_hi, wg_s[PACKED:D_MODEL, :])
        up = _dot(x_lo, wu_s[0:PACKED, :]) + _dot(x_hi, wu_s[PACKED:D_MODEL, :])
        hm = (jax.nn.silu(gate) * up).astype(BF16)
        y_ref[par * TB:(par + 1) * TB, :] = _pack_halves(
            _dot(hm, wd_s[:, 0:PACKED]), _dot(hm, wd_s[:, PACKED:D_MODEL]))

    @pl.when(pl.program_id(0) == 0)
    def _():
        def first_rows(r, carry):
            for k in range(GATHER_AHEAD):
                gather(k, r, *bufs[k]).start()
            return carry
        lax.fori_loop(0, TB, first_rows, 0)
        wslot[0] = 0
        for c in weight_copies(be_ref[0], 0):
            c.start(priority=1)

    for par in range(BLOCKS_PER_STEP):
        b = BLOCKS_PER_STEP * pl.program_id(0) + par
        pl.when(b < n_act)(functools.partial(step, b, par))

        @pl.when((b >= n_act) & (b < n_act + GATHER_AHEAD))
        def _():
            wait_gather(*bufs[par])

        @pl.when((b >= n_act) & (b < N_BLOCKS))
        def _():
            y_ref[par * TB:(par + 1) * TB, :] = jnp.zeros((TB, PACKED), jnp.uint32)


def _experts(l, block_e, next_e, n_active, inv, xp, w_gate, w_up, w_down):
    row_buf = pltpu.VMEM((TB, PACKED), jnp.uint32)
    in_out = (D_MODEL, D_EXPERT)
    out_in = (D_EXPERT, D_MODEL)
    n_steps = N_BLOCKS // BLOCKS_PER_STEP
    return pl.pallas_call(
        functools.partial(_experts_body, l),
        grid_spec=pltpu.PrefetchScalarGridSpec(
            num_scalar_prefetch=4,
            grid=(n_steps + 1,),
            in_specs=[
                pl.BlockSpec(memory_space=pl.ANY),
                pl.BlockSpec(memory_space=pl.ANY),
                pl.BlockSpec(memory_space=pl.ANY),
                pl.BlockSpec(memory_space=pl.ANY),
            ],
            out_specs=pl.BlockSpec((BLOCKS_PER_STEP * TB, PACKED),
                                   lambda s, be, nx, na, iv: (jnp.minimum(s, n_steps - 1), 0)),
            scratch_shapes=[
                row_buf, row_buf, row_buf, row_buf,
                pltpu.SemaphoreType.DMA, pltpu.SemaphoreType.DMA,
                pltpu.SemaphoreType.DMA, pltpu.SemaphoreType.DMA,
                pltpu.VMEM((2,) + in_out, F32), pltpu.VMEM((2,) + in_out, F32), pltpu.VMEM((2,) + out_in, F32),
                pltpu.SemaphoreType.DMA((2,)),
                pltpu.SMEM((1,), jnp.int32),
                pltpu.VMEM(in_out, BF16), pltpu.VMEM(in_out, BF16), pltpu.VMEM(out_in, BF16),
            ],
        ),
        out_shape=jax.ShapeDtypeStruct((P_SLOTS, PACKED), jnp.uint32),
        compiler_params=pltpu.CompilerParams(
            dimension_semantics=("arbitrary",), vmem_limit_bytes=VMEM_LIMIT),
        name="experts",
    )(block_e, next_e, n_active, inv, xp, w_gate, w_up, w_down)


def _combine_body(dest_ref, h1_ref, rw_ref, gf_ref, y_ref, out_ref,
                  ya0, yb0, ya1, yb1, sem0, sem1):
    i = pl.program_id(0)
    bufs = ((ya0, yb0, sem0), (ya1, yb1, sem1))

    def fetch(tile, r, k, dst, sem):
        slot = dest_ref[(tile * TC + r) * TOP_K + k]
        return pltpu.make_async_copy(y_ref.at[pl.ds(slot, 1)], dst.at[pl.ds(r, 1)], sem)

    def step(par, last):
        ya, yb, sem = bufs[par]
        for dst in (ya, yb):
            pltpu.make_async_copy(y_ref.at[pl.ds(0, TC)], dst, sem).wait()
        if not last:
            na, nb, nsem = bufs[1 - par]
            for r in range(TC):
                fetch(i + 1, r, 0, na, nsem).start()
                fetch(i + 1, r, 1, nb, nsem).start(priority=1)
        w = rw_ref[...]
        w1, w2 = w[:, 0:1], w[:, 1:2]
        a_lo, a_hi = _unpack_halves(ya[...])
        b_lo, b_hi = _unpack_halves(yb[...])
        lo = h1_ref[:, 0:PACKED] + (w1 * a_lo + w2 * b_lo)
        hi = h1_ref[:, PACKED:D_MODEL] + (w1 * a_hi + w2 * b_hi)
        ms = (jnp.sum(lo * lo, axis=-1, keepdims=True)
              + jnp.sum(hi * hi, axis=-1, keepdims=True)) / D_MODEL
        scale = lax.rsqrt(ms + EPS)
        out_ref[:, 0:PACKED] = (lo * scale) * gf_ref[:, 0:PACKED]
        out_ref[:, PACKED:D_MODEL] = (hi * scale) * gf_ref[:, PACKED:D_MODEL]

    @pl.when(i == 0)
    def _():
        def first_rows(r, carry):
            fetch(0, r, 0, ya0, sem0).start()
            fetch(0, r, 1, yb0, sem0).start()
            return carry
        lax.fori_loop(0, TC, first_rows, 0)

    @pl.when(i % 2 == 0)
    def _():
        step(0, last=False)

    @pl.when((i % 2 == 1) & (i < N_CTILES - 1))
    def _():
        step(1, last=False)

    @pl.when(i == N_CTILES - 1)
    def _():
        step(1, last=True)


def _combine(dest, h1, rw, gf, y):
    row_buf = pltpu.VMEM((TC, PACKED), jnp.uint32)
    return pl.pallas_call(
        _combine_body,
        grid_spec=pltpu.PrefetchScalarGridSpec(
            num_scalar_prefetch=1,
            grid=(N_CTILES,),
            in_specs=[
                pl.BlockSpec((TC, D_MODEL), lambda i, d: (i, 0)),
                pl.BlockSpec((TC, LANES), lambda i, d: (i, 0)),
                pl.BlockSpec((1, D_MODEL), lambda i, d: (0, 0)),
                pl.BlockSpec(memory_space=pl.ANY),
            ],
            out_specs=pl.BlockSpec((TC, D_MODEL), lambda i, d: (i, 0)),
            scratch_shapes=[row_buf, row_buf, row_buf, row_buf,
                            pltpu.SemaphoreType.DMA, pltpu.SemaphoreType.DMA],
        ),
        out_shape=jax.ShapeDtypeStruct((N_TOK, D_MODEL), F32),
        compiler_params=pltpu.CompilerParams(
            dimension_semantics=("arbitrary",), vmem_limit_bytes=VMEM_LIMIT),
        name="combine",
    )(dest, h1, rw, gf, y)


def _route_tables(cnt, ri):
    counts = cnt[0, N_GROUPS:N_GROUPS + N_EXPERTS].astype(jnp.int32)
    padded = (counts + TB - 1) // TB * TB
    pad_end = jnp.cumsum(padded)
    pad_start = pad_end - padded
    experts = jnp.arange(N_EXPERTS, dtype=jnp.int32)
    start_of = jnp.sum(jnp.where(ri[:, 0:TOP_K, None] == experts, pad_start, 0), axis=-1)
    dest = (start_of + ri[:, TOP_K:2 * TOP_K]).reshape(N_ASSIGN)
    n_active = (pad_end[-1:] // TB).astype(jnp.int32)
    block_row = jnp.arange(N_BLOCKS, dtype=jnp.int32)[:, None] * TB
    block_e = jnp.minimum(jnp.sum((pad_end[None, :] <= block_row).astype(jnp.int32), axis=1), N_EXPERTS - 1)
    run_end = jnp.sum(jnp.where(block_e[:, None] == experts, pad_end, 0), axis=-1) // TB
    next_e = jnp.where(run_end < n_active[0], block_e[jnp.minimum(run_end, N_BLOCKS - 1)], -1)
    return dest, block_e, next_e.astype(jnp.int32), n_active


def kernel(x, norm1_g, w_in, b_gate, conv_w, ln_v_g, ln_v_b, sgu_w, sgu_b, w_branch, w_out, norm2_g,
           router_g, router_g_b, router_e, router_e_b, w_gate, w_up, w_down, final_g):
    h = x.reshape(N_TOK, D_MODEL)

    w_in_b = w_in.astype(BF16)
    w_branch_b = w_branch.astype(BF16)
    w_out_b = w_out.astype(BF16)
    g1 = norm1_g.reshape(DEPTH, 1, D_MODEL)
    g2 = norm2_g.reshape(DEPTH, 1, D_MODEL)
    gf = final_g.reshape(1, D_MODEL)
    ln_g = ln_v_g.reshape(DEPTH, 1, WIDTH)
    ln_b = ln_v_b.reshape(DEPTH, 1, WIDTH)
    bg = b_gate.reshape(DEPTH, 1, 2 * D_MODEL)
    sgu_bias = jnp.repeat(jnp.swapaxes(sgu_b, 1, 2), GROUP_DIM, axis=2)
    pad = LANES - N_GROUPS - N_EXPERTS
    w_r = jnp.concatenate([router_g, router_e, jnp.zeros((DEPTH, D_MODEL, pad), F32)], axis=2)
    wr_hi = w_r.astype(BF16)
    w_router = jnp.concatenate([wr_hi, (w_r - wr_hi.astype(F32)).astype(BF16)], axis=2)
    r_bias = jnp.concatenate([router_g_b, router_e_b, jnp.zeros((DEPTH, pad), F32)], axis=1)
    r_bias = r_bias.reshape(DEPTH, 1, LANES)

    moe = None
    for l in range(DEPTH):
        if moe is None:
            ya, yb = _mixer_in(l, TN, h, g1, w_in_b, conv_w, ln_g, ln_b, sgu_w, sgu_bias)
        else:
            ya, yb, h = _mixer_in(l, TN, None, g1, w_in_b, conv_w, ln_g, ln_b, sgu_w, sgu_bias, moe)
        h1, xp, ri, rw, cnt = _mixer_out(l, h, g1, ya, yb, w_in_b, bg, w_branch_b, w_out_b, g2,
                                         w_router, r_bias)
        dest, block_e, next_e, n_active = _route_tables(cnt, ri)
        y = _experts(l, block_e, next_e, n_active, _invert(dest), xp, w_gate, w_up, w_down)
        moe = (dest, h1, rw, y)
    return _combine(*moe[:3], gf, moe[3]).reshape(BATCH, SEQ, D_MODEL)
```

```python
import functools

import jax
import jax.numpy as jnp
from jax import lax
from jax.experimental import pallas as pl
from jax.experimental.pallas import tpu as pltpu

F32 = jnp.float32
BF16 = jnp.bfloat16

D_MODEL = 2048
BATCH = 4
SEQ = 4096
DEPTH = 2
N_TOK = BATCH * SEQ
WIDTH = D_MODEL // 2
CONV_K = 3
SGU_GROUPS = 8
CHUNK = 128
GROUP_DIM = WIDTH // SGU_GROUPS
N_GROUPS = 4
PER_GROUP = 8
N_EXPERTS = N_GROUPS * PER_GROUP
TOP_K = 2
N_ASSIGN = N_TOK * TOP_K
D_EXPERT = D_MODEL // 4
EPS = 1e-6

LANES = 128
SUBLANES = 8
VMEM_LIMIT = 56 * 1024 * 1024

COL_C, COL_B, COL_XA = 0, WIDTH, 2 * WIDTH
COL_U, COL_V = 3 * WIDTH, 4 * WIDTH
COL_GA, COL_GB = 5 * WIDTH, 5 * WIDTH + D_MODEL

TM = 512
TN = 512
TILES_PER_SEQ = SEQ // TM
J1 = D_MODEL // TN
J2 = D_MODEL // TN
J_OUT = J1 + J2

PACKED = D_MODEL // 2
assert (J2 // 2) * TN == PACKED
TB = 256
P_SLOTS = N_ASSIGN + N_EXPERTS * TB
N_BLOCKS = P_SLOTS // TB
BLOCKS_PER_STEP = 4
GATHER_AHEAD = 2
assert N_BLOCKS % BLOCKS_PER_STEP == 0 and GATHER_AHEAD < BLOCKS_PER_STEP
INVERT_UNROLL = 16
TC = 512
N_CTILES = N_TOK // TC
assert N_CTILES % 2 == 0


def _dot(a, b):
    return jnp.dot(a, b, preferred_element_type=F32)


def _pack_halves(lo, hi):
    lo_bits = pltpu.bitcast(lo.astype(BF16).astype(F32), jnp.uint32)
    hi_bits = pltpu.bitcast(hi.astype(BF16).astype(F32), jnp.uint32)
    return lax.shift_right_logical(lo_bits, jnp.uint32(16)) | (hi_bits & jnp.uint32(0xFFFF0000))


def _unpack_halves(words):
    lo = pltpu.bitcast(lax.shift_left(words, jnp.uint32(16)), F32)
    hi = pltpu.bitcast(words & jnp.uint32(0xFFFF0000), F32)
    return lo, hi


def _mixer_in_body(tn, fused, *refs):
    if fused:
        (dest_ref, h1_ref, rw_ref, y_ref, g1_ref, wc_ref, wb_ref, wa_ref, wu_ref, wv_ref, cw_ref, lng_ref,
         lnb_ref, sw_ref, sb_ref, ya_ref, yb_ref, h_ref, xn_scr, xc_scr, carry_scr, u_scr, v_scr,
         fa, fb, fsem) = refs
    else:
        (h_ref, g1_ref, wc_ref, wb_ref, wa_ref, wu_ref, wv_ref, cw_ref, lng_ref, lnb_ref, sw_ref, sb_ref,
         ya_ref, yb_ref, xn_scr, xc_scr, carry_scr, u_scr, v_scr) = refs
    jc = WIDTH // tn
    jz = WIDTH // tn
    n_tiles = N_TOK // TM
    rows_per_step = TM
    i = pl.program_id(0)
    j = pl.program_id(1)

    def fetch(tile, r, k):
        src = y_ref.at[pl.ds(dest_ref[(tile * TM + r) * TOP_K + k], 1)]
        return pltpu.make_async_copy(src, (fa, fb)[k].at[pl.ds(r, 1)], fsem)

    def wait_fetch():
        for dst in (fa, fb):
            pltpu.make_async_copy(y_ref.at[pl.ds(0, TM)], dst, fsem).wait()

    def issue_fetch(step):
        if fused:
            nxt = jnp.where(i + 1 == n_tiles, 0, i + 1)
            for r in range(step * rows_per_step, (step + 1) * rows_per_step):
                fetch(nxt, r, 0).start(priority=1)
                fetch(nxt, r, 1).start(priority=1)

    if fused:
        @pl.when((i == 0) & (j == 0))
        def _():
            def first_rows(r, carry):
                fetch(0, r, 0).start()
                fetch(0, r, 1).start()
                return carry
            lax.fori_loop(0, TM, first_rows, 0)

    @pl.when((i == 0) & (j == 0))
    def _():
        carry_scr[...] = jnp.zeros((jc, SUBLANES, tn), F32)

    @pl.when(j == 0)
    def _():
        if fused:
            wait_fetch()
            w = rw_ref[...]
            w1, w2 = w[:, 0:1], w[:, 1:2]
            a_lo, a_hi = _unpack_halves(fa[...])
            b_lo, b_hi = _unpack_halves(fb[...])
            halves = (h1_ref[:, 0:PACKED] + (w1 * a_lo + w2 * b_lo),
                      h1_ref[:, PACKED:D_MODEL] + (w1 * a_hi + w2 * b_hi))
            ms = sum(jnp.sum(v * v, axis=-1, keepdims=True) for v in halves) / D_MODEL
            scale = lax.rsqrt(ms + EPS)
            for c, v in enumerate(halves):
                cols = slice(c * PACKED, (c + 1) * PACKED)
                h_ref[:, cols] = v
                xn_scr[:, cols] = ((v * scale) * g1_ref[:, cols]).astype(BF16)
        else:
            x = h_ref[...]
            ms = jnp.mean(x * x, axis=-1, keepdims=True)
            xn_scr[...] = ((x * lax.rsqrt(ms + EPS)) * g1_ref[...]).astype(BF16)

    def conv_step(jv):
        if jv == 0:
            issue_fetch(jv)
        xn = xn_scr[...]
        xc = _dot(xn, wc_ref[...]) * _dot(xn, wa_ref[...])

        xc_scr[0:SUBLANES, :] = jnp.where(i % TILES_PER_SEQ == 0, 0.0, carry_scr[jv])
        xc_scr[SUBLANES:SUBLANES + TM, :] = xc
        x1 = xc_scr[SUBLANES - 1:SUBLANES - 1 + TM, :]
        x2 = xc_scr[SUBLANES - 2:SUBLANES - 2 + TM, :]
        cw = cw_ref[jv]
        conv = cw[0:1, :] * x2 + cw[1:2, :] * x1 + cw[2:3, :] * xc
        carry_scr[jv] = xc_scr[TM:TM + SUBLANES, :]
        ya_ref[...] = (_dot(xn, wb_ref[...]) * conv).astype(BF16)

    def gating_step(jv):
        xn = xn_scr[...]
        u_scr[jv] = jax.nn.gelu(_dot(xn, wu_ref[...]))
        v_scr[jv] = jax.nn.gelu(_dot(xn, wv_ref[...]))

    for jv in range(jc):
        pl.when(j == jv)(functools.partial(conv_step, jv))
    for jv in range(jz):
        pl.when(j == jc + jv)(functools.partial(gating_step, jv))

    @pl.when(j == jc + jz - 1)
    def _():
        s1 = jnp.zeros((TM, 1), F32)
        for k in range(jz):
            s1 = s1 + jnp.sum(v_scr[k], axis=-1, keepdims=True)
        mu = s1 / WIDTH
        s2 = jnp.zeros((TM, 1), F32)
        for k in range(jz):
            d = v_scr[k] - mu
            s2 = s2 + jnp.sum(d * d, axis=-1, keepdims=True)
        rstd = lax.rsqrt(s2 / WIDTH + EPS)
        row = lax.broadcasted_iota(jnp.int32, (CHUNK, CHUNK), 0)
        col = lax.broadcasted_iota(jnp.int32, (CHUNK, CHUNK), 1)
        causal = col <= row
        gpt = tn // GROUP_DIM
        for k in range(jz):
            vn = (((v_scr[k] - mu) * rstd) * lng_ref[:, k * tn:(k + 1) * tn]
                  + lnb_ref[:, k * tn:(k + 1) * tn]).astype(BF16)
            for gl in range(gpt):
                g = k * gpt + gl
                w = jnp.where(causal, sw_ref[g], 0.0).astype(BF16)
                bias = sb_ref[:, g * GROUP_DIM:(g + 1) * GROUP_DIM]
                for n in range(TM // CHUNK):
                    rows = slice(n * CHUNK, (n + 1) * CHUNK)
                    mixed = _dot(w, vn[rows, gl * GROUP_DIM:(gl + 1) * GROUP_DIM])
                    u = u_scr[k, rows, gl * GROUP_DIM:(gl + 1) * GROUP_DIM]
                    yb_ref[rows, g * GROUP_DIM:(g + 1) * GROUP_DIM] = (u * (mixed + bias)).astype(BF16)

    if fused:
        @pl.when((i == n_tiles - 1) & (j == jc + jz - 1))
        def _():
            wait_fetch()


def _mixer_in(l, tn, h, g1, w_in, conv_w, ln_g, ln_b, sgu_w, sgu_bias, moe=None):
    fused = moe is not None
    jc = jz = WIDTH // tn
    cj = lambda j: jnp.minimum(j, jc - 1)
    zj = lambda j: jnp.clip(j - jc, 0, jz - 1)
    wspec = lambda off, f: pl.BlockSpec((None, D_MODEL, tn), lambda i, j, *_: (l, 0, off // tn + f(j)))
    row_tile = pl.BlockSpec((TM, D_MODEL), lambda i, j, *_: (i, 0))
    in_specs = [
        pl.BlockSpec((None, 1, D_MODEL), lambda i, j, *_: (l, 0, 0)),
        wspec(COL_C, cj), wspec(COL_B, cj), wspec(COL_XA, cj), wspec(COL_U, zj), wspec(COL_V, zj),
        pl.BlockSpec((None, jc, CONV_K, tn), lambda i, j, *_: (l, 0, 0, 0)),
        pl.BlockSpec((None, 1, WIDTH), lambda i, j, *_: (l, 0, 0)),
        pl.BlockSpec((None, 1, WIDTH), lambda i, j, *_: (l, 0, 0)),
        pl.BlockSpec((None, SGU_GROUPS, CHUNK, CHUNK), lambda i, j, *_: (l, 0, 0, 0)),
        pl.BlockSpec((None, CHUNK, WIDTH), lambda i, j, *_: (l, 0, 0)),
    ]
    out_specs = [
        pl.BlockSpec((TM, tn), lambda i, j, *_: (i, cj(j))),
        pl.BlockSpec((TM, WIDTH), lambda i, j, *_: (i, 0)),
    ]
    out_shape = [jax.ShapeDtypeStruct((N_TOK, WIDTH), BF16), jax.ShapeDtypeStruct((N_TOK, WIDTH), BF16)]
    scratch = [
        pltpu.VMEM((TM, D_MODEL), BF16),
        pltpu.VMEM((TM + SUBLANES, tn), F32),
        pltpu.VMEM((jc, SUBLANES, tn), F32),
        pltpu.VMEM((jz, TM, tn), F32),
        pltpu.VMEM((jz, TM, tn), F32),
    ]
    weights = (g1, w_in, w_in, w_in, w_in, w_in, conv_w, ln_g, ln_b, sgu_w, sgu_bias)
    if fused:
        dest, h1, rw, y = moe
        in_specs = [row_tile, pl.BlockSpec((TM, LANES), lambda i, j, *_: (i, 0)),
                    pl.BlockSpec(memory_space=pl.ANY)] + in_specs
        out_specs.append(row_tile)
        out_shape.append(jax.ShapeDtypeStruct((N_TOK, D_MODEL), F32))
        scratch += [pltpu.VMEM((TM, PACKED), jnp.uint32), pltpu.VMEM((TM, PACKED), jnp.uint32),
                    pltpu.SemaphoreType.DMA]
        operands = (dest, h1, rw, y) + weights
    else:
        in_specs = [row_tile] + in_specs
        operands = (h,) + weights
    return pl.pallas_call(
        functools.partial(_mixer_in_body, tn, fused),
        grid_spec=pltpu.PrefetchScalarGridSpec(
            num_scalar_prefetch=1 if fused else 0,
            grid=(N_TOK // TM, jc + jz),
            in_specs=in_specs, out_specs=out_specs, scratch_shapes=scratch),
        out_shape=out_shape,
        compiler_params=pltpu.CompilerParams(
            dimension_semantics=("arbitrary", "arbitrary"), vmem_limit_bytes=VMEM_LIMIT),
        name="mixer_in",
    )(*operands)


def _mixer_out_body(h_ref, g1_ref, ya_ref, yb_ref, wga_ref, wgb_ref, bg_ref, wba_ref, wbb_ref,
                    wo_ref, g2_ref, wr_ref, rb_ref,
                    h1_ref, xp_ref, ri_ref, rw_ref, cnt_ref,
                    xn_scr, mg_scr, out_scr, cnt_scr):
    i = pl.program_id(0)
    j = pl.program_id(1)

    @pl.when(j == 0)
    def _():
        x = h_ref[...]
        ms = jnp.mean(x * x, axis=-1, keepdims=True)
        xn_scr[...] = ((x * lax.rsqrt(ms + EPS)) * g1_ref[...]).astype(BF16)

    @pl.when((i == 0) & (j == 0))
    def _():
        cnt_scr[...] = jnp.zeros((1, LANES), F32)

    @pl.when(j < J1)
    def _():
        xn = xn_scr[...]
        ga = jax.nn.sigmoid(_dot(xn, wga_ref[...]) + bg_ref[j])
        gb = jax.nn.sigmoid(_dot(xn, wgb_ref[...]) + bg_ref[J1 + j])
        mg_scr[j] = (ga * _dot(ya_ref[...], wba_ref[...]) + gb * _dot(yb_ref[...], wbb_ref[...])).astype(BF16)

    @pl.when(j >= J1)
    def _():
        acc = _dot(mg_scr[0], wo_ref[0:TN, :])
        for k in range(1, J1):
            acc = acc + _dot(mg_scr[k], wo_ref[k * TN:(k + 1) * TN, :])
        out_scr[j - J1] = acc

    @pl.when(j == J_OUT - 1)
    def _():
        ss = jnp.zeros((TM, 1), F32)
        for k in range(J2):
            cols = slice(k * TN, (k + 1) * TN)
            hk = h_ref[:, cols] + out_scr[k]
            h1_ref[:, cols] = hk
            ss = ss + jnp.sum(hk * hk, axis=-1, keepdims=True)
        rstd = lax.rsqrt(ss / D_MODEL + EPS)

        acc = jnp.zeros((TM, 2 * LANES), F32)
        for k in range(J2):
            cols = slice(k * TN, (k + 1) * TN)
            xk = (h1_ref[:, cols] * rstd) * g2_ref[:, cols]
            hi = xk.astype(BF16)
            lo = (xk - hi.astype(F32)).astype(BF16)
            acc = acc + (_dot(hi, wr_ref[cols, :]) + _dot(lo, wr_ref[cols, :]))
            bits = pltpu.bitcast(hi.astype(F32), jnp.uint32)
            if k < J2 // 2:
                xp_ref[:, cols] = lax.shift_right_logical(bits, jnp.uint32(16))
            else:
                pcols = slice(k * TN - PACKED, (k + 1) * TN - PACKED)
                xp_ref[:, pcols] = xp_ref[:, pcols] | (bits & jnp.uint32(0xFFFF0000))
        lg = (acc[:, 0:LANES] + acc[:, LANES:2 * LANES]) + rb_ref[...]

        lane_i = lax.broadcasted_iota(jnp.int32, (TM, LANES), 1)
        lane = lane_i.astype(F32)
        neg = -jnp.inf
        big = float(LANES)
        is_g = lane_i < N_GROUPS
        gl = jnp.where(is_g, lg, neg)
        gmax = jnp.max(gl, axis=-1, keepdims=True)
        g_idx = jnp.min(jnp.where(gl == gmax, lane, big), axis=-1, keepdims=True)
        g_w = 1.0 / jnp.sum(jnp.exp(gl - gmax), axis=-1, keepdims=True)

        first = N_GROUPS + g_idx * PER_GROUP
        in_grp = (lane >= first) & (lane < first + PER_GROUP)
        el = jnp.where(in_grp, lg, neg)
        m1 = jnp.max(el, axis=-1, keepdims=True)
        i1 = jnp.min(jnp.where(in_grp & (el == m1), lane, big), axis=-1, keepdims=True)
        el2 = jnp.where(lane == i1, neg, el)
        m2 = jnp.max(el2, axis=-1, keepdims=True)
        i2 = jnp.min(jnp.where(in_grp & (lane != i1) & (el2 == m2), lane, big), axis=-1, keepdims=True)
        t = jnp.exp(m2 - m1)
        w1 = g_w * (1.0 / (1.0 + t))
        w2 = g_w * (t / (1.0 + t))

        onehot = jnp.where((lane == i1) | (lane == i2), 1.0, 0.0)
        r_i = lax.broadcasted_iota(jnp.int32, (TM, TM), 0)
        c_i = lax.broadcasted_iota(jnp.int32, (TM, TM), 1)
        tri = jnp.where(c_i < r_i, 1.0, 0.0).astype(BF16)
        before = cnt_scr[...] + _dot(tri, onehot.astype(BF16))
        rank1 = jnp.sum(jnp.where(lane == i1, before, 0.0), axis=-1, keepdims=True)
        rank2 = jnp.sum(jnp.where(lane == i2, before, 0.0), axis=-1, keepdims=True)
        cnt_scr[...] = cnt_scr[...] + jnp.sum(onehot, axis=0, keepdims=True)

        ri = jnp.where(lane_i == 0, i1 - N_GROUPS,
                       jnp.where(lane_i == 1, i2 - N_GROUPS,
                                 jnp.where(lane_i == 2, rank1, jnp.where(lane_i == 3, rank2, 0.0))))
        ri_ref[...] = ri.astype(jnp.int32)
        rw_ref[...] = jnp.where(lane_i == 0, w1, jnp.where(lane_i == 1, w2, 0.0))
        cnt_ref[...] = jnp.broadcast_to(cnt_scr[...], (SUBLANES, LANES))


def _mixer_out(l, h, g1, ya, yb, w_in, b_gate, w_branch, w_out, g2, w_router, r_bias):
    mj = lambda j: jnp.minimum(j, J1 - 1)
    oj = lambda j: jnp.clip(j - J1, 0, J2 - 1)
    const = lambda *blk: pl.BlockSpec((None,) + blk, lambda i, j: (l,) + (0,) * len(blk))
    return pl.pallas_call(
        _mixer_out_body,
        grid=(N_TOK // TM, J_OUT),
        in_specs=[
            pl.BlockSpec((TM, D_MODEL), lambda i, j: (i, 0)),
            const(1, D_MODEL),
            pl.BlockSpec((TM, WIDTH), lambda i, j: (i, 0)),
            pl.BlockSpec((TM, WIDTH), lambda i, j: (i, 0)),
            pl.BlockSpec((None, D_MODEL, TN), lambda i, j: (l, 0, COL_GA // TN + mj(j))),
            pl.BlockSpec((None, D_MODEL, TN), lambda i, j: (l, 0, COL_GB // TN + mj(j))),
            const(2 * J1, 1, TN),
            pl.BlockSpec((None, None, WIDTH, TN), lambda i, j: (l, 0, 0, mj(j))),
            pl.BlockSpec((None, None, WIDTH, TN), lambda i, j: (l, 1, 0, mj(j))),
            pl.BlockSpec((None, D_MODEL, TN), lambda i, j: (l, 0, oj(j))),
            const(1, D_MODEL),
            const(D_MODEL, 2 * LANES), const(1, LANES),
        ],
        out_specs=[
            pl.BlockSpec((TM, D_MODEL), lambda i, j: (i, 0)),
            pl.BlockSpec((TM, PACKED), lambda i, j: (i, 0)),
            pl.BlockSpec((TM, LANES), lambda i, j: (i, 0)),
            pl.BlockSpec((TM, LANES), lambda i, j: (i, 0)),
            pl.BlockSpec((SUBLANES, LANES), lambda i, j: (0, 0)),
        ],
        out_shape=[jax.ShapeDtypeStruct((N_TOK, D_MODEL), F32),
                   jax.ShapeDtypeStruct((N_TOK, PACKED), jnp.uint32),
                   jax.ShapeDtypeStruct((N_TOK, LANES), jnp.int32),
                   jax.ShapeDtypeStruct((N_TOK, LANES), F32),
                   jax.ShapeDtypeStruct((SUBLANES, LANES), F32)],
        scratch_shapes=[
            pltpu.VMEM((TM, D_MODEL), BF16),
            pltpu.VMEM((J1, TM, TN), BF16),
            pltpu.VMEM((J2, TM, TN), F32),
            pltpu.VMEM((1, LANES), F32),
        ],
        compiler_params=pltpu.CompilerParams(
            dimension_semantics=("arbitrary", "arbitrary"), vmem_limit_bytes=VMEM_LIMIT),
        name="mixer_out",
    )(h, g1, ya, yb, w_in, w_in, b_gate, w_branch, w_branch, w_out, g2, w_router, r_bias)


def _invert_body(first_ref, second_ref, spare_ref, inv_ref, sem):
    fill = pltpu.make_async_copy(spare_ref, inv_ref, sem)
    fill.start()
    fill.wait()

    def place(c, carry):
        t0 = c * INVERT_UNROLL
        for u in range(INVERT_UNROLL):
            inv_ref[first_ref[t0 + u]] = t0 + u
            inv_ref[second_ref[t0 + u]] = t0 + u
        return carry

    lax.fori_loop(0, N_TOK // INVERT_UNROLL, place, 0)


def _invert(dest):
    spare = jnp.arange(P_SLOTS, dtype=jnp.int32) % N_TOK
    smem = pl.BlockSpec(memory_space=pltpu.SMEM)
    return pl.pallas_call(
        _invert_body,
        in_specs=[smem, smem, pl.BlockSpec(memory_space=pl.ANY)],
        out_specs=pl.BlockSpec(memory_space=pltpu.SMEM),
        out_shape=jax.ShapeDtypeStruct((P_SLOTS,), jnp.int32),
        scratch_shapes=[pltpu.SemaphoreType.DMA],
        name="invert",
    )(dest[0::TOP_K], dest[1::TOP_K], spare)


def _experts_body(l, be_ref, nx_ref, na_ref, inv_ref, xp_ref, wg_ref, wu_ref, wd_ref, y_ref,
                  xbuf0, xbuf1, xbuf2, xbuf3, gsem0, gsem1, gsem2, gsem3,
                  wg_f, wu_f, wd_f, wsem, wslot, wg_s, wu_s, wd_s):
    n_act = na_ref[0]
    bufs = ((xbuf0, gsem0), (xbuf1, gsem1), (xbuf2, gsem2), (xbuf3, gsem3))

    def weight_copies(e, s):
        return [pltpu.make_async_copy(src.at[l, e], dst.at[s], wsem.at[s])
                for src, dst in ((wg_ref, wg_f), (wu_ref, wu_f), (wd_ref, wd_f))]

    def gather(blk, r, xdst, sem):
        return pltpu.make_async_copy(
            xp_ref.at[pl.ds(inv_ref[blk * TB + r], 1)], xdst.at[pl.ds(r, 1)], sem)

    def wait_gather(xdst, sem):
        pltpu.make_async_copy(xp_ref.at[pl.ds(0, TB)], xdst, sem).wait()

    def step(b, par):
        xcur, gcur = bufs[par]
        xnxt, gnxt = bufs[(par + GATHER_AHEAD) % BLOCKS_PER_STEP]
        wait_gather(xcur, gcur)

        @pl.when((b == 0) | (be_ref[b] != be_ref[jnp.maximum(b - 1, 0)]))
        def _():
            s = wslot[0]
            for c in weight_copies(be_ref[b], s):
                c.wait()
            wg_s[...] = wg_f[s].astype(BF16)
            wu_s[...] = wu_f[s].astype(BF16)
            wd_s[...] = wd_f[s].astype(BF16)

            @pl.when(nx_ref[b] >= 0)
            def _():
                for c in weight_copies(nx_ref[b], 1 - s):
                    c.start(priority=1)

            wslot[0] = 1 - s

        ahead = jnp.minimum(b + GATHER_AHEAD, N_BLOCKS - 1)
        for r in range(TB):
            gather(ahead, r, xnxt, gnxt).start()
        x_lo, x_hi = _unpack_halves(xcur[...])
        x_lo, x_hi = x_lo.astype(BF16), x_hi.astype(BF16)
        gate = _dot(x_lo, wg_s[0:PACKED, :]) + _dot(x_hi, wg_s[PACKED:D_MODEL, :])
        up = _dot(x_lo, wu_s[0:PACKED, :]) + _dot(x_hi, wu_s[PACKED:D_MODEL, :])
        hm = (jax.nn.silu(gate) * up).astype(BF16)
        y_ref[par * TB:(par + 1) * TB, :] = _pack_halves(
            _dot(hm, wd_s[:, 0:PACKED]), _dot(hm, wd_s[:, PACKED:D_MODEL]))

    @pl.when(pl.program_id(0) == 0)
    def _():
        def first_rows(r, carry):
            for k in range(GATHER_AHEAD):
                gather(k, r, *bufs[k]).start()
            return carry
        lax.fori_loop(0, TB, first_rows, 0)
        wslot[0] = 0
        for c in weight_copies(be_ref[0], 0):
            c.start(priority=1)

    for par in range(BLOCKS_PER_STEP):
        b = BLOCKS_PER_STEP * pl.program_id(0) + par
        pl.when(b < n_act)(functools.partial(step, b, par))

        @pl.when((b >= n_act) & (b < n_act + GATHER_AHEAD))
        def _():
            wait_gather(*bufs[par])

        @pl.when((b >= n_act) & (b < N_BLOCKS))
        def _():
            y_ref[par * TB:(par + 1) * TB, :] = jnp.zeros((TB, PACKED), jnp.uint32)


def _experts(l, block_e, next_e, n_active, inv, xp, w_gate, w_up, w_down):
    row_buf = pltpu.VMEM((TB, PACKED), jnp.uint32)
    in_out = (D_MODEL, D_EXPERT)
    out_in = (D_EXPERT, D_MODEL)
    n_steps = N_BLOCKS // BLOCKS_PER_STEP
    return pl.pallas_call(
        functools.partial(_experts_body, l),
        grid_spec=pltpu.PrefetchScalarGridSpec(
            num_scalar_prefetch=4,
            grid=(n_steps + 1,),
            in_specs=[
                pl.BlockSpec(memory_space=pl.ANY),
                pl.BlockSpec(memory_space=pl.ANY),
                pl.BlockSpec(memory_space=pl.ANY),
                pl.BlockSpec(memory_space=pl.ANY),
            ],
            out_specs=pl.BlockSpec((BLOCKS_PER_STEP * TB, PACKED),
                                   lambda s, be, nx, na, iv: (jnp.minimum(s, n_steps - 1), 0)),
            scratch_shapes=[
                row_buf, row_buf, row_buf, row_buf,
                pltpu.SemaphoreType.DMA, pltpu.SemaphoreType.DMA,
                pltpu.SemaphoreType.DMA, pltpu.SemaphoreType.DMA,
                pltpu.VMEM((2,) + in_out, F32), pltpu.VMEM((2,) + in_out, F32), pltpu.VMEM((2,) + out_in, F32),
                pltpu.SemaphoreType.DMA((2,)),
                pltpu.SMEM((1,), jnp.int32),
                pltpu.VMEM(in_out, BF16), pltpu.VMEM(in_out, BF16), pltpu.VMEM(out_in, BF16),
            ],
        ),
        out_shape=jax.ShapeDtypeStruct((P_SLOTS, PACKED), jnp.uint32),
        compiler_params=pltpu.CompilerParams(
            dimension_semantics=("arbitrary",), vmem_limit_bytes=VMEM_LIMIT),
        name="experts",
    )(block_e, next_e, n_active, inv, xp, w_gate, w_up, w_down)


def _combine_body(dest_ref, h1_ref, rw_ref, gf_ref, y_ref, out_ref,
                  ya0, yb0, ya1, yb1, sem0, sem1):
    i = pl.program_id(0)
    bufs = ((ya0, yb0, sem0), (ya1, yb1, sem1))

    def fetch(tile, r, k, dst, sem):
        slot = dest_ref[(tile * TC + r) * TOP_K + k]
        return pltpu.make_async_copy(y_ref.at[pl.ds(slot, 1)], dst.at[pl.ds(r, 1)], sem)

    def step(par, last):
        ya, yb, sem = bufs[par]
        for dst in (ya, yb):
            pltpu.make_async_copy(y_ref.at[pl.ds(0, TC)], dst, sem).wait()
        if not last:
            na, nb, nsem = bufs[1 - par]
            for r in range(TC):
                fetch(i + 1, r, 0, na, nsem).start()
                fetch(i + 1, r, 1, nb, nsem).start(priority=1)
        w = rw_ref[...]
        w1, w2 = w[:, 0:1], w[:, 1:2]
        a_lo, a_hi = _unpack_halves(ya[...])
        b_lo, b_hi = _unpack_halves(yb[...])
        lo = h1_ref[:, 0:PACKED] + (w1 * a_lo + w2 * b_lo)
        hi = h1_ref[:, PACKED:D_MODEL] + (w1 * a_hi + w2 * b_hi)
        ms = (jnp.sum(lo * lo, axis=-1, keepdims=True)
              + jnp.sum(hi * hi, axis=-1, keepdims=True)) / D_MODEL
        scale = lax.rsqrt(ms + EPS)
        out_ref[:, 0:PACKED] = (lo * scale) * gf_ref[:, 0:PACKED]
        out_ref[:, PACKED:D_MODEL] = (hi * scale) * gf_ref[:, PACKED:D_MODEL]

    @pl.when(i == 0)
    def _():
        def first_rows(r, carry):
            fetch(0, r, 0, ya0, sem0).start()
            fetch(0, r, 1, yb0, sem0).start()
            return carry
        lax.fori_loop(0, TC, first_rows, 0)

    @pl.when(i % 2 == 0)
    def _():
        step(0, last=False)

    @pl.when((i % 2 == 1) & (i < N_CTILES - 1))
    def _():
        step(1, last=False)

    @pl.when(i == N_CTILES - 1)
    def _():
        step(1, last=True)


def _combine(dest, h1, rw, gf, y):
    row_buf = pltpu.VMEM((TC, PACKED), jnp.uint32)
    return pl.pallas_call(
        _combine_body,
        grid_spec=pltpu.PrefetchScalarGridSpec(
            num_scalar_prefetch=1,
            grid=(N_CTILES,),
            in_specs=[
                pl.BlockSpec((TC, D_MODEL), lambda i, d: (i, 0)),
                pl.BlockSpec((TC, LANES), lambda i, d: (i, 0)),
                pl.BlockSpec((1, D_MODEL), lambda i, d: (0, 0)),
                pl.BlockSpec(memory_space=pl.ANY),
            ],
            out_specs=pl.BlockSpec((TC, D_MODEL), lambda i, d: (i, 0)),
            scratch_shapes=[row_buf, row_buf, row_buf, row_buf,
                            pltpu.SemaphoreType.DMA, pltpu.SemaphoreType.DMA],
        ),
        out_shape=jax.ShapeDtypeStruct((N_TOK, D_MODEL), F32),
        compiler_params=pltpu.CompilerParams(
            dimension_semantics=("arbitrary",), vmem_limit_bytes=VMEM_LIMIT),
        name="combine",
    )(dest, h1, rw, gf, y)


def _route_tables(cnt, ri):
    counts = cnt[0, N_GROUPS:N_GROUPS + N_EXPERTS].astype(jnp.int32)
    padded = (counts + TB - 1) // TB * TB
    pad_end = jnp.cumsum(padded)
    pad_start = pad_end - padded
    experts = jnp.arange(N_EXPERTS, dtype=jnp.int32)
    start_of = jnp.sum(jnp.where(ri[:, 0:TOP_K, None] == experts, pad_start, 0), axis=-1)
    dest = (start_of + ri[:, TOP_K:2 * TOP_K]).reshape(N_ASSIGN)
    n_active = (pad_end[-1:] // TB).astype(jnp.int32)
    block_row = jnp.arange(N_BLOCKS, dtype=jnp.int32)[:, None] * TB
    block_e = jnp.minimum(jnp.sum((pad_end[None, :] <= block_row).astype(jnp.int32), axis=1), N_EXPERTS - 1)
    run_end = jnp.sum(jnp.where(block_e[:, None] == experts, pad_end, 0), axis=-1) // TB
    next_e = jnp.where(run_end < n_active[0], block_e[jnp.minimum(run_end, N_BLOCKS - 1)], -1)
    return dest, block_e, next_e.astype(jnp.int32), n_active


def kernel(x, norm1_g, w_in, b_gate, conv_w, ln_v_g, ln_v_b, sgu_w, sgu_b, w_branch, w_out, norm2_g,
           router_g, router_g_b, router_e, router_e_b, w_gate, w_up, w_down, final_g):
    h = x.reshape(N_TOK, D_MODEL)

    w_in_b = w_in.astype(BF16)
    w_branch_b = w_branch.astype(BF16)
    w_out_b = w_out.astype(BF16)
    g1 = norm1_g.reshape(DEPTH, 1, D_MODEL)
    conv_w = conv_w.reshape(DEPTH, CONV_K, WIDTH // TN, TN).transpose(0, 2, 1, 3)
    g2 = norm2_g.reshape(DEPTH, 1, D_MODEL)
    gf = final_g.reshape(1, D_MODEL)
    ln_g = ln_v_g.reshape(DEPTH, 1, WIDTH)
    ln_b = ln_v_b.reshape(DEPTH, 1, WIDTH)
    bg = b_gate.reshape(DEPTH, 2 * J1, 1, TN)
    sgu_bias = jnp.repeat(jnp.swapaxes(sgu_b, 1, 2), GROUP_DIM, axis=2)
    pad = LANES - N_GROUPS - N_EXPERTS
    w_r = jnp.concatenate([router_g, router_e, jnp.zeros((DEPTH, D_MODEL, pad), F32)], axis=2)
    wr_hi = w_r.astype(BF16)
    w_router = jnp.concatenate([wr_hi, (w_r - wr_hi.astype(F32)).astype(BF16)], axis=2)
    r_bias = jnp.concatenate([router_g_b, router_e_b, jnp.zeros((DEPTH, pad), F32)], axis=1)
    r_bias = r_bias.reshape(DEPTH, 1, LANES)

    moe = None
    for l in range(DEPTH):
        if moe is None:
            ya, yb = _mixer_in(l, TN, h, g1, w_in_b, conv_w, ln_g, ln_b, sgu_w, sgu_bias)
        else:
            ya, yb, h = _mixer_in(l, TN, None, g1, w_in_b, conv_w, ln_g, ln_b, sgu_w, sgu_bias, moe)
        h1, xp, ri, rw, cnt = _mixer_out(l, h, g1, ya, yb, w_in_b, bg, w_branch_b, w_out_b, g2,
                                         w_router, r_bias)
        dest, block_e, next_e, n_active = _route_tables(cnt, ri)
        y = _experts(l, block_e, next_e, n_active, _invert(dest), xp, w_gate, w_up, w_down)
        moe = (dest, h1, rw, y)
    return _combine(*moe[:3], gf, moe[3]).reshape(BATCH, SEQ, D_MODEL)
```

```python
import functools

import jax
import jax.numpy as jnp
from jax import lax
from jax.experimental import pallas as pl
from jax.experimental.pallas import tpu as pltpu

F32 = jnp.float32
BF16 = jnp.bfloat16

D_MODEL = 2048
BATCH = 4
SEQ = 4096
DEPTH = 2
N_TOK = BATCH * SEQ
WIDTH = D_MODEL // 2
CONV_K = 3
SGU_GROUPS = 8
CHUNK = 128
GROUP_DIM = WIDTH // SGU_GROUPS
N_GROUPS = 4
PER_GROUP = 8
N_EXPERTS = N_GROUPS * PER_GROUP
TOP_K = 2
N_ASSIGN = N_TOK * TOP_K
D_EXPERT = D_MODEL // 4
EPS = 1e-6

LANES = 128
SUBLANES = 8
VMEM_LIMIT = 56 * 1024 * 1024

COL_C, COL_B, COL_XA = 0, WIDTH, 2 * WIDTH
COL_U, COL_V = 3 * WIDTH, 4 * WIDTH
COL_GA, COL_GB = 5 * WIDTH, 5 * WIDTH + D_MODEL

TM = 512
TN = 512
TILES_PER_SEQ = SEQ // TM
J1 = D_MODEL // TN
J2 = D_MODEL // TN
J_OUT = J1 + J2

PACKED = D_MODEL // 2
assert (J2 // 2) * TN == PACKED
TB = 256
P_SLOTS = N_ASSIGN + N_EXPERTS * TB
N_BLOCKS = P_SLOTS // TB
BLOCKS_PER_STEP = 4
GATHER_AHEAD = 2
assert N_BLOCKS % BLOCKS_PER_STEP == 0 and GATHER_AHEAD < BLOCKS_PER_STEP
INVERT_UNROLL = 16
TC = 512
N_CTILES = N_TOK // TC
assert N_CTILES % 2 == 0


def _dot(a, b):
    return jnp.dot(a, b, preferred_element_type=F32)


def _pack_halves(lo, hi):
    lo_bits = pltpu.bitcast(lo.astype(BF16).astype(F32), jnp.uint32)
    hi_bits = pltpu.bitcast(hi.astype(BF16).astype(F32), jnp.uint32)
    return lax.shift_right_logical(lo_bits, jnp.uint32(16)) | (hi_bits & jnp.uint32(0xFFFF0000))


def _unpack_halves(words):
    lo = pltpu.bitcast(lax.shift_left(words, jnp.uint32(16)), F32)
    hi = pltpu.bitcast(words & jnp.uint32(0xFFFF0000), F32)
    return lo, hi


def _mixer_in_body(tn, fused, *refs):
    if fused:
        (dest_ref, h1_ref, rw_ref, y_ref, g1_ref, wc_ref, wb_ref, wa_ref, wu_ref, wv_ref, cw_ref, lng_ref,
         lnb_ref, sw_ref, sb_ref, ya_ref, yb_ref, h_ref, xn_scr, xc_scr, carry_scr, u_scr, v_scr,
         fa, fb, fsem) = refs
    else:
        (h_ref, g1_ref, wc_ref, wb_ref, wa_ref, wu_ref, wv_ref, cw_ref, lng_ref, lnb_ref, sw_ref, sb_ref,
         ya_ref, yb_ref, xn_scr, xc_scr, carry_scr, u_scr, v_scr) = refs
    jc = WIDTH // tn
    jz = WIDTH // tn
    n_tiles = N_TOK // TM
    rows_per_step = TM
    i = pl.program_id(0)
    j = pl.program_id(1)

    def fetch(tile, r, k):
        src = y_ref.at[pl.ds(dest_ref[(tile * TM + r) * TOP_K + k], 1)]
        return pltpu.make_async_copy(src, (fa, fb)[k].at[pl.ds(r, 1)], fsem)

    def wait_fetch():
        for dst in (fa, fb):
            pltpu.make_async_copy(y_ref.at[pl.ds(0, TM)], dst, fsem).wait()

    def issue_fetch(step):
        if fused:
            nxt = jnp.where(i + 1 == n_tiles, 0, i + 1)
            for r in range(step * rows_per_step, (step + 1) * rows_per_step):
                fetch(nxt, r, 0).start(priority=1)
                fetch(nxt, r, 1).start(priority=1)

    if fused:
        @pl.when((i == 0) & (j == 0))
        def _():
            def first_rows(r, carry):
                fetch(0, r, 0).start()
                fetch(0, r, 1).start()
                return carry
            lax.fori_loop(0, TM, first_rows, 0)

    @pl.when((i == 0) & (j == 0))
    def _():
        carry_scr[...] = jnp.zeros((jc, SUBLANES, tn), F32)

    @pl.when(j == 0)
    def _():
        if fused:
            wait_fetch()
            w = rw_ref[...]
            w1, w2 = w[:, 0:1], w[:, 1:2]
            a_lo, a_hi = _unpack_halves(fa[...])
            b_lo, b_hi = _unpack_halves(fb[...])
            halves = (h1_ref[:, 0:PACKED] + (w1 * a_lo + w2 * b_lo),
                      h1_ref[:, PACKED:D_MODEL] + (w1 * a_hi + w2 * b_hi))
            ms = sum(jnp.sum(v * v, axis=-1, keepdims=True) for v in halves) / D_MODEL
            scale = lax.rsqrt(ms + EPS)
            for c, v in enumerate(halves):
                cols = slice(c * PACKED, (c + 1) * PACKED)
                h_ref[:, cols] = v
                xn_scr[:, cols] = ((v * scale) * g1_ref[:, cols]).astype(BF16)
        else:
            x = h_ref[...]
            ms = jnp.mean(x * x, axis=-1, keepdims=True)
            xn_scr[...] = ((x * lax.rsqrt(ms + EPS)) * g1_ref[...]).astype(BF16)

    def conv_step(jv):
        if jv == 0:
            issue_fetch(jv)
        xn = xn_scr[...]
        xc = _dot(xn, wc_ref[...]) * _dot(xn, wa_ref[...])

        xc_scr[0:SUBLANES, :] = jnp.where(i % TILES_PER_SEQ == 0, 0.0, carry_scr[jv])
        xc_scr[SUBLANES:SUBLANES + TM, :] = xc
        x1 = xc_scr[SUBLANES - 1:SUBLANES - 1 + TM, :]
        x2 = xc_scr[SUBLANES - 2:SUBLANES - 2 + TM, :]
        cw = cw_ref[...]
        conv = cw[0:1, :] * x2 + cw[1:2, :] * x1 + cw[2:3, :] * xc
        carry_scr[jv] = xc_scr[TM:TM + SUBLANES, :]
        ya_ref[...] = (_dot(xn, wb_ref[...]) * conv).astype(BF16)

    def gating_step(jv):
        xn = xn_scr[...]
        u_scr[jv] = jax.nn.gelu(_dot(xn, wu_ref[...]))
        v_scr[jv] = jax.nn.gelu(_dot(xn, wv_ref[...]))

    for jv in range(jc):
        pl.when(j == jv)(functools.partial(conv_step, jv))
    for jv in range(jz):
        pl.when(j == jc + jv)(functools.partial(gating_step, jv))

    @pl.when(j == jc + jz - 1)
    def _():
        s1 = jnp.zeros((TM, 1), F32)
        for k in range(jz):
            s1 = s1 + jnp.sum(v_scr[k], axis=-1, keepdims=True)
        mu = s1 / WIDTH
        s2 = jnp.zeros((TM, 1), F32)
        for k in range(jz):
            d = v_scr[k] - mu
            s2 = s2 + jnp.sum(d * d, axis=-1, keepdims=True)
        rstd = lax.rsqrt(s2 / WIDTH + EPS)
        row = lax.broadcasted_iota(jnp.int32, (CHUNK, CHUNK), 0)
        col = lax.broadcasted_iota(jnp.int32, (CHUNK, CHUNK), 1)
        causal = col <= row
        gpt = tn // GROUP_DIM
        for k in range(jz):
            vn = (((v_scr[k] - mu) * rstd) * lng_ref[:, k * tn:(k + 1) * tn]
                  + lnb_ref[:, k * tn:(k + 1) * tn]).astype(BF16)
            for gl in range(gpt):
                g = k * gpt + gl
                w = jnp.where(causal, sw_ref[g], 0.0).astype(BF16)
                bias = sb_ref[:, g * GROUP_DIM:(g + 1) * GROUP_DIM]
                for n in range(TM // CHUNK):
                    rows = slice(n * CHUNK, (n + 1) * CHUNK)
                    mixed = _dot(w, vn[rows, gl * GROUP_DIM:(gl + 1) * GROUP_DIM])
                    u = u_scr[k, rows, gl * GROUP_DIM:(gl + 1) * GROUP_DIM]
                    yb_ref[rows, g * GROUP_DIM:(g + 1) * GROUP_DIM] = (u * (mixed + bias)).astype(BF16)

    if fused:
        @pl.when((i == n_tiles - 1) & (j == jc + jz - 1))
        def _():
            wait_fetch()


def _mixer_in(l, tn, h, g1, w_in, conv_w, ln_g, ln_b, sgu_w, sgu_bias, moe=None):
    fused = moe is not None
    jc = jz = WIDTH // tn
    cj = lambda j: jnp.minimum(j, jc - 1)
    zj = lambda j: jnp.clip(j - jc, 0, jz - 1)
    wspec = lambda off, f: pl.BlockSpec((None, D_MODEL, tn), lambda i, j, *_: (l, 0, off // tn + f(j)))
    row_tile = pl.BlockSpec((TM, D_MODEL), lambda i, j, *_: (i, 0))
    in_specs = [
        pl.BlockSpec((None, 1, D_MODEL), lambda i, j, *_: (l, 0, 0)),
        wspec(COL_C, cj), wspec(COL_B, cj), wspec(COL_XA, cj), wspec(COL_U, zj), wspec(COL_V, zj),
        pl.BlockSpec((None, CONV_K, tn), lambda i, j, *_: (l, 0, cj(j))),
        pl.BlockSpec((None, 1, WIDTH), lambda i, j, *_: (l, 0, 0)),
        pl.BlockSpec((None, 1, WIDTH), lambda i, j, *_: (l, 0, 0)),
        pl.BlockSpec((None, SGU_GROUPS, CHUNK, CHUNK), lambda i, j, *_: (l, 0, 0, 0)),
        pl.BlockSpec((None, CHUNK, WIDTH), lambda i, j, *_: (l, 0, 0)),
    ]
    out_specs = [
        pl.BlockSpec((TM, tn), lambda i, j, *_: (i, cj(j))),
        pl.BlockSpec((TM, WIDTH), lambda i, j, *_: (i, 0)),
    ]
    out_shape = [jax.ShapeDtypeStruct((N_TOK, WIDTH), BF16), jax.ShapeDtypeStruct((N_TOK, WIDTH), BF16)]
    scratch = [
        pltpu.VMEM((TM, D_MODEL), BF16),
        pltpu.VMEM((TM + SUBLANES, tn), F32),
        pltpu.VMEM((jc, SUBLANES, tn), F32),
        pltpu.VMEM((jz, TM, tn), F32),
        pltpu.VMEM((jz, TM, tn), F32),
    ]
    weights = (g1, w_in, w_in, w_in, w_in, w_in, conv_w, ln_g, ln_b, sgu_w, sgu_bias)
    if fused:
        dest, h1, rw, y = moe
        in_specs = [row_tile, pl.BlockSpec((TM, LANES), lambda i, j, *_: (i, 0)),
                    pl.BlockSpec(memory_space=pl.ANY)] + in_specs
        out_specs.append(row_tile)
        out_shape.append(jax.ShapeDtypeStruct((N_TOK, D_MODEL), F32))
        scratch += [pltpu.VMEM((TM, PACKED), jnp.uint32), pltpu.VMEM((TM, PACKED), jnp.uint32),
                    pltpu.SemaphoreType.DMA]
        operands = (dest, h1, rw, y) + weights
    else:
        in_specs = [row_tile] + in_specs
        operands = (h,) + weights
    return pl.pallas_call(
        functools.partial(_mixer_in_body, tn, fused),
        grid_spec=pltpu.PrefetchScalarGridSpec(
            num_scalar_prefetch=1 if fused else 0,
            grid=(N_TOK // TM, jc + jz),
            in_specs=in_specs, out_specs=out_specs, scratch_shapes=scratch),
        out_shape=out_shape,
        compiler_params=pltpu.CompilerParams(
            dimension_semantics=("arbitrary", "arbitrary"), vmem_limit_bytes=VMEM_LIMIT),
        name="mixer_in",
    )(*operands)


def _mixer_out_body(h_ref, g1_ref, ya_ref, yb_ref, wga_ref, wgb_ref, bga_ref, bgb_ref, wba_ref, wbb_ref,
                    wo_ref, g2_ref, wr_ref, rb_ref,
                    h1_ref, xp_ref, ri_ref, rw_ref, cnt_ref,
                    xn_scr, mg_scr, out_scr, cnt_scr):
    i = pl.program_id(0)
    j = pl.program_id(1)

    @pl.when(j == 0)
    def _():
        x = h_ref[...]
        ms = jnp.mean(x * x, axis=-1, keepdims=True)
        xn_scr[...] = ((x * lax.rsqrt(ms + EPS)) * g1_ref[...]).astype(BF16)

    @pl.when((i == 0) & (j == 0))
    def _():
        cnt_scr[...] = jnp.zeros((1, LANES), F32)

    @pl.when(j < J1)
    def _():
        xn = xn_scr[...]
        ga = jax.nn.sigmoid(_dot(xn, wga_ref[...]) + bga_ref[...])
        gb = jax.nn.sigmoid(_dot(xn, wgb_ref[...]) + bgb_ref[...])
        mg_scr[j] = (ga * _dot(ya_ref[...], wba_ref[...]) + gb * _dot(yb_ref[...], wbb_ref[...])).astype(BF16)

    @pl.when(j >= J1)
    def _():
        acc = _dot(mg_scr[0], wo_ref[0:TN, :])
        for k in range(1, J1):
            acc = acc + _dot(mg_scr[k], wo_ref[k * TN:(k + 1) * TN, :])
        out_scr[j - J1] = acc

    @pl.when(j == J_OUT - 1)
    def _():
        ss = jnp.zeros((TM, 1), F32)
        for k in range(J2):
            cols = slice(k * TN, (k + 1) * TN)
            hk = h_ref[:, cols] + out_scr[k]
            h1_ref[:, cols] = hk
            ss = ss + jnp.sum(hk * hk, axis=-1, keepdims=True)
        rstd = lax.rsqrt(ss / D_MODEL + EPS)

        acc = jnp.zeros((TM, 2 * LANES), F32)
        for k in range(J2):
            cols = slice(k * TN, (k + 1) * TN)
            xk = (h1_ref[:, cols] * rstd) * g2_ref[:, cols]
            hi = xk.astype(BF16)
            lo = (xk - hi.astype(F32)).astype(BF16)
            acc = acc + (_dot(hi, wr_ref[cols, :]) + _dot(lo, wr_ref[cols, :]))
            bits = pltpu.bitcast(hi.astype(F32), jnp.uint32)
            if k < J2 // 2:
                xp_ref[:, cols] = lax.shift_right_logical(bits, jnp.uint32(16))
            else:
                pcols = slice(k * TN - PACKED, (k + 1) * TN - PACKED)
                xp_ref[:, pcols] = xp_ref[:, pcols] | (bits & jnp.uint32(0xFFFF0000))
        lg = (acc[:, 0:LANES] + acc[:, LANES:2 * LANES]) + rb_ref[...]

        lane_i = lax.broadcasted_iota(jnp.int32, (TM, LANES), 1)
        lane = lane_i.astype(F32)
        neg = -jnp.inf
        big = float(LANES)
        is_g = lane_i < N_GROUPS
        gl = jnp.where(is_g, lg, neg)
        gmax = jnp.max(gl, axis=-1, keepdims=True)
        g_idx = jnp.min(jnp.where(gl == gmax, lane, big), axis=-1, keepdims=True)
        g_w = 1.0 / jnp.sum(jnp.exp(gl - gmax), axis=-1, keepdims=True)

        first = N_GROUPS + g_idx * PER_GROUP
        in_grp = (lane >= first) & (lane < first + PER_GROUP)
        el = jnp.where(in_grp, lg, neg)
        m1 = jnp.max(el, axis=-1, keepdims=True)
        i1 = jnp.min(jnp.where(in_grp & (el == m1), lane, big), axis=-1, keepdims=True)
        el2 = jnp.where(lane == i1, neg, el)
        m2 = jnp.max(el2, axis=-1, keepdims=True)
        i2 = jnp.min(jnp.where(in_grp & (lane != i1) & (el2 == m2), lane, big), axis=-1, keepdims=True)
        t = jnp.exp(m2 - m1)
        w1 = g_w * (1.0 / (1.0 + t))
        w2 = g_w * (t / (1.0 + t))

        onehot = jnp.where((lane == i1) | (lane == i2), 1.0, 0.0)
        r_i = lax.broadcasted_iota(jnp.int32, (TM, TM), 0)
        c_i = lax.broadcasted_iota(jnp.int32, (TM, TM), 1)
        tri = jnp.where(c_i < r_i, 1.0, 0.0).astype(BF16)
        before = cnt_scr[...] + _dot(tri, onehot.astype(BF16))
        rank1 = jnp.sum(jnp.where(lane == i1, before, 0.0), axis=-1, keepdims=True)
        rank2 = jnp.sum(jnp.where(lane == i2, before, 0.0), axis=-1, keepdims=True)
        cnt_scr[...] = cnt_scr[...] + jnp.sum(onehot, axis=0, keepdims=True)

        ri = jnp.where(lane_i == 0, i1 - N_GROUPS,
                       jnp.where(lane_i == 1, i2 - N_GROUPS,
                                 jnp.where(lane_i == 2, rank1, jnp.where(lane_i == 3, rank2, 0.0))))
        ri_ref[...] = ri.astype(jnp.int32)
        rw_ref[...] = jnp.where(lane_i == 0, w1, jnp.where(lane_i == 1, w2, 0.0))
        cnt_ref[...] = jnp.broadcast_to(cnt_scr[...], (SUBLANES, LANES))


def _mixer_out(l, h, g1, ya, yb, w_in, b_gate, w_branch, w_out, g2, w_router, r_bias):
    mj = lambda j: jnp.minimum(j, J1 - 1)
    oj = lambda j: jnp.clip(j - J1, 0, J2 - 1)
    const = lambda *blk: pl.BlockSpec((None,) + blk, lambda i, j: (l,) + (0,) * len(blk))
    return pl.pallas_call(
        _mixer_out_body,
        grid=(N_TOK // TM, J_OUT),
        in_specs=[
            pl.BlockSpec((TM, D_MODEL), lambda i, j: (i, 0)),
            const(1, D_MODEL),
            pl.BlockSpec((TM, WIDTH), lambda i, j: (i, 0)),
            pl.BlockSpec((TM, WIDTH), lambda i, j: (i, 0)),
            pl.BlockSpec((None, D_MODEL, TN), lambda i, j: (l, 0, COL_GA // TN + mj(j))),
            pl.BlockSpec((None, D_MODEL, TN), lambda i, j: (l, 0, COL_GB // TN + mj(j))),
            pl.BlockSpec((None, 1, TN), lambda i, j: (l, 0, mj(j))),
            pl.BlockSpec((None, 1, TN), lambda i, j: (l, 0, D_MODEL // TN + mj(j))),
            pl.BlockSpec((None, None, WIDTH, TN), lambda i, j: (l, 0, 0, mj(j))),
            pl.BlockSpec((None, None, WIDTH, TN), lambda i, j: (l, 1, 0, mj(j))),
            pl.BlockSpec((None, D_MODEL, TN), lambda i, j: (l, 0, oj(j))),
            const(1, D_MODEL),
            const(D_MODEL, 2 * LANES), const(1, LANES),
        ],
        out_specs=[
            pl.BlockSpec((TM, D_MODEL), lambda i, j: (i, 0)),
            pl.BlockSpec((TM, PACKED), lambda i, j: (i, 0)),
            pl.BlockSpec((TM, LANES), lambda i, j: (i, 0)),
            pl.BlockSpec((TM, LANES), lambda i, j: (i, 0)),
            pl.BlockSpec((SUBLANES, LANES), lambda i, j: (0, 0)),
        ],
        out_shape=[jax.ShapeDtypeStruct((N_TOK, D_MODEL), F32),
                   jax.ShapeDtypeStruct((N_TOK, PACKED), jnp.uint32),
                   jax.ShapeDtypeStruct((N_TOK, LANES), jnp.int32),
                   jax.ShapeDtypeStruct((N_TOK, LANES), F32),
                   jax.ShapeDtypeStruct((SUBLANES, LANES), F32)],
        scratch_shapes=[
            pltpu.VMEM((TM, D_MODEL), BF16),
            pltpu.VMEM((J1, TM, TN), BF16),
            pltpu.VMEM((J2, TM, TN), F32),
            pltpu.VMEM((1, LANES), F32),
        ],
        compiler_params=pltpu.CompilerParams(
            dimension_semantics=("arbitrary", "arbitrary"), vmem_limit_bytes=VMEM_LIMIT),
        name="mixer_out",
    )(h, g1, ya, yb, w_in, w_in, b_gate, b_gate, w_branch, w_branch, w_out, g2, w_router, r_bias)


def _invert_body(first_ref, second_ref, spare_ref, inv_ref, sem):
    fill = pltpu.make_async_copy(spare_ref, inv_ref, sem)
    fill.start()
    fill.wait()

    def place(c, carry):
        t0 = c * INVERT_UNROLL
        for u in range(INVERT_UNROLL):
            inv_ref[first_ref[t0 + u]] = t0 + u
            inv_ref[second_ref[t0 + u]] = t0 + u
        return carry

    lax.fori_loop(0, N_TOK // INVERT_UNROLL, place, 0)


def _invert(dest):
    spare = jnp.arange(P_SLOTS, dtype=jnp.int32) % N_TOK
    smem = pl.BlockSpec(memory_space=pltpu.SMEM)
    return pl.pallas_call(
        _invert_body,
        in_specs=[smem, smem, pl.BlockSpec(memory_space=pl.ANY)],
        out_specs=pl.BlockSpec(memory_space=pltpu.SMEM),
        out_shape=jax.ShapeDtypeStruct((P_SLOTS,), jnp.int32),
        scratch_shapes=[pltpu.SemaphoreType.DMA],
        name="invert",
    )(dest[0::TOP_K], dest[1::TOP_K], spare)


def _experts_body(l, be_ref, nx_ref, na_ref, inv_ref, xp_ref, wg_ref, wu_ref, wd_ref, y_ref,
                  xbuf0, xbuf1, xbuf2, xbuf3, gsem0, gsem1, gsem2, gsem3,
                  wg_f, wu_f, wd_f, wsem, wslot, wg_s, wu_s, wd_s):
    n_act = na_ref[0]
    bufs = ((xbuf0, gsem0), (xbuf1, gsem1), (xbuf2, gsem2), (xbuf3, gsem3))

    def weight_copies(e, s):
        return [pltpu.make_async_copy(src.at[l, e], dst.at[s], wsem.at[s])
                for src, dst in ((wg_ref, wg_f), (wu_ref, wu_f), (wd_ref, wd_f))]

    def gather(blk, r, xdst, sem):
        return pltpu.make_async_copy(
            xp_ref.at[pl.ds(inv_ref[blk * TB + r], 1)], xdst.at[pl.ds(r, 1)], sem)

    def wait_gather(xdst, sem):
        pltpu.make_async_copy(xp_ref.at[pl.ds(0, TB)], xdst, sem).wait()

    def step(b, par):
        xcur, gcur = bufs[par]
        xnxt, gnxt = bufs[(par + GATHER_AHEAD) % BLOCKS_PER_STEP]
        wait_gather(xcur, gcur)

        @pl.when((b == 0) | (be_ref[b] != be_ref[jnp.maximum(b - 1, 0)]))
        def _():
            s = wslot[0]
            for c in weight_copies(be_ref[b], s):
                c.wait()
            wg_s[...] = wg_f[s].astype(BF16)
            wu_s[...] = wu_f[s].astype(BF16)
            wd_s[...] = wd_f[s].astype(BF16)

            @pl.when(nx_ref[b] >= 0)
            def _():
                for c in weight_copies(nx_ref[b], 1 - s):
                    c.start(priority=1)

            wslot[0] = 1 - s

        ahead = jnp.minimum(b + GATHER_AHEAD, N_BLOCKS - 1)
        for r in range(TB):
            gather(ahead, r, xnxt, gnxt).start(priority=r % 2)
        x_lo, x_hi = _unpack_halves(xcur[...])
        x_lo, x_hi = x_lo.astype(BF16), x_hi.astype(BF16)
        gate = _dot(x_lo, wg_s[0:PACKED, :]) + _dot(x_hi, wg_s[PACKED:D_MODEL, :])
        up = _dot(x_lo, wu_s[0:PACKED, :]) + _dot(x_hi, wu_s[PACKED:D_MODEL, :])
        hm = (jax.nn.silu(gate) * up).astype(BF16)
        y_ref[par * TB:(par + 1) * TB, :] = _pack_halves(
            _dot(hm, wd_s[:, 0:PACKED]), _dot(hm, wd_s[:, PACKED:D_MODEL]))

    @pl.when(pl.program_id(0) == 0)
    def _():
        def first_rows(r, carry):
            for k in range(GATHER_AHEAD):
                gather(k, r, *bufs[k]).start()
            return carry
        lax.fori_loop(0, TB, first_rows, 0)
        wslot[0] = 0
        for c in weight_copies(be_ref[0], 0):
            c.start(priority=1)

    for par in range(BLOCKS_PER_STEP):
        b = BLOCKS_PER_STEP * pl.program_id(0) + par
        pl.when(b < n_act)(functools.partial(step, b, par))

        @pl.when((b >= n_act) & (b < n_act + GATHER_AHEAD))
        def _():
            wait_gather(*bufs[par])

        @pl.when((b >= n_act) & (b < N_BLOCKS))
        def _():
            y_ref[par * TB:(par + 1) * TB, :] = jnp.zeros((TB, PACKED), jnp.uint32)


def _experts(l, block_e, next_e, n_active, inv, xp, w_gate, w_up, w_down):
    row_buf = pltpu.VMEM((TB, PACKED), jnp.uint32)
    in_out = (D_MODEL, D_EXPERT)
    out_in = (D_EXPERT, D_MODEL)
    n_steps = N_BLOCKS // BLOCKS_PER_STEP
    return pl.pallas_call(
        functools.partial(_experts_body, l),
        grid_spec=pltpu.PrefetchScalarGridSpec(
            num_scalar_prefetch=4,
            grid=(n_steps + 1,),
            in_specs=[
                pl.BlockSpec(memory_space=pl.ANY),
                pl.BlockSpec(memory_space=pl.ANY),
                pl.BlockSpec(memory_space=pl.ANY),
                pl.BlockSpec(memory_space=pl.ANY),
            ],
            out_specs=pl.BlockSpec((BLOCKS_PER_STEP * TB, PACKED),
                                   lambda s, be, nx, na, iv: (jnp.minimum(s, n_steps - 1), 0)),
            scratch_shapes=[
                row_buf, row_buf, row_buf, row_buf,
                pltpu.SemaphoreType.DMA, pltpu.SemaphoreType.DMA,
                pltpu.SemaphoreType.DMA, pltpu.SemaphoreType.DMA,
                pltpu.VMEM((2,) + in_out, F32), pltpu.VMEM((2,) + in_out, F32), pltpu.VMEM((2,) + out_in, F32),
                pltpu.SemaphoreType.DMA((2,)),
                pltpu.SMEM((1,), jnp.int32),
                pltpu.VMEM(in_out, BF16), pltpu.VMEM(in_out, BF16), pltpu.VMEM(out_in, BF16),
            ],
        ),
        out_shape=jax.ShapeDtypeStruct((P_SLOTS, PACKED), jnp.uint32),
        compiler_params=pltpu.CompilerParams(
            dimension_semantics=("arbitrary",), vmem_limit_bytes=VMEM_LIMIT),
        name="experts",
    )(block_e, next_e, n_active, inv, xp, w_gate, w_up, w_down)


def _combine_body(dest_ref, h1_ref, rw_ref, gf_ref, y_ref, out_ref,
                  ya0, yb0, ya1, yb1, sem0, sem1):
    i = pl.program_id(0)
    bufs = ((ya0, yb0, sem0), (ya1, yb1, sem1))

    def fetch(tile, r, k, dst, sem):
        slot = dest_ref[(tile * TC + r) * TOP_K + k]
        return pltpu.make_async_copy(y_ref.at[pl.ds(slot, 1)], dst.at[pl.ds(r, 1)], sem)

    def step(par, last):
        ya, yb, sem = bufs[par]
        for dst in (ya, yb):
            pltpu.make_async_copy(y_ref.at[pl.ds(0, TC)], dst, sem).wait()
        if not last:
            na, nb, nsem = bufs[1 - par]
            for r in range(TC):
                fetch(i + 1, r, 0, na, nsem).start()
                fetch(i + 1, r, 1, nb, nsem).start(priority=1)
        w = rw_ref[...]
        w1, w2 = w[:, 0:1], w[:, 1:2]
        a_lo, a_hi = _unpack_halves(ya[...])
        b_lo, b_hi = _unpack_halves(yb[...])
        lo = h1_ref[:, 0:PACKED] + (w1 * a_lo + w2 * b_lo)
        hi = h1_ref[:, PACKED:D_MODEL] + (w1 * a_hi + w2 * b_hi)
        ms = (jnp.sum(lo * lo, axis=-1, keepdims=True)
              + jnp.sum(hi * hi, axis=-1, keepdims=True)) / D_MODEL
        scale = lax.rsqrt(ms + EPS)
        out_ref[:, 0:PACKED] = (lo * scale) * gf_ref[:, 0:PACKED]
        out_ref[:, PACKED:D_MODEL] = (hi * scale) * gf_ref[:, PACKED:D_MODEL]

    @pl.when(i == 0)
    def _():
        def first_rows(r, carry):
            fetch(0, r, 0, ya0, sem0).start()
            fetch(0, r, 1, yb0, sem0).start()
            return carry
        lax.fori_loop(0, TC, first_rows, 0)

    @pl.when(i % 2 == 0)
    def _():
        step(0, last=False)

    @pl.when((i % 2 == 1) & (i < N_CTILES - 1))
    def _():
        step(1, last=False)

    @pl.when(i == N_CTILES - 1)
    def _():
        step(1, last=True)


def _combine(dest, h1, rw, gf, y):
    row_buf = pltpu.VMEM((TC, PACKED), jnp.uint32)
    return pl.pallas_call(
        _combine_body,
        grid_spec=pltpu.PrefetchScalarGridSpec(
            num_scalar_prefetch=1,
            grid=(N_CTILES,),
            in_specs=[
                pl.BlockSpec((TC, D_MODEL), lambda i, d: (i, 0)),
                pl.BlockSpec((TC, LANES), lambda i, d: (i, 0)),
                pl.BlockSpec((1, D_MODEL), lambda i, d: (0, 0)),
                pl.BlockSpec(memory_space=pl.ANY),
            ],
            out_specs=pl.BlockSpec((TC, D_MODEL), lambda i, d: (i, 0)),
            scratch_shapes=[row_buf, row_buf, row_buf, row_buf,
                            pltpu.SemaphoreType.DMA, pltpu.SemaphoreType.DMA],
        ),
        out_shape=jax.ShapeDtypeStruct((N_TOK, D_MODEL), F32),
        compiler_params=pltpu.CompilerParams(
            dimension_semantics=("arbitrary",), vmem_limit_bytes=VMEM_LIMIT),
        name="combine",
    )(dest, h1, rw, gf, y)


def _route_tables(cnt, ri):
    counts = cnt[0, N_GROUPS:N_GROUPS + N_EXPERTS].astype(jnp.int32)
    padded = (counts + TB - 1) // TB * TB
    pad_end = jnp.cumsum(padded)
    pad_start = pad_end - padded
    experts = jnp.arange(N_EXPERTS, dtype=jnp.int32)
    start_of = jnp.sum(jnp.where(ri[:, 0:TOP_K, None] == experts, pad_start, 0), axis=-1)
    dest = (start_of + ri[:, TOP_K:2 * TOP_K]).reshape(N_ASSIGN)
    n_active = (pad_end[-1:] // TB).astype(jnp.int32)
    block_row = jnp.arange(N_BLOCKS, dtype=jnp.int32)[:, None] * TB
    block_e = jnp.minimum(jnp.sum((pad_end[None, :] <= block_row).astype(jnp.int32), axis=1), N_EXPERTS - 1)
    run_end = jnp.sum(jnp.where(block_e[:, None] == experts, pad_end, 0), axis=-1) // TB
    next_e = jnp.where(run_end < n_active[0], block_e[jnp.minimum(run_end, N_BLOCKS - 1)], -1)
    return dest, block_e, next_e.astype(jnp.int32), n_active


def kernel(x, norm1_g, w_in, b_gate, conv_w, ln_v_g, ln_v_b, sgu_w, sgu_b, w_branch, w_out, norm2_g,
           router_g, router_g_b, router_e, router_e_b, w_gate, w_up, w_down, final_g):
    h = x.reshape(N_TOK, D_MODEL)

    w_in_b = w_in.astype(BF16)
    w_branch_b = w_branch.astype(BF16)
    w_out_b = w_out.astype(BF16)
    g1 = norm1_g.reshape(DEPTH, 1, D_MODEL)
    g2 = norm2_g.reshape(DEPTH, 1, D_MODEL)
    gf = final_g.reshape(1, D_MODEL)
    ln_g = ln_v_g.reshape(DEPTH, 1, WIDTH)
    ln_b = ln_v_b.reshape(DEPTH, 1, WIDTH)
    bg = b_gate.reshape(DEPTH, 1, 2 * D_MODEL)
    sgu_bias = jnp.repeat(jnp.swapaxes(sgu_b, 1, 2), GROUP_DIM, axis=2)
    pad = LANES - N_GROUPS - N_EXPERTS
    w_r = jnp.concatenate([router_g, router_e, jnp.zeros((DEPTH, D_MODEL, pad), F32)], axis=2)
    wr_hi = w_r.astype(BF16)
    w_router = jnp.concatenate([wr_hi, (w_r - wr_hi.astype(F32)).astype(BF16)], axis=2)
    r_bias = jnp.concatenate([router_g_b, router_e_b, jnp.zeros((DEPTH, pad), F32)], axis=1)
    r_bias = r_bias.reshape(DEPTH, 1, LANES)

    moe = None
    for l in range(DEPTH):
        if moe is None:
            ya, yb = _mixer_in(l, TN, h, g1, w_in_b, conv_w, ln_g, ln_b, sgu_w, sgu_bias)
        else:
            ya, yb, h = _mixer_in(l, TN, None, g1, w_in_b, conv_w, ln_g, ln_b, sgu_w, sgu_bias, moe)
        h1, xp, ri, rw, cnt = _mixer_out(l, h, g1, ya, yb, w_in_b, bg, w_branch_b, w_out_b, g2,
                                         w_router, r_bias)
        dest, block_e, next_e, n_active = _route_tables(cnt, ri)
        y = _experts(l, block_e, next_e, n_active, _invert(dest), xp, w_gate, w_up, w_down)
        moe = (dest, h1, rw, y)
    return _combine(*moe[:3], gf, moe[3]).reshape(BATCH, SEQ, D_MODEL)
```
